```python
import jax
import jax.numpy as jnp
from jax import lax
import numpy as np

D_MODEL = 1024
BATCH = 4
SEQ = 8192
DEPTH = 4


HEAD_DIM = 64
N_MIX_HEADS = D_MODEL // HEAD_DIM
HEADS_A = (3 * N_MIX_HEADS) // 8
HEADS_B = (3 * N_MIX_HEADS) // 8
HEADS_C = N_MIX_HEADS - HEADS_A - HEADS_B
Q_LORA = D_MODEL // 4
KV_LORA = D_MODEL // 8
QK_NOPE = HEAD_DIM
QK_ROPE = HEAD_DIM // 2
V_DIM_A = HEAD_DIM
DILATED_PAIRS = ((128, 1), (512, 4), (2048, 16))
GRID_W = 64
NA_ROWS = 8
NA_COLS = 16
NA_QCOLS = 16
NA_KCOLS = 2 * NA_COLS
D_FF = 4 * D_MODEL
ROPE_THETA = 10000.0
Q_BLOCK = 128
NORM_EPS = 1e-6
NEG_INF = -1e30

COLS_A = Q_LORA + KV_LORA + QK_ROPE
COLS_B = 3 * HEADS_B * HEAD_DIM
COLS_C = 3 * HEADS_C * HEAD_DIM
IN_COLS = COLS_A + COLS_B + COLS_C
WIDTH_A = HEADS_A * V_DIM_A
WIDTH_B = HEADS_B * HEAD_DIM
WIDTH_C = HEADS_C * HEAD_DIM
MIX_WIDTH = WIDTH_A + WIDTH_B + WIDTH_C

kernel_name = "hybrid_mla_dilated_natten_encoder"


def rms_norm(x, g):
    xf = x.astype(jnp.float32)
    y = xf * lax.rsqrt(jnp.mean(xf * xf, axis=-1, keepdims=True) + NORM_EPS)
    return (y * g.astype(jnp.float32)).astype(x.dtype)


def rope(x, pos):
    half = x.shape[-1] // 2
    inv_freq = ROPE_THETA ** (-jnp.arange(half, dtype=jnp.float32) / half)
    ang = pos[:, None] * inv_freq[None, :]
    cos = jnp.cos(ang)[None, :, None, :]
    sin = jnp.sin(ang)[None, :, None, :]
    x1 = x[..., :half].astype(jnp.float32)
    x2 = x[..., half:].astype(jnp.float32)
    return jnp.concatenate([x1 * cos - x2 * sin, x1 * sin + x2 * cos], axis=-1).astype(x.dtype)


def dense_attention(q, k, v):
    b, s, h, dq = q.shape
    scale = dq ** -0.5
    qb = q.reshape(b, s // Q_BLOCK, Q_BLOCK, h, dq).transpose(1, 0, 2, 3, 4)

    def one_block(q_blk):
        sc = jnp.einsum('bqhd,bkhd->bhqk', q_blk, k, preferred_element_type=jnp.float32) * scale
        p = jax.nn.softmax(sc, axis=-1)
        return jnp.einsum('bhqk,bkhd->bqhd', p.astype(v.dtype), v)

    o = lax.map(one_block, qb)
    return o.transpose(1, 0, 2, 3, 4).reshape(b, s, h, v.shape[-1])


def banded_attention(q, k, v, half):
    bq, n, h, d = q.shape
    blk = half
    nb = -(-n // blk)
    n_pad = nb * blk
    qp = jnp.pad(q, ((0, 0), (0, n_pad - n), (0, 0), (0, 0))).reshape(bq, nb, blk, h, d)
    pad_kv = ((0, 0), (blk, n_pad - n + blk), (0, 0), (0, 0))
    kp = jnp.pad(k, pad_kv).reshape(bq, nb + 2, blk, h, d)
    vp = jnp.pad(v, pad_kv).reshape(bq, nb + 2, blk, h, d)
    kw = jnp.concatenate([kp[:, :-2], kp[:, 1:-1], kp[:, 2:]], axis=2)
    vw = jnp.concatenate([vp[:, :-2], vp[:, 1:-1], vp[:, 2:]], axis=2)
    q_idx = jnp.arange(n_pad).reshape(nb, blk)
    k_idx = jnp.arange(nb)[:, None] * blk - blk + jnp.arange(3 * blk)[None, :]
    mask = ((jnp.abs(q_idx[:, :, None] - k_idx[:, None, :]) <= half)
            & (k_idx[:, None, :] >= 0) & (k_idx[:, None, :] < n))
    sc = jnp.einsum('bnqhd,bnkhd->bnhqk', qp, kw, preferred_element_type=jnp.float32) * (d ** -0.5)
    sc = jnp.where(mask[None, :, None], sc, NEG_INF)
    m = jnp.max(sc, axis=-1, keepdims=True)
    p = jnp.exp(sc - m)
    den = jnp.sum(p, axis=-1)
    o = jnp.einsum('bnhqk,bnkhd->bnqhd', p.astype(v.dtype), vw).astype(jnp.float32)
    o = o / den.transpose(0, 1, 3, 2)[..., None]
    lse = (m[..., 0] + jnp.log(den)).transpose(0, 1, 3, 2)
    o = o.reshape(bq, n_pad, h, d)[:, :n]
    lse = lse.reshape(bq, n_pad, h)[:, :n]
    return o, lse


def dilated_sliding_attention(q, k, v):
    b, s, h, d = q.shape
    outs, lses = [], []
    for window, dil in DILATED_PAIRS:
        n = s // dil
        qc = q.reshape(b, n, dil, h, d).transpose(0, 2, 1, 3, 4).reshape(b * dil, n, h, d)
        kc = k.reshape(b, n, dil, h, d).transpose(0, 2, 1, 3, 4).reshape(b * dil, n, h, d)
        vc = v.reshape(b, n, dil, h, d).transpose(0, 2, 1, 3, 4).reshape(b * dil, n, h, d)
        o, lse = banded_attention(qc, kc, vc, window // (2 * dil))
        outs.append(o.reshape(b, dil, n, h, d).transpose(0, 2, 1, 3, 4).reshape(b, s, h, d))
        lses.append(lse.reshape(b, dil, n, h).transpose(0, 2, 1, 3).reshape(b, s, h))
    w = jax.nn.softmax(jnp.stack(lses, axis=-1), axis=-1)
    o = jnp.sum(jnp.stack(outs, axis=-1) * w[:, :, :, None, :], axis=-1)
    return o.astype(q.dtype)


def neighbourhood_attention(q, k, v, rpb):
    b, s, h, d = q.shape
    rows = s // GRID_W
    kr_win = min(NA_ROWS, rows)
    q_rows = kr_win
    k_rows = min(2 * kr_win, rows)
    nrb = -(-rows // q_rows)
    rows_pad = nrb * q_rows
    ncb = GRID_W // NA_QCOLS
    qg = jnp.pad(q.reshape(b, rows, GRID_W, h, d), ((0, 0), (0, rows_pad - rows), (0, 0), (0, 0), (0, 0)))
    qg = qg.reshape(b, nrb, q_rows, ncb, NA_QCOLS, h, d).transpose(0, 1, 3, 2, 4, 5, 6)
    r_q = jnp.arange(rows_pad).reshape(nrb, q_rows)
    r_start = jnp.clip(r_q - kr_win // 2, 0, rows - kr_win)
    c_q = jnp.arange(GRID_W).reshape(ncb, NA_QCOLS)
    c_start = jnp.clip(c_q - NA_COLS // 2, 0, GRID_W - NA_COLS)
    kr_idx = (jnp.clip(jnp.arange(nrb) * q_rows - kr_win // 2, 0, rows - k_rows)[:, None]
              + jnp.arange(k_rows)[None, :])
    kc_idx = (jnp.clip(jnp.arange(ncb) * NA_QCOLS - NA_COLS // 2, 0, GRID_W - NA_KCOLS)[:, None]
              + jnp.arange(NA_KCOLS)[None, :])
    kgrid = k.reshape(b, rows, GRID_W, h, d)
    vgrid = v.reshape(b, rows, GRID_W, h, d)
    ri = kr_idx[:, None, :, None]
    ci = kc_idx[None, :, None, :]
    kg = kgrid[:, ri, ci]
    vg = vgrid[:, ri, ci]
    row_ok = (kr_idx[:, None, :] >= r_start[:, :, None]) & (kr_idx[:, None, :] < r_start[:, :, None] + kr_win)
    col_ok = (kc_idx[:, None, :] >= c_start[:, :, None]) & (kc_idx[:, None, :] < c_start[:, :, None] + NA_COLS)
    dr = jnp.clip(kr_idx[:, None, :] - r_q[:, :, None], -(NA_ROWS - 1), NA_ROWS - 1) + (NA_ROWS - 1)
    dc = jnp.clip(kc_idx[:, None, :] - c_q[:, :, None], -(NA_COLS - 1), NA_COLS - 1) + (NA_COLS - 1)
    bias = rpb[:, dr[:, None, :, None, :, None], dc[None, :, None, :, None, :]]
    mask = row_ok[:, None, :, None, :, None] & col_ok[None, :, None, :, None, :]
    sc = jnp.einsum('bnmiphd,bnmjqhd->bnmhipjq', qg, kg, preferred_element_type=jnp.float32) * (d ** -0.5)
    sc = sc + bias.transpose(1, 2, 0, 3, 4, 5, 6).astype(jnp.float32)[None]
    sc = jnp.where(mask[:, :, None][None], sc, NEG_INF)
    shp = sc.shape
    p = jax.nn.softmax(sc.reshape(shp[:-2] + (k_rows * NA_KCOLS,)), axis=-1).reshape(shp)
    o = jnp.einsum('bnmhipjq,bnmjqhd->bnmiphd', p.astype(v.dtype), vg)
    o = o.transpose(0, 1, 3, 2, 4, 5, 6).reshape(b, rows_pad, GRID_W, h, d)[:, :rows]
    return o.reshape(b, s, h, d)


def setup_inputs(seed: int = 0) -> dict:
    key = jax.random.key(seed)
    ks = jax.random.split(key, 16)

    def normal(k, shape, scale):
        return jax.random.normal(k, shape, dtype=jnp.float32) * scale

    def gain(k, shape):
        return 1.0 + 0.02 * jax.random.normal(k, shape, dtype=jnp.float32)

    return {
        "x": normal(ks[0], (BATCH, SEQ, D_MODEL), 1.0),
        "g_mix": gain(ks[1], (DEPTH, D_MODEL)),
        "w_in": normal(ks[2], (DEPTH, D_MODEL, IN_COLS), D_MODEL ** -0.5),
        "q_norm": gain(ks[3], (DEPTH, Q_LORA)),
        "w_uq": normal(ks[4], (DEPTH, Q_LORA, HEADS_A * (QK_NOPE + QK_ROPE)), Q_LORA ** -0.5),
        "kv_norm": gain(ks[5], (DEPTH, KV_LORA)),
        "w_ukv": normal(ks[6], (DEPTH, KV_LORA, HEADS_A * (QK_NOPE + V_DIM_A)), KV_LORA ** -0.5),
        "rpb": normal(ks[7], (DEPTH, HEADS_C, 2 * NA_ROWS - 1, 2 * NA_COLS - 1), 0.1),
        "out_norm_a": gain(ks[8], (DEPTH, WIDTH_A)),
        "out_norm_b": gain(ks[9], (DEPTH, WIDTH_B)),
        "out_norm_c": gain(ks[10], (DEPTH, WIDTH_C)),
        "w_out": normal(ks[11], (DEPTH, MIX_WIDTH, D_MODEL), MIX_WIDTH ** -0.5),
        "g_mlp": gain(ks[12], (DEPTH, D_MODEL)),
        "w_mlp_in": normal(ks[13], (DEPTH, D_MODEL, D_FF), D_MODEL ** -0.5),
        "w_mlp_out": normal(ks[14], (DEPTH, D_FF, D_MODEL), D_FF ** -0.5),
        "g_final": gain(ks[15], (D_MODEL,)),
    }


def reference(x, g_mix, w_in, q_norm, w_uq, kv_norm, w_ukv, rpb, out_norm_a, out_norm_b, out_norm_c,
              w_out, g_mlp, w_mlp_in, w_mlp_out, g_final):
    b, s, _ = x.shape
    pos = jnp.arange(s, dtype=jnp.float32)
    for l in range(DEPTH):
        h = rms_norm(x, g_mix[l])
        proj = h @ w_in[l]
        p_a = proj[..., :COLS_A]
        p_b = proj[..., COLS_A:COLS_A + COLS_B]
        p_c = proj[..., COLS_A + COLS_B:]
        c_q = p_a[..., :Q_LORA]
        c_kv = p_a[..., Q_LORA:Q_LORA + KV_LORA]
        k_pe = p_a[..., Q_LORA + KV_LORA:]
        qa = (rms_norm(c_q, q_norm[l]) @ w_uq[l]).reshape(b, s, HEADS_A, QK_NOPE + QK_ROPE)
        kva = (rms_norm(c_kv, kv_norm[l]) @ w_ukv[l]).reshape(b, s, HEADS_A, QK_NOPE + V_DIM_A)
        k_pe = jnp.broadcast_to(rope(k_pe[:, :, None, :], pos), (b, s, HEADS_A, QK_ROPE))
        qa = jnp.concatenate([qa[..., :QK_NOPE], rope(qa[..., QK_NOPE:], pos)], axis=-1)
        ka = jnp.concatenate([kva[..., :QK_NOPE], k_pe], axis=-1)
        o_a = dense_attention(qa, ka, kva[..., QK_NOPE:])
        pb = p_b.reshape(b, s, 3, HEADS_B, HEAD_DIM)
        o_b = dilated_sliding_attention(rope(pb[:, :, 0], pos), rope(pb[:, :, 1], pos), pb[:, :, 2])
        pc = p_c.reshape(b, s, 3, HEADS_C, HEAD_DIM)
        o_c = neighbourhood_attention(pc[:, :, 0], pc[:, :, 1], pc[:, :, 2], rpb[l])
        mixed = jnp.concatenate([
            rms_norm(o_a.reshape(b, s, WIDTH_A), out_norm_a[l]),
            rms_norm(o_b.reshape(b, s, WIDTH_B), out_norm_b[l]),
            rms_norm(o_c.reshape(b, s, WIDTH_C), out_norm_c[l]),
        ], axis=-1)
        x = x + mixed @ w_out[l]
        h2 = rms_norm(x, g_mlp[l])
        x = x + jnp.square(jax.nn.relu(h2 @ w_mlp_in[l])) @ w_mlp_out[l]
    return rms_norm(x, g_final)
```

```python
import functools

import jax
import jax.numpy as jnp
from jax import lax
from jax.experimental import pallas as pl
from jax.experimental.pallas import tpu as pltpu

HEAD_DIM = 64
LANES = 128
HEADS_A = 6
HEADS_B = 6
HEADS_C = 4
Q_LORA = 256
KV_LORA = 128
QK_NOPE = 64
QK_ROPE = 32
DILATED_PAIRS = ((128, 1), (512, 4), (2048, 16))
GRID_W = 64
NA_ROWS = 8
NA_COLS = 16
ROPE_THETA = 10000.0
NORM_EPS = 1e-6
NEG_INF = -1e30

WIDTH_A = HEADS_A * HEAD_DIM
WIDTH_B = HEADS_B * HEAD_DIM
WIDTH_C = HEADS_C * HEAD_DIM
QK_A_PAD = HEADS_A * LANES
RPB_PER_HEAD = (2 * NA_ROWS - 1) * (2 * NA_COLS - 1)

VMEM_LIMIT = 56 * 1024 * 1024

BF16 = jnp.bfloat16
F32 = jnp.float32


def _rms(x, g):
    return x * lax.rsqrt(jnp.mean(x * x, axis=-1, keepdims=True) + NORM_EPS) * g


def _dot(a, b):
    return jnp.dot(a, b, preferred_element_type=F32)


def _dot_nt(a, b):
    return lax.dot_general(a, b, (((1,), (1,)), ((), ())), preferred_element_type=F32)


def _lane_is_first_head(shape):
    return lax.broadcasted_iota(jnp.int32, shape, len(shape) - 1) < HEAD_DIM


def _keep_head(q, first, j):
    zero = jnp.zeros_like(q)
    return jnp.where(first, q, zero) if j == 0 else jnp.where(first, zero, q)


_C_CQ = 0
_C_CKV = _C_CQ + Q_LORA
_C_KPE = _C_CKV + KV_LORA
_C_KPR = _C_KPE + LANES
_C_QB = _C_KPR + LANES
_C_KB = _C_QB + WIDTH_B
_C_VB = _C_KB + WIDTH_B
_C_QC = _C_VB + WIDTH_B
_C_KC = _C_QC + WIDTH_C
_C_VC = _C_KC + WIDTH_C
_C_END = _C_VC + WIDTH_C

_T_COSQ, _T_SINQ, _T_COSK, _T_SINK, _T_COSB, _T_SINB = range(6)


def _proj_kernel(x_ref, g_ref, wbig_ref, qn_ref, wuq_ref, kvn_ref, wukv_ref, tab_ref,
                 qa_ref, ka_ref, va_ref, qb_ref, kb_ref, vb_ref, qc_ref, kc_ref, vc_ref):
    def tab(i):
        return tab_ref[:, i * LANES:(i + 1) * LANES]

    h = _rms(x_ref[...], g_ref[...]).astype(BF16)
    proj = _dot(h, wbig_ref[...])

    cqn = _rms(proj[:, _C_CQ:_C_CKV], qn_ref[...]).astype(BF16)
    qa2 = _dot(cqn, wuq_ref[...])
    ckvn = _rms(proj[:, _C_CKV:_C_KPE], kvn_ref[...]).astype(BF16)
    kv2 = _dot(ckvn, wukv_ref[...])
    kpe = proj[:, _C_KPE:_C_KPR] * tab(_T_COSK) + proj[:, _C_KPR:_C_QB] * tab(_T_SINK)
    cosq, sinq = tab(_T_COSQ), tab(_T_SINQ)
    for hd in range(HEADS_A):
        sl = slice(hd * LANES, (hd + 1) * LANES)
        rot = slice(QK_A_PAD + hd * LANES, QK_A_PAD + (hd + 1) * LANES)
        qa_ref[:, sl] = (qa2[:, sl] * cosq + qa2[:, rot] * sinq).astype(BF16)
        ka_ref[:, sl] = (kv2[:, sl] + kpe).astype(BF16)
    va_ref[...] = kv2[:, QK_A_PAD:].astype(BF16)

    cosb, sinb = tab(_T_COSB), tab(_T_SINB)
    first_half = (lax.broadcasted_iota(jnp.int32, cosb.shape, 1) % HEAD_DIM) < HEAD_DIM // 2
    for src, dst in ((_C_QB, qb_ref), (_C_KB, kb_ref)):
        for blk in range(WIDTH_B // LANES):
            xb = proj[:, src + blk * LANES:src + (blk + 1) * LANES]
            swapped = jnp.where(first_half, pltpu.roll(xb, LANES - HEAD_DIM // 2, 1),
                                pltpu.roll(xb, HEAD_DIM // 2, 1))
            dst[:, blk * LANES:(blk + 1) * LANES] = (xb * cosb + swapped * sinb).astype(BF16)
    vb_ref[...] = proj[:, _C_VB:_C_QC].astype(BF16)

    qc_ref[...] = proj[:, _C_QC:_C_KC].astype(BF16)
    kc_ref[...] = proj[:, _C_KC:_C_VC].astype(BF16)
    vc_ref[...] = proj[:, _C_VC:_C_END].astype(BF16)


def _proj_call(x2, g, wbig, qn, wuq, kvn, wukv, tabs, seq, tm):
    t, d = x2.shape
    nseq = seq // tm
    row = lambda i: (i, 0)
    const = lambda i: (0, 0)
    widths = (QK_A_PAD, QK_A_PAD, WIDTH_A, WIDTH_B, WIDTH_B, WIDTH_B, WIDTH_C, WIDTH_C, WIDTH_C)
    return pl.pallas_call(
        _proj_kernel,
        grid=(t // tm,),
        in_specs=[
            pl.BlockSpec((tm, d), row),
            pl.BlockSpec((1, d), const),
            pl.BlockSpec(wbig.shape, const),
            pl.BlockSpec((1, Q_LORA), const),
            pl.BlockSpec(wuq.shape, const),
            pl.BlockSpec((1, KV_LORA), const),
            pl.BlockSpec(wukv.shape, const),
            pl.BlockSpec((tm, tabs.shape[1]), lambda i: (i % nseq, 0)),
        ],
        out_specs=[pl.BlockSpec((tm, w), row) for w in widths],
        out_shape=[jax.ShapeDtypeStruct((t, w), BF16) for w in widths],
        compiler_params=pltpu.CompilerParams(dimension_semantics=("parallel",),
                                             vmem_limit_bytes=VMEM_LIMIT),
    )(x2, g, wbig, qn, wuq, kvn, wukv, tabs)


def _attn_a_kernel(q_ref, k_ref, v_ref, o_ref, *, tk):
    tq = q_ref.shape[1]
    nk = k_ref.shape[1] // tk
    outs = []
    for j in range(2):
        q = q_ref[0, :, j * LANES:(j + 1) * LANES]

        def body(kb, carry, q=q, j=j):
            m, l, acc = carry
            ks = pl.multiple_of(kb * tk, tk)
            k = k_ref[0, pl.ds(ks, tk), j * LANES:(j + 1) * LANES]
            v = v_ref[0, pl.ds(ks, tk), :]
            s = _dot_nt(q, k)
            m_new = jnp.maximum(m, jnp.max(s, axis=-1, keepdims=True))
            alpha = jnp.exp(m - m_new)
            p = jnp.exp(s - m_new)
            l = alpha * l + jnp.sum(p, axis=-1, keepdims=True)
            acc = alpha * acc + _dot(p.astype(BF16), v)
            return m_new, l, acc

        init = (jnp.full((tq, 1), NEG_INF, F32), jnp.zeros((tq, 1), F32), jnp.zeros((tq, LANES), F32))
        _, l, acc = lax.fori_loop(0, nk, body, init)
        outs.append(acc / l)
    o_ref[0] = jnp.where(_lane_is_first_head(outs[0].shape), outs[0], outs[1]).astype(o_ref.dtype)


def _attn_a_call(qa, ka, va, tq, tk):
    b, s, _ = qa.shape
    pairs = HEADS_A // 2
    return pl.pallas_call(
        functools.partial(_attn_a_kernel, tk=tk),
        grid=(b, pairs, s // tq),
        in_specs=[
            pl.BlockSpec((1, tq, 2 * LANES), lambda bi, p, qi: (bi, qi, p)),
            pl.BlockSpec((1, s, 2 * LANES), lambda bi, p, qi: (bi, 0, p)),
            pl.BlockSpec((1, s, LANES), lambda bi, p, qi: (bi, 0, p)),
        ],
        out_specs=pl.BlockSpec((1, tq, LANES), lambda bi, p, qi: (bi, qi, p)),
        out_shape=jax.ShapeDtypeStruct((b, s, WIDTH_A), BF16),
        compiler_params=pltpu.CompilerParams(dimension_semantics=("parallel", "parallel", "arbitrary"),
                                             vmem_limit_bytes=VMEM_LIMIT),
    )(qa, ka, va)


def _attn_b_kernel(q_ref, k_ref, v_ref, o_ref, lse_ref, *, tq, half):
    n = q_ref.shape[1]
    kw = min(tq + 2 * half, n)
    first = _lane_is_first_head((tq, LANES))

    def body(i, carry):
        q0 = pl.multiple_of(i * tq, tq)
        ks = pl.multiple_of(jnp.clip(q0 - half, 0, n - kw), half)
        q = q_ref[0, pl.ds(q0, tq), :]
        k = k_ref[0, pl.ds(ks, kw), :]
        v = v_ref[0, pl.ds(ks, kw), :]
        qi = q0 + lax.broadcasted_iota(jnp.int32, (tq, kw), 0)
        ki = ks + lax.broadcasted_iota(jnp.int32, (tq, kw), 1)
        mask = jnp.abs(qi - ki) <= half
        outs, lses = [], []
        for j in range(2):
            qj = _keep_head(q, first, j)
            s = jnp.where(mask, _dot_nt(qj, k), NEG_INF)
            m = jnp.max(s, axis=-1, keepdims=True)
            p = jnp.exp(s - m)
            den = jnp.sum(p, axis=-1, keepdims=True)
            outs.append(_dot(p.astype(BF16), v) / den)
            lses.append(jnp.broadcast_to(m + jnp.log(den), (tq, LANES)))
        o_ref[0, pl.ds(q0, tq), :] = jnp.where(first, outs[0], outs[1]).astype(o_ref.dtype)
        lse_ref[0, pl.ds(q0, tq), :] = jnp.where(first, lses[0], lses[1])
        return carry

    lax.fori_loop(0, n // tq, body, 0)


def _attn_b_call(q, k, v, dil, half, tq):
    b, s, w = q.shape
    n = s // dil
    pairs = w // LANES
    view = lambda a: a.reshape(b, n, dil * w)
    spec = pl.BlockSpec((1, n, LANES), lambda bi, r, p: (bi, 0, r * pairs + p))
    o, lse = pl.pallas_call(
        functools.partial(_attn_b_kernel, tq=min(tq, n), half=half),
        grid=(b, dil, pairs),
        in_specs=[spec, spec, spec],
        out_specs=[spec, spec],
        out_shape=[jax.ShapeDtypeStruct((b, n, dil * w), BF16), jax.ShapeDtypeStruct((b, n, dil * w), F32)],
        compiler_params=pltpu.CompilerParams(dimension_semantics=("parallel", "parallel", "parallel"),
                                             vmem_limit_bytes=VMEM_LIMIT),
    )(view(q), view(k), view(v))
    return o.reshape(b * s, w), lse.reshape(b * s, w)


def _na_bias_kernel(rpb_ref, o_ref):
    base = pl.program_id(0) * RPB_PER_HEAD
    shape = (GRID_W, LANES)
    lane = lax.broadcasted_iota(jnp.int32, shape, 1)
    p = lax.broadcasted_iota(jnp.int32, shape, 0)
    c = lane % GRID_W
    upper = lane >= GRID_W
    c_start = jnp.clip(p - NA_COLS // 2, 0, GRID_W - NA_COLS)
    col_ok = (c >= c_start) & (c < c_start + NA_COLS)
    dc = c - p + (NA_COLS - 1)
    n_dc = 2 * NA_COLS - 1
    for v in range(NA_ROWS):
        for m in range(NA_ROWS * GRID_W // LANES):
            a_lo = 2 * m - v + (NA_ROWS - 1)
            acc = jnp.full(shape, NEG_INF, F32)
            for b in range(n_dc):
                val = jnp.where(upper, rpb_ref[base + (a_lo + 1) * n_dc + b], rpb_ref[base + a_lo * n_dc + b])
                acc = jnp.where(dc == b, val, acc)
            o_ref[0, v, :, m * LANES:(m + 1) * LANES] = jnp.where(col_ok, acc, NEG_INF)


def _na_bias_call(rpb):
    nh = rpb.shape[0] * rpb.shape[1]
    return pl.pallas_call(
        _na_bias_kernel,
        grid=(nh,),
        in_specs=[pl.BlockSpec(memory_space=pltpu.SMEM)],
        out_specs=pl.BlockSpec((1, NA_ROWS, GRID_W, NA_ROWS * GRID_W), lambda g: (g, 0, 0, 0)),
        out_shape=jax.ShapeDtypeStruct((nh, NA_ROWS, GRID_W, NA_ROWS * GRID_W), F32),
    )(rpb.reshape(-1))


def _attn_c_kernel(q_ref, k_ref, v_ref, bias_ref, o_ref):
    rows = q_ref.shape[1] // GRID_W
    win = NA_ROWS * GRID_W
    first = _lane_is_first_head((GRID_W, LANES))

    def body(r, carry):
        r_start = jnp.clip(r - NA_ROWS // 2, 0, rows - NA_ROWS)
        variant = r - r_start
        q = q_ref[0, pl.ds(pl.multiple_of(r * GRID_W, GRID_W), GRID_W), :]
        ks = pl.multiple_of(r_start * GRID_W, GRID_W)
        k = k_ref[0, pl.ds(ks, win), :]
        v = v_ref[0, pl.ds(ks, win), :]
        outs = []
        for j in range(2):
            qj = _keep_head(q, first, j)
            s = _dot_nt(qj, k) + bias_ref[j, variant]
            m = jnp.max(s, axis=-1, keepdims=True)
            p = jnp.exp(s - m)
            den = jnp.sum(p, axis=-1, keepdims=True)
            outs.append(_dot(p.astype(BF16), v) / den)
        o_ref[0, pl.ds(pl.multiple_of(r * GRID_W, GRID_W), GRID_W), :] = (
            jnp.where(first, outs[0], outs[1]).astype(o_ref.dtype))
        return carry

    lax.fori_loop(0, rows, body, 0)


def _attn_c_call(q, k, v, bias):
    b, s, w = q.shape
    pairs = w // LANES
    spec = pl.BlockSpec((1, s, LANES), lambda bi, p: (bi, 0, p))
    return pl.pallas_call(
        _attn_c_kernel,
        grid=(b, pairs),
        in_specs=[spec, spec, spec,
                  pl.BlockSpec((2,) + bias.shape[1:], lambda bi, p: (p, 0, 0, 0))],
        out_specs=spec,
        out_shape=jax.ShapeDtypeStruct((b, s, w), BF16),
        compiler_params=pltpu.CompilerParams(dimension_semantics=("parallel", "parallel"),
                                             vmem_limit_bytes=VMEM_LIMIT),
    )(q, k, v, bias)


def _out_kernel(x_ref, oa_ref, ob1_ref, ob2_ref, ob3_ref, l1_ref, l2_ref, l3_ref, oc_ref,
                ga_ref, gb_ref, gc_ref, w_ref, o_ref):
    na = _rms(oa_ref[...].astype(F32), ga_ref[...]).astype(BF16)
    l1, l2, l3 = l1_ref[...], l2_ref[...], l3_ref[...]
    lmax = jnp.maximum(jnp.maximum(l1, l2), l3)
    e1, e2, e3 = jnp.exp(l1 - lmax), jnp.exp(l2 - lmax), jnp.exp(l3 - lmax)
    ob = (e1 * ob1_ref[...].astype(F32) + e2 * ob2_ref[...].astype(F32) + e3 * ob3_ref[...].astype(F32)) / (e1 + e2 + e3)
    nb = _rms(ob, gb_ref[...]).astype(BF16)
    nc = _rms(oc_ref[...].astype(F32), gc_ref[...]).astype(BF16)
    acc = _dot(na, w_ref[0:WIDTH_A, :])
    acc += _dot(nb, w_ref[WIDTH_A:WIDTH_A + WIDTH_B, :])
    acc += _dot(nc, w_ref[WIDTH_A + WIDTH_B:, :])
    o_ref[...] = x_ref[...] + acc


def _out_call(x2, oa, obs, lses, oc, ga, gb, gc, w, tm):
    t, d = x2.shape
    row = lambda i: (i, 0)
    const = lambda i: (0, 0)
    rs = lambda width: pl.BlockSpec((tm, width), row)
    cs = lambda width: pl.BlockSpec((1, width), const)
    return pl.pallas_call(
        _out_kernel,
        grid=(t // tm,),
        in_specs=[rs(d), rs(WIDTH_A), rs(WIDTH_B), rs(WIDTH_B), rs(WIDTH_B), rs(WIDTH_B), rs(WIDTH_B), rs(WIDTH_B),
                  rs(WIDTH_C), cs(WIDTH_A), cs(WIDTH_B), cs(WIDTH_C), pl.BlockSpec(w.shape, const)],
        out_specs=rs(d),
        out_shape=jax.ShapeDtypeStruct((t, d), F32),
        compiler_params=pltpu.CompilerParams(dimension_semantics=("parallel",),
                                             vmem_limit_bytes=VMEM_LIMIT),
    )(x2, oa, *obs, *lses, oc, ga, gb, gc, w)


def _mlp_kernel(x_ref, g_ref, w1_ref, w2_ref, gf_ref, o_ref, h_ref, acc_ref, *, final_norm):
    f = pl.program_id(1)

    @pl.when(f == 0)
    def _():
        h_ref[...] = _rms(x_ref[...], g_ref[...]).astype(BF16)
        acc_ref[...] = jnp.zeros_like(acc_ref)

    u = jnp.maximum(_dot(h_ref[...], w1_ref[...]), 0.0)
    acc_ref[...] += _dot((u * u).astype(BF16), w2_ref[...])

    @pl.when(f == pl.num_programs(1) - 1)
    def _():
        y = x_ref[...] + acc_ref[...]
        if final_norm:
            y = _rms(y, gf_ref[...])
        o_ref[...] = y


def _mlp_call(x2, g, w1, w2, gf, final_norm, tm, tf):
    t, d = x2.shape
    dff = w1.shape[1]
    return pl.pallas_call(
        functools.partial(_mlp_kernel, final_norm=final_norm),
        grid=(t // tm, dff // tf),
        in_specs=[
            pl.BlockSpec((tm, d), lambda i, f: (i, 0)),
            pl.BlockSpec((1, d), lambda i, f: (0, 0)),
            pl.BlockSpec((d, tf), lambda i, f: (0, f)),
            pl.BlockSpec((tf, d), lambda i, f: (f, 0)),
            pl.BlockSpec((1, d), lambda i, f: (0, 0)),
        ],
        out_specs=pl.BlockSpec((tm, d), lambda i, f: (i, 0)),
        out_shape=jax.ShapeDtypeStruct((t, d), F32),
        scratch_shapes=[pltpu.VMEM((tm, d), BF16), pltpu.VMEM((tm, d), F32)],
        compiler_params=pltpu.CompilerParams(dimension_semantics=("parallel", "arbitrary"),
                                             vmem_limit_bytes=VMEM_LIMIT),
    )(x2, g, w1, w2, gf)


def _rotate_half_cols(w, half):
    return jnp.concatenate([-w[..., half:], w[..., :half]], axis=-1)


def _rope_tables(seq):
    pos = jnp.arange(seq, dtype=F32)

    def cos_sin(half):
        inv_freq = ROPE_THETA ** (-jnp.arange(half, dtype=F32) / half)
        ang = pos[:, None] * inv_freq[None, :]
        return jnp.cos(ang), jnp.sin(ang)

    ca, sa = cos_sin(QK_ROPE // 2)
    ca2, sa2 = jnp.concatenate([ca, ca], -1), jnp.concatenate([sa, sa], -1)
    ones = jnp.ones((seq, QK_NOPE), F32)
    zeros = jnp.zeros((seq, QK_NOPE), F32)
    tail = jnp.zeros((seq, LANES - QK_NOPE - QK_ROPE), F32)
    scale_a = (QK_NOPE + QK_ROPE) ** -0.5
    cosq = jnp.concatenate([ones, ca2, tail], -1) * scale_a
    sinq = jnp.concatenate([zeros, sa2, tail], -1) * scale_a
    cosk = jnp.concatenate([zeros, ca2, tail], -1)
    sink = jnp.concatenate([zeros, sa2, tail], -1)
    cb, sb = cos_sin(HEAD_DIM // 2)
    cosb = jnp.concatenate([cb, cb, cb, cb], -1)
    sinb = jnp.concatenate([-sb, sb, -sb, sb], -1)
    return jnp.concatenate([cosq, sinq, cosk, sink, cosb, sinb], -1)


def _layer_weights(w_in, w_uq, w_ukv):
    d = w_in.shape[0]
    scale = HEAD_DIM ** -0.5
    c_b = Q_LORA + KV_LORA + QK_ROPE
    c_c = c_b + 3 * WIDTH_B
    w_kpe = w_in[:, Q_LORA + KV_LORA:c_b]

    def place(w):
        return jnp.concatenate([jnp.zeros((d, QK_NOPE), F32), w, jnp.zeros((d, LANES - QK_NOPE - QK_ROPE), F32)], -1)

    wbig = jnp.concatenate([
        w_in[:, :Q_LORA + KV_LORA],
        place(w_kpe), place(_rotate_half_cols(w_kpe, QK_ROPE // 2)),
        w_in[:, c_b:c_b + WIDTH_B] * scale, w_in[:, c_b + WIDTH_B:c_c],
        w_in[:, c_c:c_c + WIDTH_C] * scale, w_in[:, c_c + WIDTH_C:],
    ], -1).astype(BF16)

    uq = w_uq.reshape(Q_LORA, HEADS_A, QK_NOPE + QK_ROPE)
    pad = jnp.zeros((Q_LORA, HEADS_A, LANES - QK_NOPE - QK_ROPE), F32)
    uq_pad = jnp.concatenate([uq, pad], -1)
    uq_rot = jnp.concatenate([jnp.zeros((Q_LORA, HEADS_A, QK_NOPE), F32),
                              _rotate_half_cols(uq[..., QK_NOPE:], QK_ROPE // 2), pad], -1)
    wuq = jnp.concatenate([uq_pad.reshape(Q_LORA, QK_A_PAD), uq_rot.reshape(Q_LORA, QK_A_PAD)], -1).astype(BF16)

    ukv = w_ukv.reshape(KV_LORA, HEADS_A, QK_NOPE + HEAD_DIM)
    uk_pad = jnp.concatenate([ukv[..., :QK_NOPE], jnp.zeros((KV_LORA, HEADS_A, LANES - QK_NOPE), F32)], -1)
    wukv = jnp.concatenate([uk_pad.reshape(KV_LORA, QK_A_PAD), ukv[..., QK_NOPE:].reshape(KV_LORA, WIDTH_A)],
                           -1).astype(BF16)
    return wbig, wuq, wukv


def kernel(x, g_mix, w_in, q_norm, w_uq, kv_norm, w_ukv, rpb, out_norm_a, out_norm_b, out_norm_c, w_out, g_mlp,
           w_mlp_in, w_mlp_out, g_final):
    b, s, d = x.shape
    depth = w_in.shape[0]
    t = b * s
    assert s % GRID_W == 0 and s // GRID_W >= NA_ROWS
    tm_proj = min(512, s)
    tm_out = min(512, t)
    tm_mlp = min(1024, t)
    tq_a = min(512, s)
    tk_a = min(512, s)

    tabs = _rope_tables(s)
    na_bias = _na_bias_call(rpb).reshape(depth, HEADS_C, NA_ROWS, GRID_W, NA_ROWS * GRID_W)
    row = lambda a: a.reshape(1, -1)

    x2 = x.reshape(t, d)
    for l in range(depth):
        wbig, wuq, wukv = _layer_weights(w_in[l], w_uq[l], w_ukv[l])
        qa, ka, va, qb, kb, vb, qc, kc, vc = _proj_call(
            x2, row(g_mix[l]), wbig, row(q_norm[l]), wuq, row(kv_norm[l]), wukv, tabs, s, tm_proj)
        seq3 = lambda a: a.reshape(b, s, a.shape[-1])
        oa = _attn_a_call(seq3(qa), seq3(ka), seq3(va), tq_a, tk_a).reshape(t, WIDTH_A)
        obs, lses = [], []
        for window, dil in DILATED_PAIRS:
            o_i, lse_i = _attn_b_call(seq3(qb), seq3(kb), seq3(vb), dil, window // (2 * dil), 128)
            obs.append(o_i)
            lses.append(lse_i)
        oc = _attn_c_call(seq3(qc), seq3(kc), seq3(vc), na_bias[l]).reshape(t, WIDTH_C)
        x2 = _out_call(x2, oa, obs, lses, oc, row(out_norm_a[l]), row(out_norm_b[l]), row(out_norm_c[l]),
                       w_out[l].astype(BF16), tm_out)
        x2 = _mlp_call(x2, row(g_mlp[l]), w_mlp_in[l].astype(BF16), w_mlp_out[l].astype(BF16), row(g_final),
                       l == depth - 1, tm_mlp, min(1024, w_mlp_in.shape[2]))
    return x2.reshape(b, s, d)
```

```python
import functools

import jax
import jax.numpy as jnp
from jax import lax
from jax.experimental import pallas as pl
from jax.experimental.pallas import tpu as pltpu

HEAD_DIM = 64
LANES = 128
HEADS_A = 6
HEADS_B = 6
HEADS_C = 4
Q_LORA = 256
KV_LORA = 128
QK_NOPE = 64
QK_ROPE = 32
DILATED_PAIRS = ((128, 1), (512, 4), (2048, 16))
GRID_W = 64
NA_ROWS = 8
NA_COLS = 16
ROPE_THETA = 10000.0
NORM_EPS = 1e-6
NEG_INF = -1e30
LOG2_E = 1.4426950408889634

WIDTH_A = HEADS_A * HEAD_DIM
WIDTH_B = HEADS_B * HEAD_DIM
WIDTH_C = HEADS_C * HEAD_DIM
QK_A_PAD = HEADS_A * LANES
RPB_PER_HEAD = (2 * NA_ROWS - 1) * (2 * NA_COLS - 1)

VMEM_LIMIT = 56 * 1024 * 1024

BF16 = jnp.bfloat16
F32 = jnp.float32


def _rms(x, g):
    return x * lax.rsqrt(jnp.mean(x * x, axis=-1, keepdims=True) + NORM_EPS) * g


def _dot(a, b):
    return jnp.dot(a, b, preferred_element_type=F32)


def _dot_nt(a, b):
    return lax.dot_general(a, b, (((1,), (1,)), ((), ())), preferred_element_type=F32)


def _lane_is_first_head(shape):
    return lax.broadcasted_iota(jnp.int32, shape, len(shape) - 1) < HEAD_DIM


def _keep_head(q, first, j):
    zero = jnp.zeros_like(q)
    return jnp.where(first, q, zero) if j == 0 else jnp.where(first, zero, q)


_C_CQ = 0
_C_CKV = _C_CQ + Q_LORA
_C_KPE = _C_CKV + KV_LORA
_C_KPR = _C_KPE + LANES
_C_QB = _C_KPR + LANES
_C_KB = _C_QB + WIDTH_B
_C_VB = _C_KB + WIDTH_B
_C_QC = _C_VB + WIDTH_B
_C_KC = _C_QC + WIDTH_C
_C_VC = _C_KC + WIDTH_C
_C_END = _C_VC + WIDTH_C

_T_COSQ, _T_SINQ, _T_COSK, _T_SINK, _T_COSB, _T_SINB = range(6)


def _proj_kernel(x_ref, g_ref, wbig_ref, qn_ref, wuq_ref, kvn_ref, wukv_ref, tab_ref,
                 qa_ref, ka_ref, va_ref, qb_ref, kb_ref, vb_ref, qc_ref, kc_ref, vc_ref):
    def tab(i):
        return tab_ref[:, i * LANES:(i + 1) * LANES]

    h = _rms(x_ref[...], g_ref[...]).astype(BF16)
    proj = _dot(h, wbig_ref[...])

    cqn = _rms(proj[:, _C_CQ:_C_CKV], qn_ref[...]).astype(BF16)
    qa2 = _dot(cqn, wuq_ref[...])
    ckvn = _rms(proj[:, _C_CKV:_C_KPE], kvn_ref[...]).astype(BF16)
    kv2 = _dot(ckvn, wukv_ref[...])
    kpe = proj[:, _C_KPE:_C_KPR] * tab(_T_COSK) + proj[:, _C_KPR:_C_QB] * tab(_T_SINK)
    cosq, sinq = tab(_T_COSQ), tab(_T_SINQ)
    for hd in range(HEADS_A):
        sl = slice(hd * LANES, (hd + 1) * LANES)
        rot = slice(QK_A_PAD + hd * LANES, QK_A_PAD + (hd + 1) * LANES)
        qa_ref[:, sl] = (qa2[:, sl] * cosq + qa2[:, rot] * sinq).astype(BF16)
        ka_ref[:, sl] = (kv2[:, sl] + kpe).astype(BF16)
    va_ref[...] = kv2[:, QK_A_PAD:].astype(BF16)

    cosb, sinb = tab(_T_COSB), tab(_T_SINB)
    first_half = (lax.broadcasted_iota(jnp.int32, cosb.shape, 1) % HEAD_DIM) < HEAD_DIM // 2
    for src, dst in ((_C_QB, qb_ref), (_C_KB, kb_ref)):
        for blk in range(WIDTH_B // LANES):
            xb = proj[:, src + blk * LANES:src + (blk + 1) * LANES]
            swapped = jnp.where(first_half, pltpu.roll(xb, LANES - HEAD_DIM // 2, 1),
                                pltpu.roll(xb, HEAD_DIM // 2, 1))
            dst[:, blk * LANES:(blk + 1) * LANES] = (xb * cosb + swapped * sinb).astype(BF16)
    vb_ref[...] = proj[:, _C_VB:_C_QC].astype(BF16)

    qc_ref[...] = proj[:, _C_QC:_C_KC].astype(BF16)
    kc_ref[...] = proj[:, _C_KC:_C_VC].astype(BF16)
    vc_ref[...] = proj[:, _C_VC:_C_END].astype(BF16)


def _proj_call(x2, g, wbig, qn, wuq, kvn, wukv, tabs, seq, tm):
    t, d = x2.shape
    nseq = seq // tm
    row = lambda i: (i, 0)
    const = lambda i: (0, 0)
    widths = (QK_A_PAD, QK_A_PAD, WIDTH_A, WIDTH_B, WIDTH_B, WIDTH_B, WIDTH_C, WIDTH_C, WIDTH_C)
    return pl.pallas_call(
        _proj_kernel,
        grid=(t // tm,),
        in_specs=[
            pl.BlockSpec((tm, d), row),
            pl.BlockSpec((1, d), const),
            pl.BlockSpec(wbig.shape, const),
            pl.BlockSpec((1, Q_LORA), const),
            pl.BlockSpec(wuq.shape, const),
            pl.BlockSpec((1, KV_LORA), const),
            pl.BlockSpec(wukv.shape, const),
            pl.BlockSpec((tm, tabs.shape[1]), lambda i: (i % nseq, 0)),
        ],
        out_specs=[pl.BlockSpec((tm, w), row) for w in widths],
        out_shape=[jax.ShapeDtypeStruct((t, w), BF16) for w in widths],
        compiler_params=pltpu.CompilerParams(dimension_semantics=("parallel",),
                                             vmem_limit_bytes=VMEM_LIMIT),
    )(x2, g, wbig, qn, wuq, kvn, wukv, tabs)


def _attn_a_kernel(qt_ref, k_ref, vt_ref, o_ref, st_a, st_b, *, tk):
    tq = qt_ref.shape[2]
    nk = k_ref.shape[1] // tk

    def scores(c, st_ref):
        ks = pl.multiple_of(jnp.minimum(c, nk - 1) * tk, tk)
        for j in range(2):
            st_ref[j] = _dot(k_ref[0, pl.ds(ks, tk), j * LANES:(j + 1) * LANES],
                             qt_ref[0, j * LANES:(j + 1) * LANES, :])

    def accumulate(c, st_ref, carry):
        ks = pl.multiple_of(c * tk, tk)
        stats = []
        for j in range(2):
            m, l, _ = carry[j]
            st = st_ref[j]
            m_new = jnp.maximum(m, jnp.max(st, axis=0, keepdims=True))
            alpha = jnp.exp2(m - m_new)
            pt = jnp.exp2(st - m_new)
            l = alpha * l + jnp.sum(pt, axis=0, keepdims=True)
            stats.append((m_new, l, alpha, pt.astype(BF16)))
        new = []
        for j in range(2):
            m_new, l, alpha, pt = stats[j]
            vt = vt_ref[0, j * HEAD_DIM:(j + 1) * HEAD_DIM, pl.ds(ks, tk)]
            new.append((m_new, l, alpha * carry[j][2] + _dot(vt, pt)))
        return tuple(new)

    def body(i, carry):
        scores(2 * i + 1, st_b)
        carry = accumulate(2 * i, st_a, carry)
        scores(2 * i + 2, st_a)
        return accumulate(2 * i + 1, st_b, carry)

    init = tuple((jnp.full((1, tq), NEG_INF, F32), jnp.zeros((1, tq), F32), jnp.zeros((HEAD_DIM, tq), F32))
                 for _ in range(2))
    scores(0, st_a)
    res = lax.fori_loop(0, nk // 2, body, init)
    for j in range(2):
        _, l, acc = res[j]
        o_ref[0, j * HEAD_DIM:(j + 1) * HEAD_DIM, :] = (acc / l).astype(o_ref.dtype)


def _attn_a_call(qa, ka, va, tq, tk):
    b, s, _ = qa.shape
    pairs = HEADS_A // 2
    assert s % (2 * tk) == 0
    qt = jnp.swapaxes(qa, 1, 2)
    vt = jnp.swapaxes(va, 1, 2)
    ot = pl.pallas_call(
        functools.partial(_attn_a_kernel, tk=tk),
        grid=(b, pairs, s // tq),
        in_specs=[
            pl.BlockSpec((1, 2 * LANES, tq), lambda bi, p, qi: (bi, p, qi)),
            pl.BlockSpec((1, s, 2 * LANES), lambda bi, p, qi: (bi, 0, p)),
            pl.BlockSpec((1, LANES, s), lambda bi, p, qi: (bi, p, 0)),
        ],
        out_specs=pl.BlockSpec((1, LANES, tq), lambda bi, p, qi: (bi, p, qi)),
        out_shape=jax.ShapeDtypeStruct((b, WIDTH_A, s), BF16),
        scratch_shapes=[pltpu.VMEM((2, tk, tq), F32), pltpu.VMEM((2, tk, tq), F32)],
        compiler_params=pltpu.CompilerParams(dimension_semantics=("parallel", "parallel", "arbitrary"),
                                             vmem_limit_bytes=VMEM_LIMIT),
    )(qt, ka, vt)
    return jnp.swapaxes(ot, 1, 2)


def _attn_b_kernel(q_ref, k_ref, v_ref, o_ref, lse_ref, *, tq, half):
    n = q_ref.shape[1]
    kw = min(tq + 2 * half, n)
    first = _lane_is_first_head((tq, LANES))

    def body(i, carry):
        q0 = pl.multiple_of(i * tq, tq)
        ks = pl.multiple_of(jnp.clip(q0 - half, 0, n - kw), half)
        q = q_ref[0, pl.ds(q0, tq), :]
        k = k_ref[0, pl.ds(ks, kw), :]
        v = v_ref[0, pl.ds(ks, kw), :]
        qi = q0 + lax.broadcasted_iota(jnp.int32, (tq, kw), 0)
        ki = ks + lax.broadcasted_iota(jnp.int32, (tq, kw), 1)
        mask = jnp.abs(qi - ki) <= half
        outs, lses = [], []
        for j in range(2):
            qj = _keep_head(q, first, j)
            s = jnp.where(mask, _dot_nt(qj, k), NEG_INF)
            m = jnp.max(s, axis=-1, keepdims=True)
            p = jnp.exp(s - m)
            den = jnp.sum(p, axis=-1, keepdims=True)
            outs.append(_dot(p.astype(BF16), v) / den)
            lses.append(jnp.broadcast_to(m + jnp.log(den), (tq, LANES)))
        o_ref[0, pl.ds(q0, tq), :] = jnp.where(first, outs[0], outs[1]).astype(o_ref.dtype)
        lse_ref[0, pl.ds(q0, tq), :] = jnp.where(first, lses[0], lses[1])
        return carry

    lax.fori_loop(0, n // tq, body, 0)


def _attn_b_call(q, k, v, dil, half, tq):
    b, s, w = q.shape
    n = s // dil
    pairs = w // LANES
    view = lambda a: a.reshape(b, n, dil * w)
    spec = pl.BlockSpec((1, n, LANES), lambda bi, r, p: (bi, 0, r * pairs + p))
    o, lse = pl.pallas_call(
        functools.partial(_attn_b_kernel, tq=min(tq, n), half=half),
        grid=(b, dil, pairs),
        in_specs=[spec, spec, spec],
        out_specs=[spec, spec],
        out_shape=[jax.ShapeDtypeStruct((b, n, dil * w), BF16), jax.ShapeDtypeStruct((b, n, dil * w), F32)],
        compiler_params=pltpu.CompilerParams(dimension_semantics=("parallel", "parallel", "parallel"),
                                             vmem_limit_bytes=VMEM_LIMIT),
    )(view(q), view(k), view(v))
    return o.reshape(b * s, w), lse.reshape(b * s, w)


def _na_bias_kernel(rpb_ref, o_ref):
    base = pl.program_id(0) * RPB_PER_HEAD
    shape = (GRID_W, LANES)
    lane = lax.broadcasted_iota(jnp.int32, shape, 1)
    p = lax.broadcasted_iota(jnp.int32, shape, 0)
    c = lane % GRID_W
    upper = lane >= GRID_W
    c_start = jnp.clip(p - NA_COLS // 2, 0, GRID_W - NA_COLS)
    col_ok = (c >= c_start) & (c < c_start + NA_COLS)
    dc = c - p + (NA_COLS - 1)
    n_dc = 2 * NA_COLS - 1
    for v in range(NA_ROWS):
        for m in range(NA_ROWS * GRID_W // LANES):
            a_lo = 2 * m - v + (NA_ROWS - 1)
            acc = jnp.full(shape, NEG_INF, F32)
            for b in range(n_dc):
                val = jnp.where(upper, rpb_ref[base + (a_lo + 1) * n_dc + b], rpb_ref[base + a_lo * n_dc + b])
                acc = jnp.where(dc == b, val, acc)
            o_ref[0, v, :, m * LANES:(m + 1) * LANES] = jnp.where(col_ok, acc, NEG_INF)


def _na_bias_call(rpb):
    nh = rpb.shape[0] * rpb.shape[1]
    return pl.pallas_call(
        _na_bias_kernel,
        grid=(nh,),
        in_specs=[pl.BlockSpec(memory_space=pltpu.SMEM)],
        out_specs=pl.BlockSpec((1, NA_ROWS, GRID_W, NA_ROWS * GRID_W), lambda g: (g, 0, 0, 0)),
        out_shape=jax.ShapeDtypeStruct((nh, NA_ROWS, GRID_W, NA_ROWS * GRID_W), F32),
    )(rpb.reshape(-1))


def _attn_c_kernel(q_ref, k_ref, v_ref, bias_ref, o_ref):
    rows = q_ref.shape[1] // GRID_W
    win = NA_ROWS * GRID_W
    first = _lane_is_first_head((GRID_W, LANES))

    def body(r, carry):
        r_start = jnp.clip(r - NA_ROWS // 2, 0, rows - NA_ROWS)
        variant = r - r_start
        q = q_ref[0, pl.ds(pl.multiple_of(r * GRID_W, GRID_W), GRID_W), :]
        ks = pl.multiple_of(r_start * GRID_W, GRID_W)
        k = k_ref[0, pl.ds(ks, win), :]
        v = v_ref[0, pl.ds(ks, win), :]
        outs = []
        for j in range(2):
            qj = _keep_head(q, first, j)
            s = _dot_nt(qj, k) + bias_ref[j, variant]
            m = jnp.max(s, axis=-1, keepdims=True)
            p = jnp.exp(s - m)
            den = jnp.sum(p, axis=-1, keepdims=True)
            outs.append(_dot(p.astype(BF16), v) / den)
        o_ref[0, pl.ds(pl.multiple_of(r * GRID_W, GRID_W), GRID_W), :] = (
            jnp.where(first, outs[0], outs[1]).astype(o_ref.dtype))
        return carry

    lax.fori_loop(0, rows, body, 0)


def _attn_c_call(q, k, v, bias):
    b, s, w = q.shape
    pairs = w // LANES
    spec = pl.BlockSpec((1, s, LANES), lambda bi, p: (bi, 0, p))
    return pl.pallas_call(
        _attn_c_kernel,
        grid=(b, pairs),
        in_specs=[spec, spec, spec,
                  pl.BlockSpec((2,) + bias.shape[1:], lambda bi, p: (p, 0, 0, 0))],
        out_specs=spec,
        out_shape=jax.ShapeDtypeStruct((b, s, w), BF16),
        compiler_params=pltpu.CompilerParams(dimension_semantics=("parallel", "parallel"),
                                             vmem_limit_bytes=VMEM_LIMIT),
    )(q, k, v, bias)


def _out_kernel(x_ref, oa_ref, ob1_ref, ob2_ref, ob3_ref, l1_ref, l2_ref, l3_ref, oc_ref,
                ga_ref, gb_ref, gc_ref, w_ref, o_ref):
    na = _rms(oa_ref[...].astype(F32), ga_ref[...]).astype(BF16)
    l1, l2, l3 = l1_ref[...], l2_ref[...], l3_ref[...]
    lmax = jnp.maximum(jnp.maximum(l1, l2), l3)
    e1, e2, e3 = jnp.exp(l1 - lmax), jnp.exp(l2 - lmax), jnp.exp(l3 - lmax)
    ob = (e1 * ob1_ref[...].astype(F32) + e2 * ob2_ref[...].astype(F32) + e3 * ob3_ref[...].astype(F32)) / (e1 + e2 + e3)
    nb = _rms(ob, gb_ref[...]).astype(BF16)
    nc = _rms(oc_ref[...].astype(F32), gc_ref[...]).astype(BF16)
    acc = _dot(na, w_ref[0:WIDTH_A, :])
    acc += _dot(nb, w_ref[WIDTH_A:WIDTH_A + WIDTH_B, :])
    acc += _dot(nc, w_ref[WIDTH_A + WIDTH_B:, :])
    o_ref[...] = x_ref[...] + acc


def _out_call(x2, oa, obs, lses, oc, ga, gb, gc, w, tm):
    t, d = x2.shape
    row = lambda i: (i, 0)
    const = lambda i: (0, 0)
    rs = lambda width: pl.BlockSpec((tm, width), row)
    cs = lambda width: pl.BlockSpec((1, width), const)
    return pl.pallas_call(
        _out_kernel,
        grid=(t // tm,),
        in_specs=[rs(d), rs(WIDTH_A), rs(WIDTH_B), rs(WIDTH_B), rs(WIDTH_B), rs(WIDTH_B), rs(WIDTH_B), rs(WIDTH_B),
                  rs(WIDTH_C), cs(WIDTH_A), cs(WIDTH_B), cs(WIDTH_C), pl.BlockSpec(w.shape, const)],
        out_specs=rs(d),
        out_shape=jax.ShapeDtypeStruct((t, d), F32),
        compiler_params=pltpu.CompilerParams(dimension_semantics=("parallel",),
                                             vmem_limit_bytes=VMEM_LIMIT),
    )(x2, oa, *obs, *lses, oc, ga, gb, gc, w)


def _mlp_kernel(x_ref, g_ref, w1_ref, w2_ref, gf_ref, o_ref, h_ref, acc_ref, *, final_norm):
    f = pl.program_id(1)

    @pl.when(f == 0)
    def _():
        h_ref[...] = _rms(x_ref[...], g_ref[...]).astype(BF16)
        acc_ref[...] = jnp.zeros_like(acc_ref)

    u = jnp.maximum(_dot(h_ref[...], w1_ref[...]), 0.0)
    acc_ref[...] += _dot((u * u).astype(BF16), w2_ref[...])

    @pl.when(f == pl.num_programs(1) - 1)
    def _():
        y = x_ref[...] + acc_ref[...]
        if final_norm:
            y = _rms(y, gf_ref[...])
        o_ref[...] = y


def _mlp_call(x2, g, w1, w2, gf, final_norm, tm, tf):
    t, d = x2.shape
    dff = w1.shape[1]
    return pl.pallas_call(
        functools.partial(_mlp_kernel, final_norm=final_norm),
        grid=(t // tm, dff // tf),
        in_specs=[
            pl.BlockSpec((tm, d), lambda i, f: (i, 0)),
            pl.BlockSpec((1, d), lambda i, f: (0, 0)),
            pl.BlockSpec((d, tf), lambda i, f: (0, f)),
            pl.BlockSpec((tf, d), lambda i, f: (f, 0)),
            pl.BlockSpec((1, d), lambda i, f: (0, 0)),
        ],
        out_specs=pl.BlockSpec((tm, d), lambda i, f: (i, 0)),
        out_shape=jax.ShapeDtypeStruct((t, d), F32),
        scratch_shapes=[pltpu.VMEM((tm, d), BF16), pltpu.VMEM((tm, d), F32)],
        compiler_params=pltpu.CompilerParams(dimension_semantics=("parallel", "arbitrary"),
                                             vmem_limit_bytes=VMEM_LIMIT),
    )(x2, g, w1, w2, gf)


def _rotate_half_cols(w, half):
    return jnp.concatenate([-w[..., half:], w[..., :half]], axis=-1)


def _rope_tables(seq):
    pos = jnp.arange(seq, dtype=F32)

    def cos_sin(half):
        inv_freq = ROPE_THETA ** (-jnp.arange(half, dtype=F32) / half)
        ang = pos[:, None] * inv_freq[None, :]
        return jnp.cos(ang), jnp.sin(ang)

    ca, sa = cos_sin(QK_ROPE // 2)
    ca2, sa2 = jnp.concatenate([ca, ca], -1), jnp.concatenate([sa, sa], -1)
    ones = jnp.ones((seq, QK_NOPE), F32)
    zeros = jnp.zeros((seq, QK_NOPE), F32)
    tail = jnp.zeros((seq, LANES - QK_NOPE - QK_ROPE), F32)
    scale_a = (QK_NOPE + QK_ROPE) ** -0.5 * LOG2_E
    cosq = jnp.concatenate([ones, ca2, tail], -1) * scale_a
    sinq = jnp.concatenate([zeros, sa2, tail], -1) * scale_a
    cosk = jnp.concatenate([zeros, ca2, tail], -1)
    sink = jnp.concatenate([zeros, sa2, tail], -1)
    cb, sb = cos_sin(HEAD_DIM // 2)
    cosb = jnp.concatenate([cb, cb, cb, cb], -1)
    sinb = jnp.concatenate([-sb, sb, -sb, sb], -1)
    return jnp.concatenate([cosq, sinq, cosk, sink, cosb, sinb], -1)


def _layer_weights(w_in, w_uq, w_ukv):
    d = w_in.shape[0]
    scale = HEAD_DIM ** -0.5
    c_b = Q_LORA + KV_LORA + QK_ROPE
    c_c = c_b + 3 * WIDTH_B
    w_kpe = w_in[:, Q_LORA + KV_LORA:c_b]

    def place(w):
        return jnp.concatenate([jnp.zeros((d, QK_NOPE), F32), w, jnp.zeros((d, LANES - QK_NOPE - QK_ROPE), F32)], -1)

    wbig = jnp.concatenate([
        w_in[:, :Q_LORA + KV_LORA],
        place(w_kpe), place(_rotate_half_cols(w_kpe, QK_ROPE // 2)),
        w_in[:, c_b:c_b + WIDTH_B] * scale, w_in[:, c_b + WIDTH_B:c_c],
        w_in[:, c_c:c_c + WIDTH_C] * scale, w_in[:, c_c + WIDTH_C:],
    ], -1).astype(BF16)

    uq = w_uq.reshape(Q_LORA, HEADS_A, QK_NOPE + QK_ROPE)
    pad = jnp.zeros((Q_LORA, HEADS_A, LANES - QK_NOPE - QK_ROPE), F32)
    uq_pad = jnp.concatenate([uq, pad], -1)
    uq_rot = jnp.concatenate([jnp.zeros((Q_LORA, HEADS_A, QK_NOPE), F32),
                              _rotate_half_cols(uq[..., QK_NOPE:], QK_ROPE // 2), pad], -1)
    wuq = jnp.concatenate([uq_pad.reshape(Q_LORA, QK_A_PAD), uq_rot.reshape(Q_LORA, QK_A_PAD)], -1).astype(BF16)

    ukv = w_ukv.reshape(KV_LORA, HEADS_A, QK_NOPE + HEAD_DIM)
    uk_pad = jnp.concatenate([ukv[..., :QK_NOPE], jnp.zeros((KV_LORA, HEADS_A, LANES - QK_NOPE), F32)], -1)
    wukv = jnp.concatenate([uk_pad.reshape(KV_LORA, QK_A_PAD), ukv[..., QK_NOPE:].reshape(KV_LORA, WIDTH_A)],
                           -1).astype(BF16)
    return wbig, wuq, wukv


def kernel(x, g_mix, w_in, q_norm, w_uq, kv_norm, w_ukv, rpb, out_norm_a, out_norm_b, out_norm_c, w_out, g_mlp,
           w_mlp_in, w_mlp_out, g_final):
    b, s, d = x.shape
    depth = w_in.shape[0]
    t = b * s
    assert s % GRID_W == 0 and s // GRID_W >= NA_ROWS
    tm_proj = min(512, s)
    tm_out = min(512, t)
    tm_mlp = min(1024, t)
    tq_a = min(256, s)
    tk_a = min(512, s)

    tabs = _rope_tables(s)
    na_bias = _na_bias_call(rpb).reshape(depth, HEADS_C, NA_ROWS, GRID_W, NA_ROWS * GRID_W)
    row = lambda a: a.reshape(1, -1)

    x2 = x.reshape(t, d)
    for l in range(depth):
        wbig, wuq, wukv = _layer_weights(w_in[l], w_uq[l], w_ukv[l])
        qa, ka, va, qb, kb, vb, qc, kc, vc = _proj_call(
            x2, row(g_mix[l]), wbig, row(q_norm[l]), wuq, row(kv_norm[l]), wukv, tabs, s, tm_proj)
        seq3 = lambda a: a.reshape(b, s, a.shape[-1])
        oa = _attn_a_call(seq3(qa), seq3(ka), seq3(va), tq_a, tk_a).reshape(t, WIDTH_A)
        obs, lses = [], []
        for window, dil in DILATED_PAIRS:
            o_i, lse_i = _attn_b_call(seq3(qb), seq3(kb), seq3(vb), dil, window // (2 * dil), 128)
            obs.append(o_i)
            lses.append(lse_i)
        oc = _attn_c_call(seq3(qc), seq3(kc), seq3(vc), na_bias[l]).reshape(t, WIDTH_C)
        x2 = _out_call(x2, oa, obs, lses, oc, row(out_norm_a[l]), row(out_norm_b[l]), row(out_norm_c[l]),
                       w_out[l].astype(BF16), tm_out)
        x2 = _mlp_call(x2, row(g_mlp[l]), w_mlp_in[l].astype(BF16), w_mlp_out[l].astype(BF16), row(g_final),
                       l == depth - 1, tm_mlp, min(1024, w_mlp_in.shape[2]))
    return x2.reshape(b, s, d)
```

```python
import functools

import jax
import jax.numpy as jnp
from jax import lax
from jax.experimental import pallas as pl
from jax.experimental.pallas import tpu as pltpu

HEAD_DIM = 64
LANES = 128
HEADS_A = 6
HEADS_B = 6
HEADS_C = 4
Q_LORA = 256
KV_LORA = 128
QK_NOPE = 64
QK_ROPE = 32
DILATED_PAIRS = ((128, 1), (512, 4), (2048, 16))
GRID_W = 64
NA_ROWS = 8
NA_COLS = 16
ROPE_THETA = 10000.0
NORM_EPS = 1e-6
NEG_INF = -1e30
LOG2_E = 1.4426950408889634
BLOCKS_PER_STEP = 4

WIDTH_A = HEADS_A * HEAD_DIM
WIDTH_B = HEADS_B * HEAD_DIM
WIDTH_C = HEADS_C * HEAD_DIM
QK_A_PAD = HEADS_A * LANES
RPB_PER_HEAD = (2 * NA_ROWS - 1) * (2 * NA_COLS - 1)

VMEM_LIMIT = 56 * 1024 * 1024

BF16 = jnp.bfloat16
F32 = jnp.float32


def _rms(x, g):
    return x * lax.rsqrt(jnp.mean(x * x, axis=-1, keepdims=True) + NORM_EPS) * g


def _dot(a, b):
    return jnp.dot(a, b, preferred_element_type=F32)


def _dot_nt(a, b):
    return lax.dot_general(a, b, (((1,), (1,)), ((), ())), preferred_element_type=F32)


def _lane_is_first_head(shape):
    return lax.broadcasted_iota(jnp.int32, shape, len(shape) - 1) < HEAD_DIM


def _keep_head(q, first, j):
    zero = jnp.zeros_like(q)
    return jnp.where(first, q, zero) if j == 0 else jnp.where(first, zero, q)


_C_CQ = 0
_C_CKV = _C_CQ + Q_LORA
_C_KPE = _C_CKV + KV_LORA
_C_KPR = _C_KPE + LANES
_C_QB = _C_KPR + LANES
_C_KB = _C_QB + WIDTH_B
_C_VB = _C_KB + WIDTH_B
_C_QC = _C_VB + WIDTH_B
_C_KC = _C_QC + WIDTH_C
_C_VC = _C_KC + WIDTH_C
_C_END = _C_VC + WIDTH_C

_T_COSQ, _T_SINQ, _T_COSK, _T_SINK, _T_COSB, _T_SINB = range(6)


def _proj_kernel(x_ref, g_ref, wbig_ref, qn_ref, wuq_ref, kvn_ref, wukv_ref, tab_ref,
                 qa_ref, ka_ref, va_ref, qb_ref, kb_ref, vb_ref, qc_ref, kc_ref, vc_ref):
    def tab(i):
        return tab_ref[:, i * LANES:(i + 1) * LANES]

    h = _rms(x_ref[...], g_ref[...]).astype(BF16)
    proj = _dot(h, wbig_ref[...])

    cqn = _rms(proj[:, _C_CQ:_C_CKV], qn_ref[...]).astype(BF16)
    qa2 = _dot(cqn, wuq_ref[...])
    ckvn = _rms(proj[:, _C_CKV:_C_KPE], kvn_ref[...]).astype(BF16)
    kv2 = _dot(ckvn, wukv_ref[...])
    kpe = proj[:, _C_KPE:_C_KPR] * tab(_T_COSK) + proj[:, _C_KPR:_C_QB] * tab(_T_SINK)
    cosq, sinq = tab(_T_COSQ), tab(_T_SINQ)
    for hd in range(HEADS_A):
        sl = slice(hd * LANES, (hd + 1) * LANES)
        rot = slice(QK_A_PAD + hd * LANES, QK_A_PAD + (hd + 1) * LANES)
        qa_ref[:, sl] = (qa2[:, sl] * cosq + qa2[:, rot] * sinq).astype(BF16)
        ka_ref[:, sl] = (kv2[:, sl] + kpe).astype(BF16)
    va_ref[...] = kv2[:, QK_A_PAD:].astype(BF16)

    cosb, sinb = tab(_T_COSB), tab(_T_SINB)
    first_half = (lax.broadcasted_iota(jnp.int32, cosb.shape, 1) % HEAD_DIM) < HEAD_DIM // 2
    for src, dst in ((_C_QB, qb_ref), (_C_KB, kb_ref)):
        for blk in range(WIDTH_B // LANES):
            xb = proj[:, src + blk * LANES:src + (blk + 1) * LANES]
            swapped = jnp.where(first_half, pltpu.roll(xb, LANES - HEAD_DIM // 2, 1),
                                pltpu.roll(xb, HEAD_DIM // 2, 1))
            dst[:, blk * LANES:(blk + 1) * LANES] = (xb * cosb + swapped * sinb).astype(BF16)
    vb_ref[...] = proj[:, _C_VB:_C_QC].astype(BF16)

    qc_ref[...] = proj[:, _C_QC:_C_KC].astype(BF16)
    kc_ref[...] = proj[:, _C_KC:_C_VC].astype(BF16)
    vc_ref[...] = proj[:, _C_VC:_C_END].astype(BF16)


def _proj_call(x2, g, wbig, qn, wuq, kvn, wukv, tabs, seq, tm):
    t, d = x2.shape
    nseq = seq // tm
    row = lambda i: (i, 0)
    const = lambda i: (0, 0)
    widths = (QK_A_PAD, QK_A_PAD, WIDTH_A, WIDTH_B, WIDTH_B, WIDTH_B, WIDTH_C, WIDTH_C, WIDTH_C)
    return pl.pallas_call(
        _proj_kernel,
        grid=(t // tm,),
        in_specs=[
            pl.BlockSpec((tm, d), row),
            pl.BlockSpec((1, d), const),
            pl.BlockSpec(wbig.shape, const),
            pl.BlockSpec((1, Q_LORA), const),
            pl.BlockSpec(wuq.shape, const),
            pl.BlockSpec((1, KV_LORA), const),
            pl.BlockSpec(wukv.shape, const),
            pl.BlockSpec((tm, tabs.shape[1]), lambda i: (i % nseq, 0)),
        ],
        out_specs=[pl.BlockSpec((tm, w), row) for w in widths],
        out_shape=[jax.ShapeDtypeStruct((t, w), BF16) for w in widths],
        compiler_params=pltpu.CompilerParams(dimension_semantics=("parallel",),
                                             vmem_limit_bytes=VMEM_LIMIT),
    )(x2, g, wbig, qn, wuq, kvn, wukv, tabs)


def _attn_a_kernel(qt_ref, k_ref, vt_ref, o_ref, st_a, st_b, *, tk):
    tq = qt_ref.shape[2]
    nk = k_ref.shape[1] // tk

    def scores(c, st_ref):
        ks = pl.multiple_of(jnp.minimum(c, nk - 1) * tk, tk)
        for j in range(2):
            st_ref[j] = _dot(k_ref[0, pl.ds(ks, tk), j * LANES:(j + 1) * LANES],
                             qt_ref[0, j * LANES:(j + 1) * LANES, :])

    def accumulate(c, st_ref, carry):
        ks = pl.multiple_of(c * tk, tk)
        stats = []
        for j in range(2):
            m, l, _ = carry[j]
            st = st_ref[j]
            m_new = jnp.maximum(m, jnp.max(st, axis=0, keepdims=True))
            alpha = jnp.exp2(m - m_new)
            pt = jnp.exp2(st - m_new)
            l = alpha * l + jnp.sum(pt, axis=0, keepdims=True)
            stats.append((m_new, l, alpha, pt.astype(BF16)))
        new = []
        for j in range(2):
            m_new, l, alpha, pt = stats[j]
            vt = vt_ref[0, j * HEAD_DIM:(j + 1) * HEAD_DIM, pl.ds(ks, tk)]
            new.append((m_new, l, alpha * carry[j][2] + _dot(vt, pt)))
        return tuple(new)

    def body(i, carry):
        scores(2 * i + 1, st_b)
        carry = accumulate(2 * i, st_a, carry)
        scores(2 * i + 2, st_a)
        return accumulate(2 * i + 1, st_b, carry)

    init = tuple((jnp.full((1, tq), NEG_INF, F32), jnp.zeros((1, tq), F32), jnp.zeros((HEAD_DIM, tq), F32))
                 for _ in range(2))
    scores(0, st_a)
    res = lax.fori_loop(0, nk // 2, body, init)
    for j in range(2):
        _, l, acc = res[j]
        o_ref[0, j * HEAD_DIM:(j + 1) * HEAD_DIM, :] = (acc / l).astype(o_ref.dtype)


def _attn_a_call(qa, ka, va, tq, tk):
    b, s, _ = qa.shape
    pairs = HEADS_A // 2
    assert s % (2 * tk) == 0
    qt = jnp.swapaxes(qa, 1, 2)
    vt = jnp.swapaxes(va, 1, 2)
    ot = pl.pallas_call(
        functools.partial(_attn_a_kernel, tk=tk),
        grid=(b, pairs, s // tq),
        in_specs=[
            pl.BlockSpec((1, 2 * LANES, tq), lambda bi, p, qi: (bi, p, qi)),
            pl.BlockSpec((1, s, 2 * LANES), lambda bi, p, qi: (bi, 0, p)),
            pl.BlockSpec((1, LANES, s), lambda bi, p, qi: (bi, p, 0)),
        ],
        out_specs=pl.BlockSpec((1, LANES, tq), lambda bi, p, qi: (bi, p, qi)),
        out_shape=jax.ShapeDtypeStruct((b, WIDTH_A, s), BF16),
        scratch_shapes=[pltpu.VMEM((2, tk, tq), F32), pltpu.VMEM((2, tk, tq), F32)],
        compiler_params=pltpu.CompilerParams(dimension_semantics=("parallel", "parallel", "arbitrary"),
                                             vmem_limit_bytes=VMEM_LIMIT),
    )(qt, ka, vt)
    return jnp.swapaxes(ot, 1, 2)


def _attn_b_kernel(q_ref, k_ref, v_ref, o_ref, lse_ref, *, tq, half):
    n = q_ref.shape[1]
    kw = min(tq + 2 * half, n)
    first = _lane_is_first_head((tq, LANES))

    group = min(BLOCKS_PER_STEP, n // tq)
    diff = (lax.broadcasted_iota(jnp.int32, (tq, kw), 0) - lax.broadcasted_iota(jnp.int32, (tq, kw), 1))

    def body(it, carry):
        blocks = []
        for u in range(group):
            q0 = pl.multiple_of((it * group + u) * tq, tq)
            ks = pl.multiple_of(jnp.clip(q0 - half, 0, n - kw), half)
            blocks.append((q0, ks, q_ref[0, pl.ds(q0, tq), :], k_ref[0, pl.ds(ks, kw), :]))
        scores = [[_dot_nt(_keep_head(q, first, j), k) for j in range(2)] for (_, _, q, k) in blocks]
        probs = []
        for (q0, ks, _, _), s2 in zip(blocks, scores):
            mask = jnp.abs(diff + (q0 - ks)) <= half
            row = []
            for j in range(2):
                s = jnp.where(mask, s2[j], NEG_INF)
                m = jnp.max(s, axis=-1, keepdims=True)
                p = jnp.exp(s - m)
                den = jnp.sum(p, axis=-1, keepdims=True)
                row.append((p.astype(BF16), den, m + jnp.log(den)))
            probs.append(row)
        for (q0, ks, _, _), row in zip(blocks, probs):
            v = v_ref[0, pl.ds(ks, kw), :]
            outs = [_dot(p, v) / den for (p, den, _) in row]
            lses = [jnp.broadcast_to(lse, (tq, LANES)) for (_, _, lse) in row]
            o_ref[0, pl.ds(q0, tq), :] = jnp.where(first, outs[0], outs[1]).astype(o_ref.dtype)
            lse_ref[0, pl.ds(q0, tq), :] = jnp.where(first, lses[0], lses[1])
        return carry

    lax.fori_loop(0, n // (tq * group), body, 0)


def _attn_b_call(q, k, v, dil, half, tq):
    b, s, w = q.shape
    n = s // dil
    pairs = w // LANES
    view = lambda a: a.reshape(b, n, dil * w)
    spec = pl.BlockSpec((1, n, LANES), lambda bi, r, p: (bi, 0, r * pairs + p))
    o, lse = pl.pallas_call(
        functools.partial(_attn_b_kernel, tq=min(tq, n), half=half),
        grid=(b, dil, pairs),
        in_specs=[spec, spec, spec],
        out_specs=[spec, spec],
        out_shape=[jax.ShapeDtypeStruct((b, n, dil * w), BF16), jax.ShapeDtypeStruct((b, n, dil * w), F32)],
        compiler_params=pltpu.CompilerParams(dimension_semantics=("parallel", "parallel", "parallel"),
                                             vmem_limit_bytes=VMEM_LIMIT),
    )(view(q), view(k), view(v))
    return o.reshape(b * s, w), lse.reshape(b * s, w)


def _na_bias_kernel(rpb_ref, o_ref):
    base = pl.program_id(0) * RPB_PER_HEAD
    shape = (GRID_W, LANES)
    lane = lax.broadcasted_iota(jnp.int32, shape, 1)
    p = lax.broadcasted_iota(jnp.int32, shape, 0)
    c = lane % GRID_W
    upper = lane >= GRID_W
    c_start = jnp.clip(p - NA_COLS // 2, 0, GRID_W - NA_COLS)
    col_ok = (c >= c_start) & (c < c_start + NA_COLS)
    dc = c - p + (NA_COLS - 1)
    n_dc = 2 * NA_COLS - 1
    for v in range(NA_ROWS):
        for m in range(NA_ROWS * GRID_W // LANES):
            a_lo = 2 * m - v + (NA_ROWS - 1)
            acc = jnp.full(shape, NEG_INF, F32)
            for b in range(n_dc):
                val = jnp.where(upper, rpb_ref[base + (a_lo + 1) * n_dc + b], rpb_ref[base + a_lo * n_dc + b])
                acc = jnp.where(dc == b, val, acc)
            o_ref[0, v, :, m * LANES:(m + 1) * LANES] = jnp.where(col_ok, acc, NEG_INF)


def _na_bias_call(rpb):
    nh = rpb.shape[0] * rpb.shape[1]
    return pl.pallas_call(
        _na_bias_kernel,
        grid=(nh,),
        in_specs=[pl.BlockSpec(memory_space=pltpu.SMEM)],
        out_specs=pl.BlockSpec((1, NA_ROWS, GRID_W, NA_ROWS * GRID_W), lambda g: (g, 0, 0, 0)),
        out_shape=jax.ShapeDtypeStruct((nh, NA_ROWS, GRID_W, NA_ROWS * GRID_W), F32),
    )(rpb.reshape(-1))


def _attn_c_kernel(q_ref, k_ref, v_ref, bias_ref, o_ref):
    rows = q_ref.shape[1] // GRID_W
    win = NA_ROWS * GRID_W
    first = _lane_is_first_head((GRID_W, LANES))

    group = BLOCKS_PER_STEP
    assert rows % group == 0

    def body(it, carry):
        blocks = []
        for u in range(group):
            r = it * group + u
            r_start = jnp.clip(r - NA_ROWS // 2, 0, rows - NA_ROWS)
            q0 = pl.multiple_of(r * GRID_W, GRID_W)
            ks = pl.multiple_of(r_start * GRID_W, GRID_W)
            blocks.append((q0, ks, r - r_start, q_ref[0, pl.ds(q0, GRID_W), :], k_ref[0, pl.ds(ks, win), :]))
        scores = [[_dot_nt(_keep_head(q, first, j), k) for j in range(2)] for (_, _, _, q, k) in blocks]
        probs = []
        for (_, _, variant, _, _), s2 in zip(blocks, scores):
            row = []
            for j in range(2):
                s = s2[j] + bias_ref[j, variant]
                m = jnp.max(s, axis=-1, keepdims=True)
                p = jnp.exp(s - m)
                row.append((p.astype(BF16), jnp.sum(p, axis=-1, keepdims=True)))
            probs.append(row)
        for (q0, ks, _, _, _), row in zip(blocks, probs):
            v = v_ref[0, pl.ds(ks, win), :]
            outs = [_dot(p, v) / den for (p, den) in row]
            o_ref[0, pl.ds(q0, GRID_W), :] = jnp.where(first, outs[0], outs[1]).astype(o_ref.dtype)
        return carry

    lax.fori_loop(0, rows // group, body, 0)


def _attn_c_call(q, k, v, bias):
    b, s, w = q.shape
    pairs = w // LANES
    spec = pl.BlockSpec((1, s, LANES), lambda bi, p: (bi, 0, p))
    return pl.pallas_call(
        _attn_c_kernel,
        grid=(b, pairs),
        in_specs=[spec, spec, spec,
                  pl.BlockSpec((2,) + bias.shape[1:], lambda bi, p: (p, 0, 0, 0))],
        out_specs=spec,
        out_shape=jax.ShapeDtypeStruct((b, s, w), BF16),
        compiler_params=pltpu.CompilerParams(dimension_semantics=("parallel", "parallel"),
                                             vmem_limit_bytes=VMEM_LIMIT),
    )(q, k, v, bias)


def _out_kernel(x_ref, oa_ref, ob1_ref, ob2_ref, ob3_ref, l1_ref, l2_ref, l3_ref, oc_ref,
                ga_ref, gb_ref, gc_ref, w_ref, o_ref):
    na = _rms(oa_ref[...].astype(F32), ga_ref[...]).astype(BF16)
    l1, l2, l3 = l1_ref[...], l2_ref[...], l3_ref[...]
    lmax = jnp.maximum(jnp.maximum(l1, l2), l3)
    e1, e2, e3 = jnp.exp(l1 - lmax), jnp.exp(l2 - lmax), jnp.exp(l3 - lmax)
    ob = (e1 * ob1_ref[...].astype(F32) + e2 * ob2_ref[...].astype(F32) + e3 * ob3_ref[...].astype(F32)) / (e1 + e2 + e3)
    nb = _rms(ob, gb_ref[...]).astype(BF16)
    nc = _rms(oc_ref[...].astype(F32), gc_ref[...]).astype(BF16)
    acc = _dot(na, w_ref[0:WIDTH_A, :])
    acc += _dot(nb, w_ref[WIDTH_A:WIDTH_A + WIDTH_B, :])
    acc += _dot(nc, w_ref[WIDTH_A + WIDTH_B:, :])
    o_ref[...] = x_ref[...] + acc


def _out_call(x2, oa, obs, lses, oc, ga, gb, gc, w, tm):
    t, d = x2.shape
    row = lambda i: (i, 0)
    const = lambda i: (0, 0)
    rs = lambda width: pl.BlockSpec((tm, width), row)
    cs = lambda width: pl.BlockSpec((1, width), const)
    return pl.pallas_call(
        _out_kernel,
        grid=(t // tm,),
        in_specs=[rs(d), rs(WIDTH_A), rs(WIDTH_B), rs(WIDTH_B), rs(WIDTH_B), rs(WIDTH_B), rs(WIDTH_B), rs(WIDTH_B),
                  rs(WIDTH_C), cs(WIDTH_A), cs(WIDTH_B), cs(WIDTH_C), pl.BlockSpec(w.shape, const)],
        out_specs=rs(d),
        out_shape=jax.ShapeDtypeStruct((t, d), F32),
        compiler_params=pltpu.CompilerParams(dimension_semantics=("parallel",),
                                             vmem_limit_bytes=VMEM_LIMIT),
    )(x2, oa, *obs, *lses, oc, ga, gb, gc, w)


def _mlp_kernel(x_ref, g_ref, w1_ref, w2_ref, gf_ref, o_ref, h_ref, acc_ref, *, final_norm):
    f = pl.program_id(1)

    @pl.when(f == 0)
    def _():
        h_ref[...] = _rms(x_ref[...], g_ref[...]).astype(BF16)
        acc_ref[...] = jnp.zeros_like(acc_ref)

    u = jnp.maximum(_dot(h_ref[...], w1_ref[...]), 0.0)
    acc_ref[...] += _dot((u * u).astype(BF16), w2_ref[...])

    @pl.when(f == pl.num_programs(1) - 1)
    def _():
        y = x_ref[...] + acc_ref[...]
        if final_norm:
            y = _rms(y, gf_ref[...])
        o_ref[...] = y


def _mlp_call(x2, g, w1, w2, gf, final_norm, tm, tf):
    t, d = x2.shape
    dff = w1.shape[1]
    return pl.pallas_call(
        functools.partial(_mlp_kernel, final_norm=final_norm),
        grid=(t // tm, dff // tf),
        in_specs=[
            pl.BlockSpec((tm, d), lambda i, f: (i, 0)),
            pl.BlockSpec((1, d), lambda i, f: (0, 0)),
            pl.BlockSpec((d, tf), lambda i, f: (0, f)),
            pl.BlockSpec((tf, d), lambda i, f: (f, 0)),
            pl.BlockSpec((1, d), lambda i, f: (0, 0)),
        ],
        out_specs=pl.BlockSpec((tm, d), lambda i, f: (i, 0)),
        out_shape=jax.ShapeDtypeStruct((t, d), F32),
        scratch_shapes=[pltpu.VMEM((tm, d), BF16), pltpu.VMEM((tm, d), F32)],
        compiler_params=pltpu.CompilerParams(dimension_semantics=("parallel", "arbitrary"),
                                             vmem_limit_bytes=VMEM_LIMIT),
    )(x2, g, w1, w2, gf)


def _rotate_half_cols(w, half):
    return jnp.concatenate([-w[..., half:], w[..., :half]], axis=-1)


def _rope_tables(seq):
    pos = jnp.arange(seq, dtype=F32)

    def cos_sin(half):
        inv_freq = ROPE_THETA ** (-jnp.arange(half, dtype=F32) / half)
        ang = pos[:, None] * inv_freq[None, :]
        return jnp.cos(ang), jnp.sin(ang)

    ca, sa = cos_sin(QK_ROPE // 2)
    ca2, sa2 = jnp.concatenate([ca, ca], -1), jnp.concatenate([sa, sa], -1)
    ones = jnp.ones((seq, QK_NOPE), F32)
    zeros = jnp.zeros((seq, QK_NOPE), F32)
    tail = jnp.zeros((seq, LANES - QK_NOPE - QK_ROPE), F32)
    scale_a = (QK_NOPE + QK_ROPE) ** -0.5 * LOG2_E
    cosq = jnp.concatenate([ones, ca2, tail], -1) * scale_a
    sinq = jnp.concatenate([zeros, sa2, tail], -1) * scale_a
    cosk = jnp.concatenate([zeros, ca2, tail], -1)
    sink = jnp.concatenate([zeros, sa2, tail], -1)
    cb, sb = cos_sin(HEAD_DIM // 2)
    cosb = jnp.concatenate([cb, cb, cb, cb], -1)
    sinb = jnp.concatenate([-sb, sb, -sb, sb], -1)
    return jnp.concatenate([cosq, sinq, cosk, sink, cosb, sinb], -1)


def _layer_weights(w_in, w_uq, w_ukv):
    d = w_in.shape[0]
    scale = HEAD_DIM ** -0.5
    c_b = Q_LORA + KV_LORA + QK_ROPE
    c_c = c_b + 3 * WIDTH_B
    w_kpe = w_in[:, Q_LORA + KV_LORA:c_b]

    def place(w):
        return jnp.concatenate([jnp.zeros((d, QK_NOPE), F32), w, jnp.zeros((d, LANES - QK_NOPE - QK_ROPE), F32)], -1)

    wbig = jnp.concatenate([
        w_in[:, :Q_LORA + KV_LORA],
        place(w_kpe), place(_rotate_half_cols(w_kpe, QK_ROPE // 2)),
        w_in[:, c_b:c_b + WIDTH_B] * scale, w_in[:, c_b + WIDTH_B:c_c],
        w_in[:, c_c:c_c + WIDTH_C] * scale, w_in[:, c_c + WIDTH_C:],
    ], -1).astype(BF16)

    uq = w_uq.reshape(Q_LORA, HEADS_A, QK_NOPE + QK_ROPE)
    pad = jnp.zeros((Q_LORA, HEADS_A, LANES - QK_NOPE - QK_ROPE), F32)
    uq_pad = jnp.concatenate([uq, pad], -1)
    uq_rot = jnp.concatenate([jnp.zeros((Q_LORA, HEADS_A, QK_NOPE), F32),
                              _rotate_half_cols(uq[..., QK_NOPE:], QK_ROPE // 2), pad], -1)
    wuq = jnp.concatenate([uq_pad.reshape(Q_LORA, QK_A_PAD), uq_rot.reshape(Q_LORA, QK_A_PAD)], -1).astype(BF16)

    ukv = w_ukv.reshape(KV_LORA, HEADS_A, QK_NOPE + HEAD_DIM)
    uk_pad = jnp.concatenate([ukv[..., :QK_NOPE], jnp.zeros((KV_LORA, HEADS_A, LANES - QK_NOPE), F32)], -1)
    wukv = jnp.concatenate([uk_pad.reshape(KV_LORA, QK_A_PAD), ukv[..., QK_NOPE:].reshape(KV_LORA, WIDTH_A)],
                           -1).astype(BF16)
    return wbig, wuq, wukv


def kernel(x, g_mix, w_in, q_norm, w_uq, kv_norm, w_ukv, rpb, out_norm_a, out_norm_b, out_norm_c, w_out, g_mlp,
           w_mlp_in, w_mlp_out, g_final):
    b, s, d = x.shape
    depth = w_in.shape[0]
    t = b * s
    assert s % GRID_W == 0 and s // GRID_W >= NA_ROWS
    tm_proj = min(512, s)
    tm_out = min(512, t)
    tm_mlp = min(1024, t)
    tq_a = min(256, s)
    tk_a = min(512, s)

    tabs = _rope_tables(s)
    na_bias = _na_bias_call(rpb).reshape(depth, HEADS_C, NA_ROWS, GRID_W, NA_ROWS * GRID_W)
    row = lambda a: a.reshape(1, -1)

    x2 = x.reshape(t, d)
    for l in range(depth):
        wbig, wuq, wukv = _layer_weights(w_in[l], w_uq[l], w_ukv[l])
        qa, ka, va, qb, kb, vb, qc, kc, vc = _proj_call(
            x2, row(g_mix[l]), wbig, row(q_norm[l]), wuq, row(kv_norm[l]), wukv, tabs, s, tm_proj)
        seq3 = lambda a: a.reshape(b, s, a.shape[-1])
        oa = _attn_a_call(seq3(qa), seq3(ka), seq3(va), tq_a, tk_a).reshape(t, WIDTH_A)
        obs, lses = [], []
        for window, dil in DILATED_PAIRS:
            o_i, lse_i = _attn_b_call(seq3(qb), seq3(kb), seq3(vb), dil, window // (2 * dil), 128)
            obs.append(o_i)
            lses.append(lse_i)
        oc = _attn_c_call(seq3(qc), seq3(kc), seq3(vc), na_bias[l]).reshape(t, WIDTH_C)
        x2 = _out_call(x2, oa, obs, lses, oc, row(out_norm_a[l]), row(out_norm_b[l]), row(out_norm_c[l]),
                       w_out[l].astype(BF16), tm_out)
        x2 = _mlp_call(x2, row(g_mlp[l]), w_mlp_in[l].astype(BF16), w_mlp_out[l].astype(BF16), row(g_final),
                       l == depth - 1, tm_mlp, min(1024, w_mlp_in.shape[2]))
    return x2.reshape(b, s, d)
```

```python
import functools

import jax
import jax.numpy as jnp
from jax import lax
from jax.experimental import pallas as pl
from jax.experimental.pallas import tpu as pltpu

HEAD_DIM = 64
LANES = 128
HEADS_A = 6
HEADS_B = 6
HEADS_C = 4
Q_LORA = 256
KV_LORA = 128
QK_NOPE = 64
QK_ROPE = 32
DILATED_PAIRS = ((128, 1), (512, 4), (2048, 16))
_DILATIONS = tuple(dil for _, dil in DILATED_PAIRS if dil > 1)
GRID_W = 64
NA_ROWS = 8
NA_COLS = 16
ROPE_THETA = 10000.0
NORM_EPS = 1e-6
NEG_INF = -1e30
LOG2_E = 1.4426950408889634
BLOCKS_PER_STEP = 4

WIDTH_A = HEADS_A * HEAD_DIM
WIDTH_B = HEADS_B * HEAD_DIM
WIDTH_C = HEADS_C * HEAD_DIM
QK_A_PAD = HEADS_A * LANES
RPB_PER_HEAD = (2 * NA_ROWS - 1) * (2 * NA_COLS - 1)

VMEM_LIMIT = 56 * 1024 * 1024

BF16 = jnp.bfloat16
F32 = jnp.float32


def _rms(x, g):
    return x * lax.rsqrt(jnp.mean(x * x, axis=-1, keepdims=True) + NORM_EPS) * g


def _dot(a, b):
    return jnp.dot(a, b, preferred_element_type=F32)


def _dot_nt(a, b):
    return lax.dot_general(a, b, (((1,), (1,)), ((), ())), preferred_element_type=F32)


def _lane_is_first_head(shape):
    return lax.broadcasted_iota(jnp.int32, shape, len(shape) - 1) < HEAD_DIM


def _keep_head(q, first, j):
    zero = jnp.zeros_like(q)
    return jnp.where(first, q, zero) if j == 0 else jnp.where(first, zero, q)


_C_CQ = 0
_C_CKV = _C_CQ + Q_LORA
_C_KPE = _C_CKV + KV_LORA
_C_KPR = _C_KPE + LANES
_C_QB = _C_KPR + LANES
_C_KB = _C_QB + WIDTH_B
_C_VB = _C_KB + WIDTH_B
_C_QC = _C_VB + WIDTH_B
_C_KC = _C_QC + WIDTH_C
_C_VC = _C_KC + WIDTH_C
_C_END = _C_VC + WIDTH_C

_T_COSQ, _T_SINQ, _T_COSK, _T_SINK, _T_COSB, _T_SINB = range(6)


def _proj_kernel(x_ref, g_ref, wbig_ref, qn_ref, wuq_ref, kvn_ref, wukv_ref, tab_ref,
                 qa_ref, ka_ref, va_ref, qb_ref, kb_ref, vb_ref, qc_ref, kc_ref, vc_ref, *rest):
    dil_refs, stage_ref = rest[:-1], rest[-1]
    tm = x_ref.shape[0]

    def tab(i):
        return tab_ref[:, i * LANES:(i + 1) * LANES]

    h = _rms(x_ref[...], g_ref[...]).astype(BF16)
    proj = _dot(h, wbig_ref[...])

    cqn = _rms(proj[:, _C_CQ:_C_CKV], qn_ref[...]).astype(BF16)
    qa2 = _dot(cqn, wuq_ref[...])
    ckvn = _rms(proj[:, _C_CKV:_C_KPE], kvn_ref[...]).astype(BF16)
    kv2 = _dot(ckvn, wukv_ref[...])
    kpe = proj[:, _C_KPE:_C_KPR] * tab(_T_COSK) + proj[:, _C_KPR:_C_QB] * tab(_T_SINK)
    cosq, sinq = tab(_T_COSQ), tab(_T_SINQ)
    for hd in range(HEADS_A):
        sl = slice(hd * LANES, (hd + 1) * LANES)
        rot = slice(QK_A_PAD + hd * LANES, QK_A_PAD + (hd + 1) * LANES)
        qa_ref[:, sl] = (qa2[:, sl] * cosq + qa2[:, rot] * sinq).astype(BF16)
        ka_ref[:, sl] = (kv2[:, sl] + kpe).astype(BF16)
    va_ref[...] = kv2[:, QK_A_PAD:].astype(BF16)

    cosb, sinb = tab(_T_COSB), tab(_T_SINB)
    first_half = (lax.broadcasted_iota(jnp.int32, cosb.shape, 1) % HEAD_DIM) < HEAD_DIM // 2
    nblk = WIDTH_B // LANES
    for ti, (src, dst) in enumerate(((_C_QB, qb_ref), (_C_KB, kb_ref), (_C_VB, vb_ref))):
        for blk in range(nblk):
            xb = proj[:, src + blk * LANES:src + (blk + 1) * LANES]
            if dst is not vb_ref:
                swapped = jnp.where(first_half, pltpu.roll(xb, LANES - HEAD_DIM // 2, 1),
                                    pltpu.roll(xb, HEAD_DIM // 2, 1))
                xb = xb * cosb + swapped * sinb
            dst[:, blk * LANES:(blk + 1) * LANES] = xb.astype(BF16)
            stage_ref[ti * nblk + blk] = xb
    for di, dil in enumerate(_DILATIONS):
        rows = tm // dil
        for ti in range(3):
            dref = dil_refs[di * 3 + ti]
            for r in range(dil):
                for blk in range(nblk):
                    dref[0, r, :, blk * LANES:(blk + 1) * LANES] = (
                        stage_ref[ti * nblk + blk, pl.ds(r, rows, stride=dil), :].astype(BF16))

    qc_ref[...] = proj[:, _C_QC:_C_KC].astype(BF16)
    kc_ref[...] = proj[:, _C_KC:_C_VC].astype(BF16)
    vc_ref[...] = proj[:, _C_VC:_C_END].astype(BF16)


def _proj_call(x2, g, wbig, qn, wuq, kvn, wukv, tabs, seq, tm):
    t, d = x2.shape
    nseq = seq // tm
    row = lambda i: (i, 0)
    const = lambda i: (0, 0)
    widths = (QK_A_PAD, QK_A_PAD, WIDTH_A, WIDTH_B, WIDTH_B, WIDTH_B, WIDTH_C, WIDTH_C, WIDTH_C)
    out_specs = [pl.BlockSpec((tm, w), row) for w in widths]
    out_shape = [jax.ShapeDtypeStruct((t, w), BF16) for w in widths]
    for dil in _DILATIONS:
        assert tm % (16 * dil) == 0
        for _ in range(3):
            out_specs.append(pl.BlockSpec((1, dil, tm // dil, WIDTH_B), lambda i: (i // nseq, 0, i % nseq, 0)))
            out_shape.append(jax.ShapeDtypeStruct((t // seq, dil, seq // dil, WIDTH_B), BF16))
    return pl.pallas_call(
        _proj_kernel,
        grid=(t // tm,),
        in_specs=[
            pl.BlockSpec((tm, d), row),
            pl.BlockSpec((1, d), const),
            pl.BlockSpec(wbig.shape, const),
            pl.BlockSpec((1, Q_LORA), const),
            pl.BlockSpec(wuq.shape, const),
            pl.BlockSpec((1, KV_LORA), const),
            pl.BlockSpec(wukv.shape, const),
            pl.BlockSpec((tm, tabs.shape[1]), lambda i: (i % nseq, 0)),
        ],
        out_specs=out_specs,
        out_shape=out_shape,
        scratch_shapes=[pltpu.VMEM((3 * WIDTH_B // LANES, tm, LANES), F32)],
        compiler_params=pltpu.CompilerParams(dimension_semantics=("parallel",),
                                             vmem_limit_bytes=VMEM_LIMIT),
    )(x2, g, wbig, qn, wuq, kvn, wukv, tabs)


def _attn_a_kernel(qt_ref, k_ref, vt_ref, o_ref, st_a, st_b, *, tk):
    tq = qt_ref.shape[2]
    nk = k_ref.shape[1] // tk

    def scores(c, st_ref):
        ks = pl.multiple_of(jnp.minimum(c, nk - 1) * tk, tk)
        for j in range(2):
            st_ref[j] = _dot(k_ref[0, pl.ds(ks, tk), j * LANES:(j + 1) * LANES],
                             qt_ref[0, j * LANES:(j + 1) * LANES, :])

    def accumulate(c, st_ref, carry):
        ks = pl.multiple_of(c * tk, tk)
        stats = []
        for j in range(2):
            m, l, _ = carry[j]
            st = st_ref[j]
            m_new = jnp.maximum(m, jnp.max(st, axis=0, keepdims=True))
            alpha = jnp.exp2(m - m_new)
            pt = jnp.exp2(st - m_new)
            l = alpha * l + jnp.sum(pt, axis=0, keepdims=True)
            stats.append((m_new, l, alpha, pt.astype(BF16)))
        new = []
        for j in range(2):
            m_new, l, alpha, pt = stats[j]
            vt = vt_ref[0, j * HEAD_DIM:(j + 1) * HEAD_DIM, pl.ds(ks, tk)]
            new.append((m_new, l, alpha * carry[j][2] + _dot(vt, pt)))
        return tuple(new)

    def body(i, carry):
        scores(2 * i + 1, st_b)
        carry = accumulate(2 * i, st_a, carry)
        scores(2 * i + 2, st_a)
        return accumulate(2 * i + 1, st_b, carry)

    init = tuple((jnp.full((1, tq), NEG_INF, F32), jnp.zeros((1, tq), F32), jnp.zeros((HEAD_DIM, tq), F32))
                 for _ in range(2))
    scores(0, st_a)
    res = lax.fori_loop(0, nk // 2, body, init)
    for j in range(2):
        _, l, acc = res[j]
        o_ref[0, j * HEAD_DIM:(j + 1) * HEAD_DIM, :] = (acc / l).astype(o_ref.dtype)


def _attn_a_call(qa, ka, va, tq, tk):
    b, s, _ = qa.shape
    pairs = HEADS_A // 2
    assert s % (2 * tk) == 0
    qt = jnp.swapaxes(qa, 1, 2)
    vt = jnp.swapaxes(va, 1, 2)
    ot = pl.pallas_call(
        functools.partial(_attn_a_kernel, tk=tk),
        grid=(b, pairs, s // tq),
        in_specs=[
            pl.BlockSpec((1, 2 * LANES, tq), lambda bi, p, qi: (bi, p, qi)),
            pl.BlockSpec((1, s, 2 * LANES), lambda bi, p, qi: (bi, 0, p)),
            pl.BlockSpec((1, LANES, s), lambda bi, p, qi: (bi, p, 0)),
        ],
        out_specs=pl.BlockSpec((1, LANES, tq), lambda bi, p, qi: (bi, p, qi)),
        out_shape=jax.ShapeDtypeStruct((b, WIDTH_A, s), BF16),
        scratch_shapes=[pltpu.VMEM((2, tk, tq), F32), pltpu.VMEM((2, tk, tq), F32)],
        compiler_params=pltpu.CompilerParams(dimension_semantics=("parallel", "parallel", "arbitrary"),
                                             vmem_limit_bytes=VMEM_LIMIT),
    )(qt, ka, vt)
    return jnp.swapaxes(ot, 1, 2)


def _attn_b_kernel(q_ref, k_ref, v_ref, o_ref, lse_ref, *, tq, half):
    n = q_ref.shape[0]
    kw = min(tq + 2 * half, n)
    first = _lane_is_first_head((tq, LANES))

    group = min(BLOCKS_PER_STEP, n // tq)
    diff = (lax.broadcasted_iota(jnp.int32, (tq, kw), 0) - lax.broadcasted_iota(jnp.int32, (tq, kw), 1))

    def body(it, carry):
        blocks = []
        for u in range(group):
            q0 = pl.multiple_of((it * group + u) * tq, tq)
            ks = pl.multiple_of(jnp.clip(q0 - half, 0, n - kw), half)
            blocks.append((q0, ks, q_ref[pl.ds(q0, tq), :], k_ref[pl.ds(ks, kw), :]))
        scores = [[_dot_nt(_keep_head(q, first, j), k) for j in range(2)] for (_, _, q, k) in blocks]
        probs = []
        for (q0, ks, _, _), s2 in zip(blocks, scores):
            mask = jnp.abs(diff + (q0 - ks)) <= half
            row = []
            for j in range(2):
                s = jnp.where(mask, s2[j], NEG_INF)
                m = jnp.max(s, axis=-1, keepdims=True)
                p = jnp.exp(s - m)
                den = jnp.sum(p, axis=-1, keepdims=True)
                row.append((p.astype(BF16), den, m + jnp.log(den)))
            probs.append(row)
        for (q0, ks, _, _), row in zip(blocks, probs):
            v = v_ref[pl.ds(ks, kw), :]
            outs = [_dot(p, v) / den for (p, den, _) in row]
            lses = [jnp.broadcast_to(lse, (tq, LANES)) for (_, _, lse) in row]
            o_ref[pl.ds(q0, tq), :] = jnp.where(first, outs[0], outs[1]).astype(o_ref.dtype)
            lse_ref[pl.ds(q0, tq), :] = jnp.where(first, lses[0], lses[1])
        return carry

    lax.fori_loop(0, n // (tq * group), body, 0)


def _attn_b_call(q, k, v, half, tq):
    b, dil, n, w = q.shape
    pairs = w // LANES
    spec = pl.BlockSpec((None, None, n, LANES), lambda bi, r, p: (bi, r, 0, p))
    return pl.pallas_call(
        functools.partial(_attn_b_kernel, tq=min(tq, n), half=half),
        grid=(b, dil, pairs),
        in_specs=[spec, spec, spec],
        out_specs=[spec, spec],
        out_shape=[jax.ShapeDtypeStruct(q.shape, BF16), jax.ShapeDtypeStruct(q.shape, F32)],
        compiler_params=pltpu.CompilerParams(dimension_semantics=("parallel", "parallel", "parallel"),
                                             vmem_limit_bytes=VMEM_LIMIT),
    )(q, k, v)


def _na_bias_kernel(rpb_ref, o_ref):
    base = pl.program_id(0) * RPB_PER_HEAD
    shape = (GRID_W, LANES)
    lane = lax.broadcasted_iota(jnp.int32, shape, 1)
    p = lax.broadcasted_iota(jnp.int32, shape, 0)
    c = lane % GRID_W
    upper = lane >= GRID_W
    c_start = jnp.clip(p - NA_COLS // 2, 0, GRID_W - NA_COLS)
    col_ok = (c >= c_start) & (c < c_start + NA_COLS)
    dc = c - p + (NA_COLS - 1)
    n_dc = 2 * NA_COLS - 1
    for v in range(NA_ROWS):
        for m in range(NA_ROWS * GRID_W // LANES):
            a_lo = 2 * m - v + (NA_ROWS - 1)
            acc = jnp.full(shape, NEG_INF, F32)
            for b in range(n_dc):
                val = jnp.where(upper, rpb_ref[base + (a_lo + 1) * n_dc + b], rpb_ref[base + a_lo * n_dc + b])
                acc = jnp.where(dc == b, val, acc)
            o_ref[0, v, :, m * LANES:(m + 1) * LANES] = jnp.where(col_ok, acc, NEG_INF)


def _na_bias_call(rpb):
    nh = rpb.shape[0] * rpb.shape[1]
    return pl.pallas_call(
        _na_bias_kernel,
        grid=(nh,),
        in_specs=[pl.BlockSpec(memory_space=pltpu.SMEM)],
        out_specs=pl.BlockSpec((1, NA_ROWS, GRID_W, NA_ROWS * GRID_W), lambda g: (g, 0, 0, 0)),
        out_shape=jax.ShapeDtypeStruct((nh, NA_ROWS, GRID_W, NA_ROWS * GRID_W), F32),
    )(rpb.reshape(-1))


def _attn_c_kernel(q_ref, k_ref, v_ref, bias_ref, o_ref):
    rows = q_ref.shape[1] // GRID_W
    win = NA_ROWS * GRID_W
    first = _lane_is_first_head((GRID_W, LANES))

    group = BLOCKS_PER_STEP
    assert rows % group == 0

    def body(it, carry):
        blocks = []
        for u in range(group):
            r = it * group + u
            r_start = jnp.clip(r - NA_ROWS // 2, 0, rows - NA_ROWS)
            q0 = pl.multiple_of(r * GRID_W, GRID_W)
            ks = pl.multiple_of(r_start * GRID_W, GRID_W)
            blocks.append((q0, ks, r - r_start, q_ref[0, pl.ds(q0, GRID_W), :], k_ref[0, pl.ds(ks, win), :]))
        scores = [[_dot_nt(_keep_head(q, first, j), k) for j in range(2)] for (_, _, _, q, k) in blocks]
        probs = []
        for (_, _, variant, _, _), s2 in zip(blocks, scores):
            row = []
            for j in range(2):
                s = s2[j] + bias_ref[j, variant]
                m = jnp.max(s, axis=-1, keepdims=True)
                p = jnp.exp(s - m)
                row.append((p.astype(BF16), jnp.sum(p, axis=-1, keepdims=True)))
            probs.append(row)
        for (q0, ks, _, _, _), row in zip(blocks, probs):
            v = v_ref[0, pl.ds(ks, win), :]
            outs = [_dot(p, v) / den for (p, den) in row]
            o_ref[0, pl.ds(q0, GRID_W), :] = jnp.where(first, outs[0], outs[1]).astype(o_ref.dtype)
        return carry

    lax.fori_loop(0, rows // group, body, 0)


def _attn_c_call(q, k, v, bias):
    b, s, w = q.shape
    pairs = w // LANES
    spec = pl.BlockSpec((1, s, LANES), lambda bi, p: (bi, 0, p))
    return pl.pallas_call(
        _attn_c_kernel,
        grid=(b, pairs),
        in_specs=[spec, spec, spec,
                  pl.BlockSpec((2,) + bias.shape[1:], lambda bi, p: (p, 0, 0, 0))],
        out_specs=spec,
        out_shape=jax.ShapeDtypeStruct((b, s, w), BF16),
        compiler_params=pltpu.CompilerParams(dimension_semantics=("parallel", "parallel"),
                                             vmem_limit_bytes=VMEM_LIMIT),
    )(q, k, v, bias)


def _out_kernel(x_ref, oa_ref, ob1_ref, l1_ref, *rest):
    nd = len(_DILATIONS)
    dil_refs = rest[:2 * nd]
    oc_ref, ga_ref, gb_ref, gc_ref, w_ref, o_ref, stage_ref = rest[2 * nd:]
    tm = x_ref.shape[0]
    nblk = WIDTH_B // LANES
    na = _rms(oa_ref[...].astype(F32), ga_ref[...]).astype(BF16)
    outs, lses = [ob1_ref[...].astype(F32)], [l1_ref[...]]
    for di, dil in enumerate(_DILATIONS):
        rows = tm // dil
        for which, acc in ((0, outs), (1, lses)):
            src = dil_refs[2 * di + which]
            slot = (2 * di + which) * nblk
            for r in range(dil):
                for blk in range(nblk):
                    stage_ref[slot + blk, pl.ds(r, rows, stride=dil), :] = (
                        src[0, r, :, blk * LANES:(blk + 1) * LANES].astype(F32))
            acc.append(jnp.concatenate([stage_ref[slot + blk] for blk in range(nblk)], axis=-1))
    lmax = functools.reduce(jnp.maximum, lses)
    es = [jnp.exp(l - lmax) for l in lses]
    ob = sum(e * o for e, o in zip(es, outs)) / sum(es)
    nb = _rms(ob, gb_ref[...]).astype(BF16)
    nc = _rms(oc_ref[...].astype(F32), gc_ref[...]).astype(BF16)
    acc = _dot(na, w_ref[0:WIDTH_A, :])
    acc += _dot(nb, w_ref[WIDTH_A:WIDTH_A + WIDTH_B, :])
    acc += _dot(nc, w_ref[WIDTH_A + WIDTH_B:, :])
    o_ref[...] = x_ref[...] + acc


def _out_call(x2, oa, ob1, l1, dilated, oc, ga, gb, gc, w, seq, tm):
    t, d = x2.shape
    nseq = seq // tm
    row = lambda i: (i, 0)
    const = lambda i: (0, 0)
    rs = lambda width: pl.BlockSpec((tm, width), row)
    cs = lambda width: pl.BlockSpec((1, width), const)
    dil_specs, dil_args = [], []
    for dil, pair in zip(_DILATIONS, dilated):
        for a in pair:
            dil_specs.append(pl.BlockSpec((1, dil, tm // dil, WIDTH_B), lambda i: (i // nseq, 0, i % nseq, 0)))
            dil_args.append(a)
    return pl.pallas_call(
        _out_kernel,
        grid=(t // tm,),
        in_specs=[rs(d), rs(WIDTH_A), rs(WIDTH_B), rs(WIDTH_B), *dil_specs,
                  rs(WIDTH_C), cs(WIDTH_A), cs(WIDTH_B), cs(WIDTH_C), pl.BlockSpec(w.shape, const)],
        out_specs=rs(d),
        out_shape=jax.ShapeDtypeStruct((t, d), F32),
        scratch_shapes=[pltpu.VMEM((2 * len(_DILATIONS) * WIDTH_B // LANES, tm, LANES), F32)],
        compiler_params=pltpu.CompilerParams(dimension_semantics=("parallel",),
                                             vmem_limit_bytes=VMEM_LIMIT),
    )(x2, oa, ob1, l1, *dil_args, oc, ga, gb, gc, w)


def _mlp_kernel(x_ref, g_ref, w1_ref, w2_ref, gf_ref, o_ref, h_ref, acc_ref, *, final_norm):
    f = pl.program_id(1)

    @pl.when(f == 0)
    def _():
        h_ref[...] = _rms(x_ref[...], g_ref[...]).astype(BF16)
        acc_ref[...] = jnp.zeros_like(acc_ref)

    u = jnp.maximum(_dot(h_ref[...], w1_ref[...]), 0.0)
    acc_ref[...] += _dot((u * u).astype(BF16), w2_ref[...])

    @pl.when(f == pl.num_programs(1) - 1)
    def _():
        y = x_ref[...] + acc_ref[...]
        if final_norm:
            y = _rms(y, gf_ref[...])
        o_ref[...] = y


def _mlp_call(x2, g, w1, w2, gf, final_norm, tm, tf):
    t, d = x2.shape
    dff = w1.shape[1]
    return pl.pallas_call(
        functools.partial(_mlp_kernel, final_norm=final_norm),
        grid=(t // tm, dff // tf),
        in_specs=[
            pl.BlockSpec((tm, d), lambda i, f: (i, 0)),
            pl.BlockSpec((1, d), lambda i, f: (0, 0)),
            pl.BlockSpec((d, tf), lambda i, f: (0, f)),
            pl.BlockSpec((tf, d), lambda i, f: (f, 0)),
            pl.BlockSpec((1, d), lambda i, f: (0, 0)),
        ],
        out_specs=pl.BlockSpec((tm, d), lambda i, f: (i, 0)),
        out_shape=jax.ShapeDtypeStruct((t, d), F32),
        scratch_shapes=[pltpu.VMEM((tm, d), BF16), pltpu.VMEM((tm, d), F32)],
        compiler_params=pltpu.CompilerParams(dimension_semantics=("parallel", "arbitrary"),
                                             vmem_limit_bytes=VMEM_LIMIT),
    )(x2, g, w1, w2, gf)


def _rotate_half_cols(w, half):
    return jnp.concatenate([-w[..., half:], w[..., :half]], axis=-1)


def _rope_tables(seq):
    pos = jnp.arange(seq, dtype=F32)

    def cos_sin(half):
        inv_freq = ROPE_THETA ** (-jnp.arange(half, dtype=F32) / half)
        ang = pos[:, None] * inv_freq[None, :]
        return jnp.cos(ang), jnp.sin(ang)

    ca, sa = cos_sin(QK_ROPE // 2)
    ca2, sa2 = jnp.concatenate([ca, ca], -1), jnp.concatenate([sa, sa], -1)
    ones = jnp.ones((seq, QK_NOPE), F32)
    zeros = jnp.zeros((seq, QK_NOPE), F32)
    tail = jnp.zeros((seq, LANES - QK_NOPE - QK_ROPE), F32)
    scale_a = (QK_NOPE + QK_ROPE) ** -0.5 * LOG2_E
    cosq = jnp.concatenate([ones, ca2, tail], -1) * scale_a
    sinq = jnp.concatenate([zeros, sa2, tail], -1) * scale_a
    cosk = jnp.concatenate([zeros, ca2, tail], -1)
    sink = jnp.concatenate([zeros, sa2, tail], -1)
    cb, sb = cos_sin(HEAD_DIM // 2)
    cosb = jnp.concatenate([cb, cb, cb, cb], -1)
    sinb = jnp.concatenate([-sb, sb, -sb, sb], -1)
    return jnp.concatenate([cosq, sinq, cosk, sink, cosb, sinb], -1)


def _layer_weights(w_in, w_uq, w_ukv):
    d = w_in.shape[0]
    scale = HEAD_DIM ** -0.5
    c_b = Q_LORA + KV_LORA + QK_ROPE
    c_c = c_b + 3 * WIDTH_B
    w_kpe = w_in[:, Q_LORA + KV_LORA:c_b]

    def place(w):
        return jnp.concatenate([jnp.zeros((d, QK_NOPE), F32), w, jnp.zeros((d, LANES - QK_NOPE - QK_ROPE), F32)], -1)

    wbig = jnp.concatenate([
        w_in[:, :Q_LORA + KV_LORA],
        place(w_kpe), place(_rotate_half_cols(w_kpe, QK_ROPE // 2)),
        w_in[:, c_b:c_b + WIDTH_B] * scale, w_in[:, c_b + WIDTH_B:c_c],
        w_in[:, c_c:c_c + WIDTH_C] * scale, w_in[:, c_c + WIDTH_C:],
    ], -1).astype(BF16)

    uq = w_uq.reshape(Q_LORA, HEADS_A, QK_NOPE + QK_ROPE)
    pad = jnp.zeros((Q_LORA, HEADS_A, LANES - QK_NOPE - QK_ROPE), F32)
    uq_pad = jnp.concatenate([uq, pad], -1)
    uq_rot = jnp.concatenate([jnp.zeros((Q_LORA, HEADS_A, QK_NOPE), F32),
                              _rotate_half_cols(uq[..., QK_NOPE:], QK_ROPE // 2), pad], -1)
    wuq = jnp.concatenate([uq_pad.reshape(Q_LORA, QK_A_PAD), uq_rot.reshape(Q_LORA, QK_A_PAD)], -1).astype(BF16)

    ukv = w_ukv.reshape(KV_LORA, HEADS_A, QK_NOPE + HEAD_DIM)
    uk_pad = jnp.concatenate([ukv[..., :QK_NOPE], jnp.zeros((KV_LORA, HEADS_A, LANES - QK_NOPE), F32)], -1)
    wukv = jnp.concatenate([uk_pad.reshape(KV_LORA, QK_A_PAD), ukv[..., QK_NOPE:].reshape(KV_LORA, WIDTH_A)],
                           -1).astype(BF16)
    return wbig, wuq, wukv


def kernel(x, g_mix, w_in, q_norm, w_uq, kv_norm, w_ukv, rpb, out_norm_a, out_norm_b, out_norm_c, w_out, g_mlp,
           w_mlp_in, w_mlp_out, g_final):
    b, s, d = x.shape
    depth = w_in.shape[0]
    t = b * s
    assert s % GRID_W == 0 and s // GRID_W >= NA_ROWS
    tm_proj = min(512, s)
    tm_out = min(512, s)
    tm_mlp = min(1024, t)
    tq_a = min(256, s)
    tk_a = min(512, s)

    tabs = _rope_tables(s)
    na_bias = _na_bias_call(rpb).reshape(depth, HEADS_C, NA_ROWS, GRID_W, NA_ROWS * GRID_W)
    row = lambda a: a.reshape(1, -1)

    x2 = x.reshape(t, d)
    for l in range(depth):
        wbig, wuq, wukv = _layer_weights(w_in[l], w_uq[l], w_ukv[l])
        qa, ka, va, qb, kb, vb, qc, kc, vc, *dil_qkv = _proj_call(
            x2, row(g_mix[l]), wbig, row(q_norm[l]), wuq, row(kv_norm[l]), wukv, tabs, s, tm_proj)
        seq3 = lambda a: a.reshape(b, s, a.shape[-1])
        oa = _attn_a_call(seq3(qa), seq3(ka), seq3(va), tq_a, tk_a).reshape(t, WIDTH_A)
        ob1, l1, dilated = None, None, []
        for window, dil in DILATED_PAIRS:
            half = window // (2 * dil)
            if dil == 1:
                unit = lambda a: a.reshape(b, 1, s, WIDTH_B)
                o_i, lse_i = _attn_b_call(unit(qb), unit(kb), unit(vb), half, 128)
                ob1, l1 = o_i.reshape(t, WIDTH_B), lse_i.reshape(t, WIDTH_B)
            else:
                di = 3 * _DILATIONS.index(dil)
                dilated.append(_attn_b_call(*dil_qkv[di:di + 3], half, 128))
        oc = _attn_c_call(seq3(qc), seq3(kc), seq3(vc), na_bias[l]).reshape(t, WIDTH_C)
        x2 = _out_call(x2, oa, ob1, l1, dilated, oc, row(out_norm_a[l]), row(out_norm_b[l]), row(out_norm_c[l]),
                       w_out[l].astype(BF16), s, tm_out)
        x2 = _mlp_call(x2, row(g_mlp[l]), w_mlp_in[l].astype(BF16), w_mlp_out[l].astype(BF16), row(g_final),
                       l == depth - 1, tm_mlp, min(1024, w_mlp_in.shape[2]))
    return x2.reshape(b, s, d)
```

```python
import functools

import jax
import jax.numpy as jnp
from jax import lax
from jax.experimental import pallas as pl
from jax.experimental.pallas import tpu as pltpu

HEAD_DIM = 64
LANES = 128
HEADS_A = 6
HEADS_B = 6
HEADS_C = 4
Q_LORA = 256
KV_LORA = 128
QK_NOPE = 64
QK_ROPE = 32
DILATED_PAIRS = ((128, 1), (512, 4), (2048, 16))
_DILATIONS = tuple(dil for _, dil in DILATED_PAIRS if dil > 1)
GRID_W = 64
NA_ROWS = 8
NA_COLS = 16
ROPE_THETA = 10000.0
NORM_EPS = 1e-6
NEG_INF = -1e30
LOG2_E = 1.4426950408889634
BLOCKS_PER_STEP = 4

WIDTH_A = HEADS_A * HEAD_DIM
WIDTH_B = HEADS_B * HEAD_DIM
WIDTH_C = HEADS_C * HEAD_DIM
QK_A_PAD = HEADS_A * LANES
RPB_PER_HEAD = (2 * NA_ROWS - 1) * (2 * NA_COLS - 1)

VMEM_LIMIT = 56 * 1024 * 1024

BF16 = jnp.bfloat16
F32 = jnp.float32


def _rms(x, g):
    return x * lax.rsqrt(jnp.mean(x * x, axis=-1, keepdims=True) + NORM_EPS) * g


def _dot(a, b):
    return jnp.dot(a, b, preferred_element_type=F32)


def _dot_nt(a, b):
    return lax.dot_general(a, b, (((1,), (1,)), ((), ())), preferred_element_type=F32)


def _lane_is_first_head(shape):
    return lax.broadcasted_iota(jnp.int32, shape, len(shape) - 1) < HEAD_DIM


def _keep_head(q, first, j):
    zero = jnp.zeros_like(q)
    return jnp.where(first, q, zero) if j == 0 else jnp.where(first, zero, q)


_C_CQ = 0
_C_CKV = _C_CQ + Q_LORA
_C_KPE = _C_CKV + KV_LORA
_C_KPR = _C_KPE + LANES
_C_QB = _C_KPR + LANES
_C_KB = _C_QB + WIDTH_B
_C_VB = _C_KB + WIDTH_B
_C_QC = _C_VB + WIDTH_B
_C_KC = _C_QC + WIDTH_C
_C_VC = _C_KC + WIDTH_C
_C_END = _C_VC + WIDTH_C

_T_COSQ, _T_SINQ, _T_COSK, _T_SINK, _T_COSB, _T_SINB = range(6)


def _proj_kernel(x_ref, g_ref, wbig_ref, qn_ref, wuq_ref, kvn_ref, wukv_ref, tab_ref,
                 qa_ref, ka_ref, va_ref, qb_ref, kb_ref, vb_ref, qc_ref, kc_ref, vc_ref, *rest):
    dil_refs, stage_ref = rest[:-1], rest[-1]
    tm = x_ref.shape[0]

    def tab(i):
        return tab_ref[:, i * LANES:(i + 1) * LANES]

    h = _rms(x_ref[...], g_ref[...]).astype(BF16)
    proj = _dot(h, wbig_ref[...])

    cqn = _rms(proj[:, _C_CQ:_C_CKV], qn_ref[...]).astype(BF16)
    qa2 = _dot(cqn, wuq_ref[...])
    ckvn = _rms(proj[:, _C_CKV:_C_KPE], kvn_ref[...]).astype(BF16)
    kv2 = _dot(ckvn, wukv_ref[...])
    kpe = proj[:, _C_KPE:_C_KPR] * tab(_T_COSK) + proj[:, _C_KPR:_C_QB] * tab(_T_SINK)
    cosq, sinq = tab(_T_COSQ), tab(_T_SINQ)
    for hd in range(HEADS_A):
        sl = slice(hd * LANES, (hd + 1) * LANES)
        rot = slice(QK_A_PAD + hd * LANES, QK_A_PAD + (hd + 1) * LANES)
        qa_ref[:, sl] = (qa2[:, sl] * cosq + qa2[:, rot] * sinq).astype(BF16)
        ka_ref[:, sl] = (kv2[:, sl] + kpe).astype(BF16)
    va_ref[...] = kv2[:, QK_A_PAD:].astype(BF16)

    cosb, sinb = tab(_T_COSB), tab(_T_SINB)
    first_half = (lax.broadcasted_iota(jnp.int32, cosb.shape, 1) % HEAD_DIM) < HEAD_DIM // 2
    nblk = WIDTH_B // LANES
    for ti, (src, dst) in enumerate(((_C_QB, qb_ref), (_C_KB, kb_ref), (_C_VB, vb_ref))):
        for blk in range(nblk):
            xb = proj[:, src + blk * LANES:src + (blk + 1) * LANES]
            if dst is not vb_ref:
                swapped = jnp.where(first_half, pltpu.roll(xb, LANES - HEAD_DIM // 2, 1),
                                    pltpu.roll(xb, HEAD_DIM // 2, 1))
                xb = xb * cosb + swapped * sinb
            dst[:, blk * LANES:(blk + 1) * LANES] = xb.astype(BF16)
            stage_ref[ti * nblk + blk] = xb
    for di, dil in enumerate(_DILATIONS):
        rows = tm // dil
        for ti in range(3):
            dref = dil_refs[di * 3 + ti]
            for r in range(dil):
                for blk in range(nblk):
                    dref[0, r, :, blk * LANES:(blk + 1) * LANES] = (
                        stage_ref[ti * nblk + blk, pl.ds(r, rows, stride=dil), :].astype(BF16))

    qc_ref[...] = proj[:, _C_QC:_C_KC].astype(BF16)
    kc_ref[...] = proj[:, _C_KC:_C_VC].astype(BF16)
    vc_ref[...] = proj[:, _C_VC:_C_END].astype(BF16)


def _proj_call(x2, g, wbig, qn, wuq, kvn, wukv, tabs, seq, tm):
    t, d = x2.shape
    nseq = seq // tm
    row = lambda i: (i, 0)
    const = lambda i: (0, 0)
    widths = (QK_A_PAD, QK_A_PAD, WIDTH_A, WIDTH_B, WIDTH_B, WIDTH_B, WIDTH_C, WIDTH_C, WIDTH_C)
    out_specs = [pl.BlockSpec((tm, w), row) for w in widths]
    out_shape = [jax.ShapeDtypeStruct((t, w), BF16) for w in widths]
    for dil in _DILATIONS:
        assert tm % (16 * dil) == 0
        for _ in range(3):
            out_specs.append(pl.BlockSpec((1, dil, tm // dil, WIDTH_B), lambda i: (i // nseq, 0, i % nseq, 0)))
            out_shape.append(jax.ShapeDtypeStruct((t // seq, dil, seq // dil, WIDTH_B), BF16))
    return pl.pallas_call(
        _proj_kernel,
        grid=(t // tm,),
        in_specs=[
            pl.BlockSpec((tm, d), row),
            pl.BlockSpec((1, d), const),
            pl.BlockSpec(wbig.shape, const),
            pl.BlockSpec((1, Q_LORA), const),
            pl.BlockSpec(wuq.shape, const),
            pl.BlockSpec((1, KV_LORA), const),
            pl.BlockSpec(wukv.shape, const),
            pl.BlockSpec((tm, tabs.shape[1]), lambda i: (i % nseq, 0)),
        ],
        out_specs=out_specs,
        out_shape=out_shape,
        scratch_shapes=[pltpu.VMEM((3 * WIDTH_B // LANES, tm, LANES), F32)],
        compiler_params=pltpu.CompilerParams(dimension_semantics=("parallel",),
                                             vmem_limit_bytes=VMEM_LIMIT),
    )(x2, g, wbig, qn, wuq, kvn, wukv, tabs)


def _attn_a_kernel(qt_ref, k_ref, vt_ref, o_ref, a0_ref, a1_ref, b0_ref, b1_ref, *, tk):
    tq = qt_ref.shape[2]
    nk = k_ref.shape[1] // tk
    st_a, st_b = (a0_ref, a1_ref), (b0_ref, b1_ref)

    def scores(c, st_ref):
        ks = pl.multiple_of(c * tk, tk)
        for j in range(2):
            st_ref[j] = _dot(k_ref[0, pl.ds(ks, tk), j * LANES:(j + 1) * LANES],
                             qt_ref[0, j * LANES:(j + 1) * LANES, :])

    def accumulate(c, st_ref, carry):
        ks = pl.multiple_of(c * tk, tk)
        stats = []
        for j in range(2):
            m, l, _ = carry[j]
            st = st_ref[j]
            m_new = jnp.maximum(m, jnp.max(st, axis=0, keepdims=True))
            alpha = jnp.exp2(m - m_new)
            pt = jnp.exp2(st - m_new)
            l = alpha * l + jnp.sum(pt, axis=0, keepdims=True)
            stats.append((m_new, l, alpha, pt.astype(BF16)))
        new = []
        for j in range(2):
            m_new, l, alpha, pt = stats[j]
            vt = vt_ref[0, j * HEAD_DIM:(j + 1) * HEAD_DIM, pl.ds(ks, tk)]
            new.append((m_new, l, alpha * carry[j][2] + _dot(vt, pt)))
        return tuple(new)

    def half_step(c, src, dst, carry, issue=True):
        for u in range(2):
            if issue:
                scores(c + 2 + u, dst[u])
            carry = accumulate(c + u, src[u], carry)
        return carry

    def body(i, carry):
        carry = half_step(4 * i, st_a, st_b, carry)
        return half_step(4 * i + 2, st_b, st_a, carry)

    init = tuple((jnp.full((1, tq), NEG_INF, F32), jnp.zeros((1, tq), F32), jnp.zeros((HEAD_DIM, tq), F32))
                 for _ in range(2))
    scores(0, st_a[0])
    scores(1, st_a[1])
    carry = lax.fori_loop(0, nk // 4 - 1, body, init)
    carry = half_step(nk - 4, st_a, st_b, carry)
    res = half_step(nk - 2, st_b, st_a, carry, issue=False)
    for j in range(2):
        _, l, acc = res[j]
        o_ref[0, j * HEAD_DIM:(j + 1) * HEAD_DIM, :] = (acc / l).astype(o_ref.dtype)


def _attn_a_call(qa, ka, va, tq, tk):
    b, s, _ = qa.shape
    pairs = HEADS_A // 2
    assert s % (4 * tk) == 0
    qt = jnp.swapaxes(qa, 1, 2)
    vt = jnp.swapaxes(va, 1, 2)
    ot = pl.pallas_call(
        functools.partial(_attn_a_kernel, tk=tk),
        grid=(b, pairs, s // tq),
        in_specs=[
            pl.BlockSpec((1, 2 * LANES, tq), lambda bi, p, qi: (bi, p, qi)),
            pl.BlockSpec((1, s, 2 * LANES), lambda bi, p, qi: (bi, 0, p)),
            pl.BlockSpec((1, LANES, s), lambda bi, p, qi: (bi, p, 0)),
        ],
        out_specs=pl.BlockSpec((1, LANES, tq), lambda bi, p, qi: (bi, p, qi)),
        out_shape=jax.ShapeDtypeStruct((b, WIDTH_A, s), BF16),
        scratch_shapes=[pltpu.VMEM((2, tk, tq), F32) for _ in range(4)],
        compiler_params=pltpu.CompilerParams(dimension_semantics=("parallel", "parallel", "arbitrary"),
                                             vmem_limit_bytes=VMEM_LIMIT),
    )(qt, ka, vt)
    return jnp.swapaxes(ot, 1, 2)


def _attn_b_kernel(q_ref, k_ref, v_ref, o_ref, lse_ref, *, tq, half):
    n = q_ref.shape[0]
    kw = min(tq + 2 * half, n)
    first = _lane_is_first_head((tq, LANES))

    group = min(BLOCKS_PER_STEP, n // tq)
    diff = (lax.broadcasted_iota(jnp.int32, (tq, kw), 0) - lax.broadcasted_iota(jnp.int32, (tq, kw), 1))

    def body(it, carry):
        blocks = []
        for u in range(group):
            q0 = pl.multiple_of((it * group + u) * tq, tq)
            ks = pl.multiple_of(jnp.clip(q0 - half, 0, n - kw), half)
            blocks.append((q0, ks, q_ref[pl.ds(q0, tq), :], k_ref[pl.ds(ks, kw), :]))
        scores = [[_dot_nt(_keep_head(q, first, j), k) for j in range(2)] for (_, _, q, k) in blocks]
        probs = []
        for (q0, ks, _, _), s2 in zip(blocks, scores):
            mask = jnp.abs(diff + (q0 - ks)) <= half
            row = []
            for j in range(2):
                s = jnp.where(mask, s2[j], NEG_INF)
                m = jnp.max(s, axis=-1, keepdims=True)
                p = jnp.exp(s - m)
                den = jnp.sum(p, axis=-1, keepdims=True)
                row.append((p.astype(BF16), den, m + jnp.log(den)))
            probs.append(row)
        for (q0, ks, _, _), row in zip(blocks, probs):
            v = v_ref[pl.ds(ks, kw), :]
            outs = [_dot(p, v) / den for (p, den, _) in row]
            lses = [jnp.broadcast_to(lse, (tq, LANES)) for (_, _, lse) in row]
            o_ref[pl.ds(q0, tq), :] = jnp.where(first, outs[0], outs[1]).astype(o_ref.dtype)
            lse_ref[pl.ds(q0, tq), :] = jnp.where(first, lses[0], lses[1])
        return carry

    lax.fori_loop(0, n // (tq * group), body, 0)


def _attn_b_call(q, k, v, half, tq):
    b, dil, n, w = q.shape
    pairs = w // LANES
    spec = pl.BlockSpec((None, None, n, LANES), lambda bi, r, p: (bi, r, 0, p))
    return pl.pallas_call(
        functools.partial(_attn_b_kernel, tq=min(tq, n), half=half),
        grid=(b, dil, pairs),
        in_specs=[spec, spec, spec],
        out_specs=[spec, spec],
        out_shape=[jax.ShapeDtypeStruct(q.shape, BF16), jax.ShapeDtypeStruct(q.shape, F32)],
        compiler_params=pltpu.CompilerParams(dimension_semantics=("parallel", "parallel", "parallel"),
                                             vmem_limit_bytes=VMEM_LIMIT),
    )(q, k, v)


def _na_bias_kernel(rpb_ref, o_ref):
    base = pl.program_id(0) * RPB_PER_HEAD
    shape = (GRID_W, LANES)
    lane = lax.broadcasted_iota(jnp.int32, shape, 1)
    p = lax.broadcasted_iota(jnp.int32, shape, 0)
    c = lane % GRID_W
    upper = lane >= GRID_W
    c_start = jnp.clip(p - NA_COLS // 2, 0, GRID_W - NA_COLS)
    col_ok = (c >= c_start) & (c < c_start + NA_COLS)
    dc = c - p + (NA_COLS - 1)
    n_dc = 2 * NA_COLS - 1
    for v in range(NA_ROWS):
        for m in range(NA_ROWS * GRID_W // LANES):
            a_lo = 2 * m - v + (NA_ROWS - 1)
            acc = jnp.full(shape, NEG_INF, F32)
            for b in range(n_dc):
                val = jnp.where(upper, rpb_ref[base + (a_lo + 1) * n_dc + b], rpb_ref[base + a_lo * n_dc + b])
                acc = jnp.where(dc == b, val, acc)
            o_ref[0, v, :, m * LANES:(m + 1) * LANES] = jnp.where(col_ok, acc, NEG_INF)


def _na_bias_call(rpb):
    nh = rpb.shape[0] * rpb.shape[1]
    return pl.pallas_call(
        _na_bias_kernel,
        grid=(nh,),
        in_specs=[pl.BlockSpec(memory_space=pltpu.SMEM)],
        out_specs=pl.BlockSpec((1, NA_ROWS, GRID_W, NA_ROWS * GRID_W), lambda g: (g, 0, 0, 0)),
        out_shape=jax.ShapeDtypeStruct((nh, NA_ROWS, GRID_W, NA_ROWS * GRID_W), F32),
    )(rpb.reshape(-1))


def _attn_c_kernel(q_ref, k_ref, v_ref, bias_ref, o_ref):
    rows = q_ref.shape[1] // GRID_W
    win = NA_ROWS * GRID_W
    first = _lane_is_first_head((GRID_W, LANES))

    group = BLOCKS_PER_STEP
    assert rows % group == 0

    def body(it, carry):
        blocks = []
        for u in range(group):
            r = it * group + u
            r_start = jnp.clip(r - NA_ROWS // 2, 0, rows - NA_ROWS)
            q0 = pl.multiple_of(r * GRID_W, GRID_W)
            ks = pl.multiple_of(r_start * GRID_W, GRID_W)
            blocks.append((q0, ks, r - r_start, q_ref[0, pl.ds(q0, GRID_W), :], k_ref[0, pl.ds(ks, win), :]))
        scores = [[_dot_nt(_keep_head(q, first, j), k) for j in range(2)] for (_, _, _, q, k) in blocks]
        probs = []
        for (_, _, variant, _, _), s2 in zip(blocks, scores):
            row = []
            for j in range(2):
                s = s2[j] + bias_ref[j, variant]
                m = jnp.max(s, axis=-1, keepdims=True)
                p = jnp.exp(s - m)
                row.append((p.astype(BF16), jnp.sum(p, axis=-1, keepdims=True)))
            probs.append(row)
        for (q0, ks, _, _, _), row in zip(blocks, probs):
            v = v_ref[0, pl.ds(ks, win), :]
            outs = [_dot(p, v) / den for (p, den) in row]
            o_ref[0, pl.ds(q0, GRID_W), :] = jnp.where(first, outs[0], outs[1]).astype(o_ref.dtype)
        return carry

    lax.fori_loop(0, rows // group, body, 0)


def _attn_c_call(q, k, v, bias):
    b, s, w = q.shape
    pairs = w // LANES
    spec = pl.BlockSpec((1, s, LANES), lambda bi, p: (bi, 0, p))
    return pl.pallas_call(
        _attn_c_kernel,
        grid=(b, pairs),
        in_specs=[spec, spec, spec,
                  pl.BlockSpec((2,) + bias.shape[1:], lambda bi, p: (p, 0, 0, 0))],
        out_specs=spec,
        out_shape=jax.ShapeDtypeStruct((b, s, w), BF16),
        compiler_params=pltpu.CompilerParams(dimension_semantics=("parallel", "parallel"),
                                             vmem_limit_bytes=VMEM_LIMIT),
    )(q, k, v, bias)


def _out_kernel(x_ref, oa_ref, ob1_ref, l1_ref, *rest):
    nd = len(_DILATIONS)
    dil_refs = rest[:2 * nd]
    oc_ref, ga_ref, gb_ref, gc_ref, w_ref, o_ref, stage_ref = rest[2 * nd:]
    tm = x_ref.shape[0]
    nblk = WIDTH_B // LANES
    na = _rms(oa_ref[...].astype(F32), ga_ref[...]).astype(BF16)
    outs, lses = [ob1_ref[...].astype(F32)], [l1_ref[...]]
    for di, dil in enumerate(_DILATIONS):
        rows = tm // dil
        for which, acc in ((0, outs), (1, lses)):
            src = dil_refs[2 * di + which]
            slot = (2 * di + which) * nblk
            for r in range(dil):
                for blk in range(nblk):
                    stage_ref[slot + blk, pl.ds(r, rows, stride=dil), :] = (
                        src[0, r, :, blk * LANES:(blk + 1) * LANES].astype(F32))
            acc.append(jnp.concatenate([stage_ref[slot + blk] for blk in range(nblk)], axis=-1))
    lmax = functools.reduce(jnp.maximum, lses)
    es = [jnp.exp(l - lmax) for l in lses]
    ob = sum(e * o for e, o in zip(es, outs)) / sum(es)
    nb = _rms(ob, gb_ref[...]).astype(BF16)
    nc = _rms(oc_ref[...].astype(F32), gc_ref[...]).astype(BF16)
    acc = _dot(na, w_ref[0:WIDTH_A, :])
    acc += _dot(nb, w_ref[WIDTH_A:WIDTH_A + WIDTH_B, :])
    acc += _dot(nc, w_ref[WIDTH_A + WIDTH_B:, :])
    o_ref[...] = x_ref[...] + acc


def _out_call(x2, oa, ob1, l1, dilated, oc, ga, gb, gc, w, seq, tm):
    t, d = x2.shape
    nseq = seq // tm
    row = lambda i: (i, 0)
    const = lambda i: (0, 0)
    rs = lambda width: pl.BlockSpec((tm, width), row)
    cs = lambda width: pl.BlockSpec((1, width), const)
    dil_specs, dil_args = [], []
    for dil, pair in zip(_DILATIONS, dilated):
        for a in pair:
            dil_specs.append(pl.BlockSpec((1, dil, tm // dil, WIDTH_B), lambda i: (i // nseq, 0, i % nseq, 0)))
            dil_args.append(a)
    return pl.pallas_call(
        _out_kernel,
        grid=(t // tm,),
        in_specs=[rs(d), rs(WIDTH_A), rs(WIDTH_B), rs(WIDTH_B), *dil_specs,
                  rs(WIDTH_C), cs(WIDTH_A), cs(WIDTH_B), cs(WIDTH_C), pl.BlockSpec(w.shape, const)],
        out_specs=rs(d),
        out_shape=jax.ShapeDtypeStruct((t, d), F32),
        scratch_shapes=[pltpu.VMEM((2 * len(_DILATIONS) * WIDTH_B // LANES, tm, LANES), F32)],
        compiler_params=pltpu.CompilerParams(dimension_semantics=("parallel",),
                                             vmem_limit_bytes=VMEM_LIMIT),
    )(x2, oa, ob1, l1, *dil_args, oc, ga, gb, gc, w)


def _mlp_kernel(x_ref, g_ref, w1_ref, w2_ref, gf_ref, o_ref, h_ref, acc_ref, *, final_norm):
    f = pl.program_id(1)

    @pl.when(f == 0)
    def _():
        h_ref[...] = _rms(x_ref[...], g_ref[...]).astype(BF16)
        acc_ref[...] = jnp.zeros_like(acc_ref)

    u = jnp.maximum(_dot(h_ref[...], w1_ref[...]), 0.0)
    acc_ref[...] += _dot((u * u).astype(BF16), w2_ref[...])

    @pl.when(f == pl.num_programs(1) - 1)
    def _():
        y = x_ref[...] + acc_ref[...]
        if final_norm:
            y = _rms(y, gf_ref[...])
        o_ref[...] = y


def _mlp_call(x2, g, w1, w2, gf, final_norm, tm, tf):
    t, d = x2.shape
    dff = w1.shape[1]
    return pl.pallas_call(
        functools.partial(_mlp_kernel, final_norm=final_norm),
        grid=(t // tm, dff // tf),
        in_specs=[
            pl.BlockSpec((tm, d), lambda i, f: (i, 0)),
            pl.BlockSpec((1, d), lambda i, f: (0, 0)),
            pl.BlockSpec((d, tf), lambda i, f: (0, f)),
            pl.BlockSpec((tf, d), lambda i, f: (f, 0)),
            pl.BlockSpec((1, d), lambda i, f: (0, 0)),
        ],
        out_specs=pl.BlockSpec((tm, d), lambda i, f: (i, 0)),
        out_shape=jax.ShapeDtypeStruct((t, d), F32),
        scratch_shapes=[pltpu.VMEM((tm, d), BF16), pltpu.VMEM((tm, d), F32)],
        compiler_params=pltpu.CompilerParams(dimension_semantics=("parallel", "arbitrary"),
                                             vmem_limit_bytes=VMEM_LIMIT),
    )(x2, g, w1, w2, gf)


def _rotate_half_cols(w, half):
    return jnp.concatenate([-w[..., half:], w[..., :half]], axis=-1)


def _rope_tables(seq):
    pos = jnp.arange(seq, dtype=F32)

    def cos_sin(half):
        inv_freq = ROPE_THETA ** (-jnp.arange(half, dtype=F32) / half)
        ang = pos[:, None] * inv_freq[None, :]
        return jnp.cos(ang), jnp.sin(ang)

    ca, sa = cos_sin(QK_ROPE // 2)
    ca2, sa2 = jnp.concatenate([ca, ca], -1), jnp.concatenate([sa, sa], -1)
    ones = jnp.ones((seq, QK_NOPE), F32)
    zeros = jnp.zeros((seq, QK_NOPE), F32)
    tail = jnp.zeros((seq, LANES - QK_NOPE - QK_ROPE), F32)
    scale_a = (QK_NOPE + QK_ROPE) ** -0.5 * LOG2_E
    cosq = jnp.concatenate([ones, ca2, tail], -1) * scale_a
    sinq = jnp.concatenate([zeros, sa2, tail], -1) * scale_a
    cosk = jnp.concatenate([zeros, ca2, tail], -1)
    sink = jnp.concatenate([zeros, sa2, tail], -1)
    cb, sb = cos_sin(HEAD_DIM // 2)
    cosb = jnp.concatenate([cb, cb, cb, cb], -1)
    sinb = jnp.concatenate([-sb, sb, -sb, sb], -1)
    return jnp.concatenate([cosq, sinq, cosk, sink, cosb, sinb], -1)


def _layer_weights(w_in, w_uq, w_ukv):
    d = w_in.shape[0]
    scale = HEAD_DIM ** -0.5
    c_b = Q_LORA + KV_LORA + QK_ROPE
    c_c = c_b + 3 * WIDTH_B
    w_kpe = w_in[:, Q_LORA + KV_LORA:c_b]

    def place(w):
        return jnp.concatenate([jnp.zeros((d, QK_NOPE), F32), w, jnp.zeros((d, LANES - QK_NOPE - QK_ROPE), F32)], -1)

    wbig = jnp.concatenate([
        w_in[:, :Q_LORA + KV_LORA],
        place(w_kpe), place(_rotate_half_cols(w_kpe, QK_ROPE // 2)),
        w_in[:, c_b:c_b + WIDTH_B] * scale, w_in[:, c_b + WIDTH_B:c_c],
        w_in[:, c_c:c_c + WIDTH_C] * scale, w_in[:, c_c + WIDTH_C:],
    ], -1).astype(BF16)

    uq = w_uq.reshape(Q_LORA, HEADS_A, QK_NOPE + QK_ROPE)
    pad = jnp.zeros((Q_LORA, HEADS_A, LANES - QK_NOPE - QK_ROPE), F32)
    uq_pad = jnp.concatenate([uq, pad], -1)
    uq_rot = jnp.concatenate([jnp.zeros((Q_LORA, HEADS_A, QK_NOPE), F32),
                              _rotate_half_cols(uq[..., QK_NOPE:], QK_ROPE // 2), pad], -1)
    wuq = jnp.concatenate([uq_pad.reshape(Q_LORA, QK_A_PAD), uq_rot.reshape(Q_LORA, QK_A_PAD)], -1).astype(BF16)

    ukv = w_ukv.reshape(KV_LORA, HEADS_A, QK_NOPE + HEAD_DIM)
    uk_pad = jnp.concatenate([ukv[..., :QK_NOPE], jnp.zeros((KV_LORA, HEADS_A, LANES - QK_NOPE), F32)], -1)
    wukv = jnp.concatenate([uk_pad.reshape(KV_LORA, QK_A_PAD), ukv[..., QK_NOPE:].reshape(KV_LORA, WIDTH_A)],
                           -1).astype(BF16)
    return wbig, wuq, wukv


def kernel(x, g_mix, w_in, q_norm, w_uq, kv_norm, w_ukv, rpb, out_norm_a, out_norm_b, out_norm_c, w_out, g_mlp,
           w_mlp_in, w_mlp_out, g_final):
    b, s, d = x.shape
    depth = w_in.shape[0]
    t = b * s
    assert s % GRID_W == 0 and s // GRID_W >= NA_ROWS
    tm_proj = min(512, s)
    tm_out = min(512, s)
    tm_mlp = min(1024, t)
    tq_a = min(256, s)
    tk_a = min(512, s)

    tabs = _rope_tables(s)
    na_bias = _na_bias_call(rpb).reshape(depth, HEADS_C, NA_ROWS, GRID_W, NA_ROWS * GRID_W)
    row = lambda a: a.reshape(1, -1)

    x2 = x.reshape(t, d)
    for l in range(depth):
        wbig, wuq, wukv = _layer_weights(w_in[l], w_uq[l], w_ukv[l])
        qa, ka, va, qb, kb, vb, qc, kc, vc, *dil_qkv = _proj_call(
            x2, row(g_mix[l]), wbig, row(q_norm[l]), wuq, row(kv_norm[l]), wukv, tabs, s, tm_proj)
        seq3 = lambda a: a.reshape(b, s, a.shape[-1])
        oa = _attn_a_call(seq3(qa), seq3(ka), seq3(va), tq_a, tk_a).reshape(t, WIDTH_A)
        ob1, l1, dilated = None, None, []
        for window, dil in DILATED_PAIRS:
            half = window // (2 * dil)
            if dil == 1:
                unit = lambda a: a.reshape(b, 1, s, WIDTH_B)
                o_i, lse_i = _attn_b_call(unit(qb), unit(kb), unit(vb), half, 128)
                ob1, l1 = o_i.reshape(t, WIDTH_B), lse_i.reshape(t, WIDTH_B)
            else:
                di = 3 * _DILATIONS.index(dil)
                dilated.append(_attn_b_call(*dil_qkv[di:di + 3], half, 128))
        oc = _attn_c_call(seq3(qc), seq3(kc), seq3(vc), na_bias[l]).reshape(t, WIDTH_C)
        x2 = _out_call(x2, oa, ob1, l1, dilated, oc, row(out_norm_a[l]), row(out_norm_b[l]), row(out_norm_c[l]),
                       w_out[l].astype(BF16), s, tm_out)
        x2 = _mlp_call(x2, row(g_mlp[l]), w_mlp_in[l].astype(BF16), w_mlp_out[l].astype(BF16), row(g_final),
                       l == depth - 1, tm_mlp, min(1024, w_mlp_in.shape[2]))
    return x2.reshape(b, s, d)
```

```python
import functools

import jax
import jax.numpy as jnp
from jax import lax
from jax.experimental import pallas as pl
from jax.experimental.pallas import tpu as pltpu

HEAD_DIM = 64
LANES = 128
BF16_ROWS = 16
HEADS_A = 6
HEADS_B = 6
HEADS_C = 4
Q_LORA = 256
KV_LORA = 128
QK_NOPE = 64
QK_ROPE = 32
DILATED_PAIRS = ((128, 1), (512, 4), (2048, 16))
_DILATIONS = tuple(dil for _, dil in DILATED_PAIRS if dil > 1)
GRID_W = 64
NA_ROWS = 8
NA_COLS = 16
ROPE_THETA = 10000.0
NORM_EPS = 1e-6
NEG_INF = -1e30
LOG2_E = 1.4426950408889634
BLOCKS_PER_STEP = 4

WIDTH_A = HEADS_A * HEAD_DIM
WIDTH_B = HEADS_B * HEAD_DIM
WIDTH_C = HEADS_C * HEAD_DIM
QK_A_PAD = HEADS_A * LANES
RPB_PER_HEAD = (2 * NA_ROWS - 1) * (2 * NA_COLS - 1)

VMEM_LIMIT = 56 * 1024 * 1024

BF16 = jnp.bfloat16
F32 = jnp.float32


def _rms(x, g):
    return x * lax.rsqrt(jnp.mean(x * x, axis=-1, keepdims=True) + NORM_EPS) * g


def _dot(a, b):
    return jnp.dot(a, b, preferred_element_type=F32)


def _dot_nt(a, b):
    return lax.dot_general(a, b, (((1,), (1,)), ((), ())), preferred_element_type=F32)


def _lane_is_first_head(shape):
    return lax.broadcasted_iota(jnp.int32, shape, len(shape) - 1) < HEAD_DIM


def _keep_head(q, first, j):
    zero = jnp.zeros_like(q)
    return jnp.where(first, q, zero) if j == 0 else jnp.where(first, zero, q)


_C_CQ = 0
_C_CKV = _C_CQ + Q_LORA
_C_KPE = _C_CKV + KV_LORA
_C_KPR = _C_KPE + LANES
_C_QB = _C_KPR + LANES
_C_KB = _C_QB + WIDTH_B
_C_VB = _C_KB + WIDTH_B
_C_QC = _C_VB + WIDTH_B
_C_KC = _C_QC + WIDTH_C
_C_VC = _C_KC + WIDTH_C
_C_END = _C_VC + WIDTH_C

_T_COSQ, _T_SINQ, _T_COSK, _T_SINK, _T_COSB, _T_SINB = range(6)


def _proj_kernel(x_ref, g_ref, wbig_ref, qn_ref, wuq_ref, kvn_ref, wukv_ref, tab_ref,
                 qa_ref, ka_ref, va_ref, qb_ref, kb_ref, vb_ref, qc_ref, kc_ref, vc_ref, *rest):
    dil_refs, stage_ref = rest[:-1], rest[-1]
    tm = x_ref.shape[0]

    def tab(i):
        return tab_ref[:, i * LANES:(i + 1) * LANES]

    h = _rms(x_ref[...], g_ref[...]).astype(BF16)
    proj = _dot(h, wbig_ref[...])

    cqn = _rms(proj[:, _C_CQ:_C_CKV], qn_ref[...]).astype(BF16)
    qa2 = _dot(cqn, wuq_ref[...])
    ckvn = _rms(proj[:, _C_CKV:_C_KPE], kvn_ref[...]).astype(BF16)
    kv2 = _dot(ckvn, wukv_ref[...])
    kpe = proj[:, _C_KPE:_C_KPR] * tab(_T_COSK) + proj[:, _C_KPR:_C_QB] * tab(_T_SINK)
    cosq, sinq = tab(_T_COSQ), tab(_T_SINQ)
    for hd in range(HEADS_A):
        sl = slice(hd * LANES, (hd + 1) * LANES)
        rot = slice(QK_A_PAD + hd * LANES, QK_A_PAD + (hd + 1) * LANES)
        qa_ref[:, sl] = (qa2[:, sl] * cosq + qa2[:, rot] * sinq).astype(BF16)
        ka_ref[:, sl] = (kv2[:, sl] + kpe).astype(BF16)
    va_ref[...] = kv2[:, QK_A_PAD:].astype(BF16)

    cosb, sinb = tab(_T_COSB), tab(_T_SINB)
    first_half = (lax.broadcasted_iota(jnp.int32, cosb.shape, 1) % HEAD_DIM) < HEAD_DIM // 2
    nblk = WIDTH_B // LANES
    for ti, (src, dst) in enumerate(((_C_QB, qb_ref), (_C_KB, kb_ref), (_C_VB, vb_ref))):
        for blk in range(nblk):
            xb = proj[:, src + blk * LANES:src + (blk + 1) * LANES]
            if dst is not vb_ref:
                swapped = jnp.where(first_half, pltpu.roll(xb, LANES - HEAD_DIM // 2, 1),
                                    pltpu.roll(xb, HEAD_DIM // 2, 1))
                xb = xb * cosb + swapped * sinb
            dst[:, blk * LANES:(blk + 1) * LANES] = xb.astype(BF16)
            stage_ref[ti * nblk + blk] = xb
    for di, dil in enumerate(_DILATIONS):
        rows = tm // dil
        for ti in range(3):
            dref = dil_refs[di * 3 + ti]
            for r in range(dil):
                for blk in range(nblk):
                    dref[0, r, :, blk * LANES:(blk + 1) * LANES] = (
                        stage_ref[ti * nblk + blk, pl.ds(r, rows, stride=dil), :].astype(BF16))

    qc_ref[...] = proj[:, _C_QC:_C_KC].astype(BF16)
    kc_ref[...] = proj[:, _C_KC:_C_VC].astype(BF16)
    vc_ref[...] = proj[:, _C_VC:_C_END].astype(BF16)


def _proj_call(x2, g, wbig, qn, wuq, kvn, wukv, tabs, seq, tm):
    t, d = x2.shape
    nseq = seq // tm
    row = lambda i: (i, 0)
    const = lambda i: (0, 0)
    widths = (QK_A_PAD, QK_A_PAD, WIDTH_A, WIDTH_B, WIDTH_B, WIDTH_B, WIDTH_C, WIDTH_C, WIDTH_C)
    out_specs = [pl.BlockSpec((tm, w), row) for w in widths]
    out_shape = [jax.ShapeDtypeStruct((t, w), BF16) for w in widths]
    for dil in _DILATIONS:
        assert tm % (16 * dil) == 0
        for _ in range(3):
            out_specs.append(pl.BlockSpec((1, dil, tm // dil, WIDTH_B), lambda i: (i // nseq, 0, i % nseq, 0)))
            out_shape.append(jax.ShapeDtypeStruct((t // seq, dil, seq // dil, WIDTH_B), BF16))
    return pl.pallas_call(
        _proj_kernel,
        grid=(t // tm,),
        in_specs=[
            pl.BlockSpec((tm, d), row),
            pl.BlockSpec((1, d), const),
            pl.BlockSpec(wbig.shape, const),
            pl.BlockSpec((1, Q_LORA), const),
            pl.BlockSpec(wuq.shape, const),
            pl.BlockSpec((1, KV_LORA), const),
            pl.BlockSpec(wukv.shape, const),
            pl.BlockSpec((tm, tabs.shape[1]), lambda i: (i % nseq, 0)),
        ],
        out_specs=out_specs,
        out_shape=out_shape,
        scratch_shapes=[pltpu.VMEM((3 * WIDTH_B // LANES, tm, LANES), F32)],
        compiler_params=pltpu.CompilerParams(dimension_semantics=("parallel",),
                                             vmem_limit_bytes=VMEM_LIMIT),
    )(x2, g, wbig, qn, wuq, kvn, wukv, tabs)


def _attn_a_kernel(qt_ref, k_ref, vt_ref, o_ref, a0_ref, a1_ref, b0_ref, b1_ref, *, tk):
    tq = qt_ref.shape[2]
    nk = k_ref.shape[1] // tk
    st_a, st_b = (a0_ref, a1_ref), (b0_ref, b1_ref)
    ones = jnp.ones((BF16_ROWS, tk), BF16)

    def scores(c, st_ref):
        ks = pl.multiple_of(c * tk, tk)
        for j in range(2):
            st_ref[j] = _dot(k_ref[0, pl.ds(ks, tk), j * LANES:(j + 1) * LANES],
                             qt_ref[0, j * LANES:(j + 1) * LANES, :])

    def accumulate(c, st_ref, carry):
        ks = pl.multiple_of(c * tk, tk)
        stats = []
        for j in range(2):
            m = carry[j][0]
            st = st_ref[j]
            m_new = jnp.maximum(m, jnp.max(st, axis=0, keepdims=True))
            stats.append((m_new, jnp.exp2(m - m_new), jnp.exp2(st - m_new).astype(BF16)))
        new = []
        for j in range(2):
            m_new, alpha, pt = stats[j]
            vt = jnp.concatenate([vt_ref[0, j * HEAD_DIM:(j + 1) * HEAD_DIM, pl.ds(ks, tk)], ones], axis=0)
            new.append((m_new, alpha * carry[j][1] + _dot(vt, pt)))
        return tuple(new)

    def half_step(c, src, dst, carry, issue=True):
        for u in range(2):
            if issue:
                scores(c + 2 + u, dst[u])
            carry = accumulate(c + u, src[u], carry)
        return carry

    def body(i, carry):
        carry = half_step(4 * i, st_a, st_b, carry)
        return half_step(4 * i + 2, st_b, st_a, carry)

    init = tuple((jnp.full((1, tq), NEG_INF, F32), jnp.zeros((HEAD_DIM + BF16_ROWS, tq), F32)) for _ in range(2))
    scores(0, st_a[0])
    scores(1, st_a[1])
    carry = lax.fori_loop(0, nk // 4 - 1, body, init)
    carry = half_step(nk - 4, st_a, st_b, carry)
    res = half_step(nk - 2, st_b, st_a, carry, issue=False)
    for j in range(2):
        acc = res[j][1]
        o_ref[0, j * HEAD_DIM:(j + 1) * HEAD_DIM, :] = (
            acc[:HEAD_DIM] / acc[HEAD_DIM:HEAD_DIM + 1]).astype(o_ref.dtype)


def _attn_a_call(qa, ka, va, tq, tk):
    b, s, _ = qa.shape
    pairs = HEADS_A // 2
    assert s % (4 * tk) == 0
    qt = jnp.swapaxes(qa, 1, 2)
    vt = jnp.swapaxes(va, 1, 2)
    ot = pl.pallas_call(
        functools.partial(_attn_a_kernel, tk=tk),
        grid=(b, pairs, s // tq),
        in_specs=[
            pl.BlockSpec((1, 2 * LANES, tq), lambda bi, p, qi: (bi, p, qi)),
            pl.BlockSpec((1, s, 2 * LANES), lambda bi, p, qi: (bi, 0, p)),
            pl.BlockSpec((1, LANES, s), lambda bi, p, qi: (bi, p, 0)),
        ],
        out_specs=pl.BlockSpec((1, LANES, tq), lambda bi, p, qi: (bi, p, qi)),
        out_shape=jax.ShapeDtypeStruct((b, WIDTH_A, s), BF16),
        scratch_shapes=[pltpu.VMEM((2, tk, tq), F32) for _ in range(4)],
        compiler_params=pltpu.CompilerParams(dimension_semantics=("parallel", "parallel", "arbitrary"),
                                             vmem_limit_bytes=VMEM_LIMIT),
    )(qt, ka, vt)
    return jnp.swapaxes(ot, 1, 2)


def _attn_b_kernel(q_ref, k_ref, v_ref, o_ref, lse_ref, *, tq, half):
    n = q_ref.shape[0]
    kw = min(tq + 2 * half, n)
    first = _lane_is_first_head((tq, LANES))

    group = min(BLOCKS_PER_STEP, n // tq)
    diff = (lax.broadcasted_iota(jnp.int32, (tq, kw), 0) - lax.broadcasted_iota(jnp.int32, (tq, kw), 1))

    def body(it, carry):
        blocks = []
        for u in range(group):
            q0 = pl.multiple_of((it * group + u) * tq, tq)
            ks = pl.multiple_of(jnp.clip(q0 - half, 0, n - kw), half)
            blocks.append((q0, ks, q_ref[pl.ds(q0, tq), :], k_ref[pl.ds(ks, kw), :]))
        scores = [[_dot_nt(_keep_head(q, first, j), k) for j in range(2)] for (_, _, q, k) in blocks]
        probs = []
        for (q0, ks, _, _), s2 in zip(blocks, scores):
            mask = jnp.abs(diff + (q0 - ks)) <= half
            row = []
            for j in range(2):
                s = jnp.where(mask, s2[j], NEG_INF)
                m = jnp.max(s, axis=-1, keepdims=True)
                p = jnp.exp(s - m)
                den = jnp.sum(p, axis=-1, keepdims=True)
                row.append((p.astype(BF16), den, m + jnp.log(den)))
            probs.append(row)
        for (q0, ks, _, _), row in zip(blocks, probs):
            v = v_ref[pl.ds(ks, kw), :]
            outs = [_dot(p, v) / den for (p, den, _) in row]
            lses = [jnp.broadcast_to(lse, (tq, LANES)) for (_, _, lse) in row]
            o_ref[pl.ds(q0, tq), :] = jnp.where(first, outs[0], outs[1]).astype(o_ref.dtype)
            lse_ref[pl.ds(q0, tq), :] = jnp.where(first, lses[0], lses[1])
        return carry

    lax.fori_loop(0, n // (tq * group), body, 0)


def _attn_b_call(q, k, v, half, tq):
    b, dil, n, w = q.shape
    pairs = w // LANES
    spec = pl.BlockSpec((None, None, n, LANES), lambda bi, r, p: (bi, r, 0, p))
    return pl.pallas_call(
        functools.partial(_attn_b_kernel, tq=min(tq, n), half=half),
        grid=(b, dil, pairs),
        in_specs=[spec, spec, spec],
        out_specs=[spec, spec],
        out_shape=[jax.ShapeDtypeStruct(q.shape, BF16), jax.ShapeDtypeStruct(q.shape, F32)],
        compiler_params=pltpu.CompilerParams(dimension_semantics=("parallel", "parallel", "parallel"),
                                             vmem_limit_bytes=VMEM_LIMIT),
    )(q, k, v)


def _na_bias_kernel(rpb_ref, o_ref):
    base = pl.program_id(0) * RPB_PER_HEAD
    shape = (GRID_W, LANES)
    lane = lax.broadcasted_iota(jnp.int32, shape, 1)
    p = lax.broadcasted_iota(jnp.int32, shape, 0)
    c = lane % GRID_W
    upper = lane >= GRID_W
    c_start = jnp.clip(p - NA_COLS // 2, 0, GRID_W - NA_COLS)
    col_ok = (c >= c_start) & (c < c_start + NA_COLS)
    dc = c - p + (NA_COLS - 1)
    n_dc = 2 * NA_COLS - 1
    for v in range(NA_ROWS):
        for m in range(NA_ROWS * GRID_W // LANES):
            a_lo = 2 * m - v + (NA_ROWS - 1)
            acc = jnp.full(shape, NEG_INF, F32)
            for b in range(n_dc):
                val = jnp.where(upper, rpb_ref[base + (a_lo + 1) * n_dc + b], rpb_ref[base + a_lo * n_dc + b])
                acc = jnp.where(dc == b, val, acc)
            o_ref[0, v, :, m * LANES:(m + 1) * LANES] = jnp.where(col_ok, acc, NEG_INF)


def _na_bias_call(rpb):
    nh = rpb.shape[0] * rpb.shape[1]
    return pl.pallas_call(
        _na_bias_kernel,
        grid=(nh,),
        in_specs=[pl.BlockSpec(memory_space=pltpu.SMEM)],
        out_specs=pl.BlockSpec((1, NA_ROWS, GRID_W, NA_ROWS * GRID_W), lambda g: (g, 0, 0, 0)),
        out_shape=jax.ShapeDtypeStruct((nh, NA_ROWS, GRID_W, NA_ROWS * GRID_W), F32),
    )(rpb.reshape(-1))


def _attn_c_kernel(q_ref, k_ref, v_ref, bias_ref, o_ref):
    rows = q_ref.shape[1] // GRID_W
    win = NA_ROWS * GRID_W
    first = _lane_is_first_head((GRID_W, LANES))

    group = BLOCKS_PER_STEP
    assert rows % group == 0

    def body(it, carry):
        blocks = []
        for u in range(group):
            r = it * group + u
            r_start = jnp.clip(r - NA_ROWS // 2, 0, rows - NA_ROWS)
            q0 = pl.multiple_of(r * GRID_W, GRID_W)
            ks = pl.multiple_of(r_start * GRID_W, GRID_W)
            blocks.append((q0, ks, r - r_start, q_ref[0, pl.ds(q0, GRID_W), :], k_ref[0, pl.ds(ks, win), :]))
        scores = [[_dot_nt(_keep_head(q, first, j), k) for j in range(2)] for (_, _, _, q, k) in blocks]
        probs = []
        for (_, _, variant, _, _), s2 in zip(blocks, scores):
            row = []
            for j in range(2):
                s = s2[j] + bias_ref[j, variant]
                m = jnp.max(s, axis=-1, keepdims=True)
                p = jnp.exp(s - m)
                row.append((p.astype(BF16), jnp.sum(p, axis=-1, keepdims=True)))
            probs.append(row)
        for (q0, ks, _, _, _), row in zip(blocks, probs):
            v = v_ref[0, pl.ds(ks, win), :]
            outs = [_dot(p, v) / den for (p, den) in row]
            o_ref[0, pl.ds(q0, GRID_W), :] = jnp.where(first, outs[0], outs[1]).astype(o_ref.dtype)
        return carry

    lax.fori_loop(0, rows // group, body, 0)


def _attn_c_call(q, k, v, bias):
    b, s, w = q.shape
    pairs = w // LANES
    spec = pl.BlockSpec((1, s, LANES), lambda bi, p: (bi, 0, p))
    return pl.pallas_call(
        _attn_c_kernel,
        grid=(b, pairs),
        in_specs=[spec, spec, spec,
                  pl.BlockSpec((2,) + bias.shape[1:], lambda bi, p: (p, 0, 0, 0))],
        out_specs=spec,
        out_shape=jax.ShapeDtypeStruct((b, s, w), BF16),
        compiler_params=pltpu.CompilerParams(dimension_semantics=("parallel", "parallel"),
                                             vmem_limit_bytes=VMEM_LIMIT),
    )(q, k, v, bias)


def _out_kernel(x_ref, oa_ref, ob1_ref, l1_ref, *rest):
    nd = len(_DILATIONS)
    dil_refs = rest[:2 * nd]
    oc_ref, ga_ref, gb_ref, gc_ref, w_ref, o_ref, stage_ref = rest[2 * nd:]
    tm = x_ref.shape[0]
    nblk = WIDTH_B // LANES
    na = _rms(oa_ref[...].astype(F32), ga_ref[...]).astype(BF16)
    outs, lses = [ob1_ref[...].astype(F32)], [l1_ref[...]]
    for di, dil in enumerate(_DILATIONS):
        rows = tm // dil
        for which, acc in ((0, outs), (1, lses)):
            src = dil_refs[2 * di + which]
            slot = (2 * di + which) * nblk
            for r in range(dil):
                for blk in range(nblk):
                    stage_ref[slot + blk, pl.ds(r, rows, stride=dil), :] = (
                        src[0, r, :, blk * LANES:(blk + 1) * LANES].astype(F32))
            acc.append(jnp.concatenate([stage_ref[slot + blk] for blk in range(nblk)], axis=-1))
    lmax = functools.reduce(jnp.maximum, lses)
    es = [jnp.exp(l - lmax) for l in lses]
    ob = sum(e * o for e, o in zip(es, outs)) / sum(es)
    nb = _rms(ob, gb_ref[...]).astype(BF16)
    nc = _rms(oc_ref[...].astype(F32), gc_ref[...]).astype(BF16)
    acc = _dot(na, w_ref[0:WIDTH_A, :])
    acc += _dot(nb, w_ref[WIDTH_A:WIDTH_A + WIDTH_B, :])
    acc += _dot(nc, w_ref[WIDTH_A + WIDTH_B:, :])
    o_ref[...] = x_ref[...] + acc


def _out_call(x2, oa, ob1, l1, dilated, oc, ga, gb, gc, w, seq, tm):
    t, d = x2.shape
    nseq = seq // tm
    row = lambda i: (i, 0)
    const = lambda i: (0, 0)
    rs = lambda width: pl.BlockSpec((tm, width), row)
    cs = lambda width: pl.BlockSpec((1, width), const)
    dil_specs, dil_args = [], []
    for dil, pair in zip(_DILATIONS, dilated):
        for a in pair:
            dil_specs.append(pl.BlockSpec((1, dil, tm // dil, WIDTH_B), lambda i: (i // nseq, 0, i % nseq, 0)))
            dil_args.append(a)
    return pl.pallas_call(
        _out_kernel,
        grid=(t // tm,),
        in_specs=[rs(d), rs(WIDTH_A), rs(WIDTH_B), rs(WIDTH_B), *dil_specs,
                  rs(WIDTH_C), cs(WIDTH_A), cs(WIDTH_B), cs(WIDTH_C), pl.BlockSpec(w.shape, const)],
        out_specs=rs(d),
        out_shape=jax.ShapeDtypeStruct((t, d), F32),
        scratch_shapes=[pltpu.VMEM((2 * len(_DILATIONS) * WIDTH_B // LANES, tm, LANES), F32)],
        compiler_params=pltpu.CompilerParams(dimension_semantics=("parallel",),
                                             vmem_limit_bytes=VMEM_LIMIT),
    )(x2, oa, ob1, l1, *dil_args, oc, ga, gb, gc, w)


def _mlp_kernel(x_ref, g_ref, w1_ref, w2_ref, gf_ref, o_ref, h_ref, acc_ref, *, final_norm):
    f = pl.program_id(1)

    @pl.when(f == 0)
    def _():
        h_ref[...] = _rms(x_ref[...], g_ref[...]).astype(BF16)
        acc_ref[...] = jnp.zeros_like(acc_ref)

    u = jnp.maximum(_dot(h_ref[...], w1_ref[...]), 0.0)
    acc_ref[...] += _dot((u * u).astype(BF16), w2_ref[...])

    @pl.when(f == pl.num_programs(1) - 1)
    def _():
        y = x_ref[...] + acc_ref[...]
        if final_norm:
            y = _rms(y, gf_ref[...])
        o_ref[...] = y


def _mlp_call(x2, g, w1, w2, gf, final_norm, tm, tf):
    t, d = x2.shape
    dff = w1.shape[1]
    return pl.pallas_call(
        functools.partial(_mlp_kernel, final_norm=final_norm),
        grid=(t // tm, dff // tf),
        in_specs=[
            pl.BlockSpec((tm, d), lambda i, f: (i, 0)),
            pl.BlockSpec((1, d), lambda i, f: (0, 0)),
            pl.BlockSpec((d, tf), lambda i, f: (0, f)),
            pl.BlockSpec((tf, d), lambda i, f: (f, 0)),
            pl.BlockSpec((1, d), lambda i, f: (0, 0)),
        ],
        out_specs=pl.BlockSpec((tm, d), lambda i, f: (i, 0)),
        out_shape=jax.ShapeDtypeStruct((t, d), F32),
        scratch_shapes=[pltpu.VMEM((tm, d), BF16), pltpu.VMEM((tm, d), F32)],
        compiler_params=pltpu.CompilerParams(dimension_semantics=("parallel", "arbitrary"),
                                             vmem_limit_bytes=VMEM_LIMIT),
    )(x2, g, w1, w2, gf)


def _rotate_half_cols(w, half):
    return jnp.concatenate([-w[..., half:], w[..., :half]], axis=-1)


def _rope_tables(seq):
    pos = jnp.arange(seq, dtype=F32)

    def cos_sin(half):
        inv_freq = ROPE_THETA ** (-jnp.arange(half, dtype=F32) / half)
        ang = pos[:, None] * inv_freq[None, :]
        return jnp.cos(ang), jnp.sin(ang)

    ca, sa = cos_sin(QK_ROPE // 2)
    ca2, sa2 = jnp.concatenate([ca, ca], -1), jnp.concatenate([sa, sa], -1)
    ones = jnp.ones((seq, QK_NOPE), F32)
    zeros = jnp.zeros((seq, QK_NOPE), F32)
    tail = jnp.zeros((seq, LANES - QK_NOPE - QK_ROPE), F32)
    scale_a = (QK_NOPE + QK_ROPE) ** -0.5 * LOG2_E
    cosq = jnp.concatenate([ones, ca2, tail], -1) * scale_a
    sinq = jnp.concatenate([zeros, sa2, tail], -1) * scale_a
    cosk = jnp.concatenate([zeros, ca2, tail], -1)
    sink = jnp.concatenate([zeros, sa2, tail], -1)
    cb, sb = cos_sin(HEAD_DIM // 2)
    cosb = jnp.concatenate([cb, cb, cb, cb], -1)
    sinb = jnp.concatenate([-sb, sb, -sb, sb], -1)
    return jnp.concatenate([cosq, sinq, cosk, sink, cosb, sinb], -1)


def _layer_weights(w_in, w_uq, w_ukv):
    d = w_in.shape[0]
    scale = HEAD_DIM ** -0.5
    c_b = Q_LORA + KV_LORA + QK_ROPE
    c_c = c_b + 3 * WIDTH_B
    w_kpe = w_in[:, Q_LORA + KV_LORA:c_b]

    def place(w):
        return jnp.concatenate([jnp.zeros((d, QK_NOPE), F32), w, jnp.zeros((d, LANES - QK_NOPE - QK_ROPE), F32)], -1)

    wbig = jnp.concatenate([
        w_in[:, :Q_LORA + KV_LORA],
        place(w_kpe), place(_rotate_half_cols(w_kpe, QK_ROPE // 2)),
        w_in[:, c_b:c_b + WIDTH_B] * scale, w_in[:, c_b + WIDTH_B:c_c],
        w_in[:, c_c:c_c + WIDTH_C] * scale, w_in[:, c_c + WIDTH_C:],
    ], -1).astype(BF16)

    uq = w_uq.reshape(Q_LORA, HEADS_A, QK_NOPE + QK_ROPE)
    pad = jnp.zeros((Q_LORA, HEADS_A, LANES - QK_NOPE - QK_ROPE), F32)
    uq_pad = jnp.concatenate([uq, pad], -1)
    uq_rot = jnp.concatenate([jnp.zeros((Q_LORA, HEADS_A, QK_NOPE), F32),
                              _rotate_half_cols(uq[..., QK_NOPE:], QK_ROPE // 2), pad], -1)
    wuq = jnp.concatenate([uq_pad.reshape(Q_LORA, QK_A_PAD), uq_rot.reshape(Q_LORA, QK_A_PAD)], -1).astype(BF16)

    ukv = w_ukv.reshape(KV_LORA, HEADS_A, QK_NOPE + HEAD_DIM)
    uk_pad = jnp.concatenate([ukv[..., :QK_NOPE], jnp.zeros((KV_LORA, HEADS_A, LANES - QK_NOPE), F32)], -1)
    wukv = jnp.concatenate([uk_pad.reshape(KV_LORA, QK_A_PAD), ukv[..., QK_NOPE:].reshape(KV_LORA, WIDTH_A)],
                           -1).astype(BF16)
    return wbig, wuq, wukv


def kernel(x, g_mix, w_in, q_norm, w_uq, kv_norm, w_ukv, rpb, out_norm_a, out_norm_b, out_norm_c, w_out, g_mlp,
           w_mlp_in, w_mlp_out, g_final):
    b, s, d = x.shape
    depth = w_in.shape[0]
    t = b * s
    assert s % GRID_W == 0 and s // GRID_W >= NA_ROWS
    tm_proj = min(512, s)
    tm_out = min(512, s)
    tm_mlp = min(1024, t)
    tq_a = min(256, s)
    tk_a = min(512, s)

    tabs = _rope_tables(s)
    na_bias = _na_bias_call(rpb).reshape(depth, HEADS_C, NA_ROWS, GRID_W, NA_ROWS * GRID_W)
    row = lambda a: a.reshape(1, -1)

    x2 = x.reshape(t, d)
    for l in range(depth):
        wbig, wuq, wukv = _layer_weights(w_in[l], w_uq[l], w_ukv[l])
        qa, ka, va, qb, kb, vb, qc, kc, vc, *dil_qkv = _proj_call(
            x2, row(g_mix[l]), wbig, row(q_norm[l]), wuq, row(kv_norm[l]), wukv, tabs, s, tm_proj)
        seq3 = lambda a: a.reshape(b, s, a.shape[-1])
        oa = _attn_a_call(seq3(qa), seq3(ka), seq3(va), tq_a, tk_a).reshape(t, WIDTH_A)
        ob1, l1, dilated = None, None, []
        for window, dil in DILATED_PAIRS:
            half = window // (2 * dil)
            if dil == 1:
                unit = lambda a: a.reshape(b, 1, s, WIDTH_B)
                o_i, lse_i = _attn_b_call(unit(qb), unit(kb), unit(vb), half, 128)
                ob1, l1 = o_i.reshape(t, WIDTH_B), lse_i.reshape(t, WIDTH_B)
            else:
                di = 3 * _DILATIONS.index(dil)
                dilated.append(_attn_b_call(*dil_qkv[di:di + 3], half, 128))
        oc = _attn_c_call(seq3(qc), seq3(kc), seq3(vc), na_bias[l]).reshape(t, WIDTH_C)
        x2 = _out_call(x2, oa, ob1, l1, dilated, oc, row(out_norm_a[l]), row(out_norm_b[l]), row(out_norm_c[l]),
                       w_out[l].astype(BF16), s, tm_out)
        x2 = _mlp_call(x2, row(g_mlp[l]), w_mlp_in[l].astype(BF16), w_mlp_out[l].astype(BF16), row(g_final),
                       l == depth - 1, tm_mlp, min(1024, w_mlp_in.shape[2]))
    return x2.reshape(b, s, d)
```

```python
import functools

import jax
import jax.numpy as jnp
from jax import lax
from jax.experimental import pallas as pl
from jax.experimental.pallas import tpu as pltpu

HEAD_DIM = 64
LANES = 128
BF16_ROWS = 16
HEADS_A = 6
HEADS_B = 6
HEADS_C = 4
Q_LORA = 256
KV_LORA = 128
QK_NOPE = 64
QK_ROPE = 32
DILATED_PAIRS = ((128, 1), (512, 4), (2048, 16))
_DILATIONS = tuple(dil for _, dil in DILATED_PAIRS if dil > 1)
GRID_W = 64
NA_ROWS = 8
NA_COLS = 16
ROPE_THETA = 10000.0
NORM_EPS = 1e-6
NEG_INF = -1e30
LOG2_E = 1.4426950408889634
BLOCKS_PER_STEP = 4

WIDTH_A = HEADS_A * HEAD_DIM
WIDTH_B = HEADS_B * HEAD_DIM
WIDTH_C = HEADS_C * HEAD_DIM
QK_A_PAD = HEADS_A * LANES
RPB_PER_HEAD = (2 * NA_ROWS - 1) * (2 * NA_COLS - 1)

VMEM_LIMIT = 56 * 1024 * 1024

BF16 = jnp.bfloat16
F32 = jnp.float32


def _rms(x, g):
    return x * lax.rsqrt(jnp.mean(x * x, axis=-1, keepdims=True) + NORM_EPS) * g


def _dot(a, b):
    return jnp.dot(a, b, preferred_element_type=F32)


def _dot_nt(a, b):
    return lax.dot_general(a, b, (((1,), (1,)), ((), ())), preferred_element_type=F32)


def _lane_is_first_head(shape):
    return lax.broadcasted_iota(jnp.int32, shape, len(shape) - 1) < HEAD_DIM


def _keep_head(q, first, j):
    zero = jnp.zeros_like(q)
    return jnp.where(first, q, zero) if j == 0 else jnp.where(first, zero, q)


_C_CQ = 0
_C_CKV = _C_CQ + Q_LORA
_C_KPE = _C_CKV + KV_LORA
_C_KPR = _C_KPE + LANES
_C_QB = _C_KPR + LANES
_C_KB = _C_QB + WIDTH_B
_C_VB = _C_KB + WIDTH_B
_C_QC = _C_VB + WIDTH_B
_C_KC = _C_QC + WIDTH_C
_C_VC = _C_KC + WIDTH_C
_C_END = _C_VC + WIDTH_C

_T_COSQ, _T_SINQ, _T_COSK, _T_SINK, _T_COSB, _T_SINB = range(6)


def _proj_kernel(x_ref, g_ref, wbig_ref, qn_ref, wuq_ref, kvn_ref, wukv_ref, tab_ref,
                 qa_ref, ka_ref, va_ref, qb_ref, kb_ref, vb_ref, qc_ref, kc_ref, vc_ref, *rest):
    dil_refs, stage_ref = rest[:-1], rest[-1]
    tm = x_ref.shape[0]

    def tab(i):
        return tab_ref[:, i * LANES:(i + 1) * LANES]

    h = _rms(x_ref[...], g_ref[...]).astype(BF16)
    proj = _dot(h, wbig_ref[...])

    cqn = _rms(proj[:, _C_CQ:_C_CKV], qn_ref[...]).astype(BF16)
    qa2 = _dot(cqn, wuq_ref[...])
    ckvn = _rms(proj[:, _C_CKV:_C_KPE], kvn_ref[...]).astype(BF16)
    kv2 = _dot(ckvn, wukv_ref[...])
    kpe = proj[:, _C_KPE:_C_KPR] * tab(_T_COSK) + proj[:, _C_KPR:_C_QB] * tab(_T_SINK)
    cosq, sinq = tab(_T_COSQ), tab(_T_SINQ)
    for hd in range(HEADS_A):
        sl = slice(hd * LANES, (hd + 1) * LANES)
        rot = slice(QK_A_PAD + hd * LANES, QK_A_PAD + (hd + 1) * LANES)
        qa_ref[:, sl] = (qa2[:, sl] * cosq + qa2[:, rot] * sinq).astype(BF16)
        ka_ref[:, sl] = (kv2[:, sl] + kpe).astype(BF16)
    va_ref[...] = kv2[:, QK_A_PAD:].astype(BF16)

    cosb, sinb = tab(_T_COSB), tab(_T_SINB)
    first_half = (lax.broadcasted_iota(jnp.int32, cosb.shape, 1) % HEAD_DIM) < HEAD_DIM // 2
    nblk = WIDTH_B // LANES
    for ti, (src, dst) in enumerate(((_C_QB, qb_ref), (_C_KB, kb_ref), (_C_VB, vb_ref))):
        for blk in range(nblk):
            xb = proj[:, src + blk * LANES:src + (blk + 1) * LANES]
            if dst is not vb_ref:
                swapped = jnp.where(first_half, pltpu.roll(xb, LANES - HEAD_DIM // 2, 1),
                                    pltpu.roll(xb, HEAD_DIM // 2, 1))
                xb = xb * cosb + swapped * sinb
            dst[:, blk * LANES:(blk + 1) * LANES] = xb.astype(BF16)
            stage_ref[ti * nblk + blk] = xb
    for di, dil in enumerate(_DILATIONS):
        rows = tm // dil
        for ti in range(3):
            dref = dil_refs[di * 3 + ti]
            for r in range(dil):
                for blk in range(nblk):
                    dref[0, r, :, blk * LANES:(blk + 1) * LANES] = (
                        stage_ref[ti * nblk + blk, pl.ds(r, rows, stride=dil), :].astype(BF16))

    qc_ref[...] = proj[:, _C_QC:_C_KC].astype(BF16)
    kc_ref[...] = proj[:, _C_KC:_C_VC].astype(BF16)
    vc_ref[...] = proj[:, _C_VC:_C_END].astype(BF16)


def _proj_call(x2, g, wbig, qn, wuq, kvn, wukv, tabs, seq, tm):
    t, d = x2.shape
    nseq = seq // tm
    row = lambda i: (i, 0)
    const = lambda i: (0, 0)
    widths = (QK_A_PAD, QK_A_PAD, WIDTH_A, WIDTH_B, WIDTH_B, WIDTH_B, WIDTH_C, WIDTH_C, WIDTH_C)
    out_specs = [pl.BlockSpec((tm, w), row) for w in widths]
    out_shape = [jax.ShapeDtypeStruct((t, w), BF16) for w in widths]
    for dil in _DILATIONS:
        assert tm % (16 * dil) == 0
        for _ in range(3):
            out_specs.append(pl.BlockSpec((1, dil, tm // dil, WIDTH_B), lambda i: (i // nseq, 0, i % nseq, 0)))
            out_shape.append(jax.ShapeDtypeStruct((t // seq, dil, seq // dil, WIDTH_B), BF16))
    return pl.pallas_call(
        _proj_kernel,
        grid=(t // tm,),
        in_specs=[
            pl.BlockSpec((tm, d), row),
            pl.BlockSpec((1, d), const),
            pl.BlockSpec(wbig.shape, const),
            pl.BlockSpec((1, Q_LORA), const),
            pl.BlockSpec(wuq.shape, const),
            pl.BlockSpec((1, KV_LORA), const),
            pl.BlockSpec(wukv.shape, const),
            pl.BlockSpec((tm, tabs.shape[1]), lambda i: (i % nseq, 0)),
        ],
        out_specs=out_specs,
        out_shape=out_shape,
        scratch_shapes=[pltpu.VMEM((3 * WIDTH_B // LANES, tm, LANES), F32)],
        compiler_params=pltpu.CompilerParams(dimension_semantics=("parallel",),
                                             vmem_limit_bytes=VMEM_LIMIT),
    )(x2, g, wbig, qn, wuq, kvn, wukv, tabs)


def _attn_a_kernel(qt_ref, k_ref, vt_ref, o_ref, a0_ref, a1_ref, b0_ref, b1_ref, *, tq, tk):
    s = k_ref.shape[1]
    nk, nq = s // tk, s // tq
    st_a, st_b = (a0_ref, a1_ref), (b0_ref, b1_ref)
    ones = jnp.ones((BF16_ROWS, tk), BF16)

    def scores(qb, c, st_ref):
        q0 = pl.multiple_of(qb * tq, tq)
        ks = pl.multiple_of(c * tk, tk)
        for j in range(2):
            st_ref[j] = _dot(k_ref[0, pl.ds(ks, tk), j * LANES:(j + 1) * LANES],
                             qt_ref[0, j * LANES:(j + 1) * LANES, pl.ds(q0, tq)])

    def accumulate(c, st_ref, carry):
        ks = pl.multiple_of(c * tk, tk)
        stats = []
        for j in range(2):
            m = carry[j][0]
            st = st_ref[j]
            m_new = jnp.maximum(m, jnp.max(st, axis=0, keepdims=True))
            stats.append((m_new, jnp.exp2(m - m_new), jnp.exp2(st - m_new).astype(BF16)))
        new = []
        for j in range(2):
            m_new, alpha, pt = stats[j]
            vt = jnp.concatenate([vt_ref[0, j * HEAD_DIM:(j + 1) * HEAD_DIM, pl.ds(ks, tk)], ones], axis=0)
            new.append((m_new, alpha * carry[j][1] + _dot(vt, pt)))
        return tuple(new)

    def half_step(qb, c, src, dst, carry, wrap=False):
        for u in range(2):
            if wrap:
                scores(jnp.minimum(qb + 1, nq - 1), u, dst[u])
            else:
                scores(qb, c + 2 + u, dst[u])
            carry = accumulate(c + u, src[u], carry)
        return carry

    def q_block(qb, _):
        def body(i, carry):
            carry = half_step(qb, 4 * i, st_a, st_b, carry)
            return half_step(qb, 4 * i + 2, st_b, st_a, carry)

        init = tuple((jnp.full((1, tq), NEG_INF, F32), jnp.zeros((HEAD_DIM + BF16_ROWS, tq), F32))
                     for _ in range(2))
        carry = lax.fori_loop(0, nk // 4 - 1, body, init)
        carry = half_step(qb, nk - 4, st_a, st_b, carry)
        res = half_step(qb, nk - 2, st_b, st_a, carry, wrap=True)
        q0 = pl.multiple_of(qb * tq, tq)
        for j in range(2):
            acc = res[j][1]
            o_ref[0, j * HEAD_DIM:(j + 1) * HEAD_DIM, pl.ds(q0, tq)] = (
                acc[:HEAD_DIM] / acc[HEAD_DIM:HEAD_DIM + 1]).astype(o_ref.dtype)
        return 0

    scores(0, 0, st_a[0])
    scores(0, 1, st_a[1])
    lax.fori_loop(0, nq, q_block, 0)


def _attn_a_call(qa, ka, va, tq, tk):
    b, s, _ = qa.shape
    pairs = HEADS_A // 2
    assert s % (4 * tk) == 0
    qt = jnp.swapaxes(qa, 1, 2)
    vt = jnp.swapaxes(va, 1, 2)
    ot = pl.pallas_call(
        functools.partial(_attn_a_kernel, tq=tq, tk=tk),
        grid=(b, pairs),
        in_specs=[
            pl.BlockSpec((1, 2 * LANES, s), lambda bi, p: (bi, p, 0)),
            pl.BlockSpec((1, s, 2 * LANES), lambda bi, p: (bi, 0, p)),
            pl.BlockSpec((1, LANES, s), lambda bi, p: (bi, p, 0)),
        ],
        out_specs=pl.BlockSpec((1, LANES, s), lambda bi, p: (bi, p, 0)),
        out_shape=jax.ShapeDtypeStruct((b, WIDTH_A, s), BF16),
        scratch_shapes=[pltpu.VMEM((2, tk, tq), F32) for _ in range(4)],
        compiler_params=pltpu.CompilerParams(dimension_semantics=("parallel", "parallel"),
                                             vmem_limit_bytes=VMEM_LIMIT),
    )(qt, ka, vt)
    return jnp.swapaxes(ot, 1, 2)


def _attn_b_kernel(q_ref, k_ref, v_ref, o_ref, lse_ref, *, tq, half, n):
    total = q_ref.shape[0]
    kw = min(tq + 2 * half, n)
    first = _lane_is_first_head((tq, LANES))

    group = min(BLOCKS_PER_STEP, total // tq)
    diff = (lax.broadcasted_iota(jnp.int32, (tq, kw), 0) - lax.broadcasted_iota(jnp.int32, (tq, kw), 1))

    def body(it, carry):
        blocks = []
        for u in range(group):
            q0 = pl.multiple_of((it * group + u) * tq, tq)
            seg0 = (q0 // n) * n
            ks = pl.multiple_of(jnp.clip(q0 - half, seg0, seg0 + n - kw), half)
            blocks.append((q0, ks, q_ref[pl.ds(q0, tq), :], k_ref[pl.ds(ks, kw), :]))
        scores = [[_dot_nt(_keep_head(q, first, j), k) for j in range(2)] for (_, _, q, k) in blocks]
        probs = []
        for (q0, ks, _, _), s2 in zip(blocks, scores):
            mask = jnp.abs(diff + (q0 - ks)) <= half
            row = []
            for j in range(2):
                s = jnp.where(mask, s2[j], NEG_INF)
                m = jnp.max(s, axis=-1, keepdims=True)
                p = jnp.exp(s - m)
                den = jnp.sum(p, axis=-1, keepdims=True)
                row.append((p.astype(BF16), den, m + jnp.log(den)))
            probs.append(row)
        for (q0, ks, _, _), row in zip(blocks, probs):
            v = v_ref[pl.ds(ks, kw), :]
            outs = [_dot(p, v) / den for (p, den, _) in row]
            lses = [jnp.broadcast_to(lse, (tq, LANES)) for (_, _, lse) in row]
            o_ref[pl.ds(q0, tq), :] = jnp.where(first, outs[0], outs[1]).astype(o_ref.dtype)
            lse_ref[pl.ds(q0, tq), :] = jnp.where(first, lses[0], lses[1])
        return carry

    lax.fori_loop(0, total // (tq * group), body, 0)


def _attn_b_call(q, k, v, half, tq):
    b, dil, n, w = q.shape
    pairs = w // LANES
    tq = min(tq, n)
    assert n % tq == 0
    flat = lambda a: a.reshape(b, dil * n, w)
    spec = pl.BlockSpec((None, dil * n, LANES), lambda bi, p: (bi, 0, p))
    o, lse = pl.pallas_call(
        functools.partial(_attn_b_kernel, tq=tq, half=half, n=n),
        grid=(b, pairs),
        in_specs=[spec, spec, spec],
        out_specs=[spec, spec],
        out_shape=[jax.ShapeDtypeStruct((b, dil * n, w), BF16), jax.ShapeDtypeStruct((b, dil * n, w), F32)],
        compiler_params=pltpu.CompilerParams(dimension_semantics=("parallel", "parallel"),
                                             vmem_limit_bytes=VMEM_LIMIT),
    )(flat(q), flat(k), flat(v))
    return o.reshape(q.shape), lse.reshape(q.shape)


def _na_bias_kernel(rpb_ref, o_ref):
    base = pl.program_id(0) * RPB_PER_HEAD
    shape = (GRID_W, LANES)
    lane = lax.broadcasted_iota(jnp.int32, shape, 1)
    p = lax.broadcasted_iota(jnp.int32, shape, 0)
    c = lane % GRID_W
    upper = lane >= GRID_W
    c_start = jnp.clip(p - NA_COLS // 2, 0, GRID_W - NA_COLS)
    col_ok = (c >= c_start) & (c < c_start + NA_COLS)
    dc = c - p + (NA_COLS - 1)
    n_dc = 2 * NA_COLS - 1
    for v in range(NA_ROWS):
        for m in range(NA_ROWS * GRID_W // LANES):
            a_lo = 2 * m - v + (NA_ROWS - 1)
            acc = jnp.full(shape, NEG_INF, F32)
            for b in range(n_dc):
                val = jnp.where(upper, rpb_ref[base + (a_lo + 1) * n_dc + b], rpb_ref[base + a_lo * n_dc + b])
                acc = jnp.where(dc == b, val, acc)
            o_ref[0, v, :, m * LANES:(m + 1) * LANES] = jnp.where(col_ok, acc, NEG_INF)


def _na_bias_call(rpb):
    nh = rpb.shape[0] * rpb.shape[1]
    return pl.pallas_call(
        _na_bias_kernel,
        grid=(nh,),
        in_specs=[pl.BlockSpec(memory_space=pltpu.SMEM)],
        out_specs=pl.BlockSpec((1, NA_ROWS, GRID_W, NA_ROWS * GRID_W), lambda g: (g, 0, 0, 0)),
        out_shape=jax.ShapeDtypeStruct((nh, NA_ROWS, GRID_W, NA_ROWS * GRID_W), F32),
    )(rpb.reshape(-1))


def _attn_c_kernel(q_ref, k_ref, v_ref, bias_ref, o_ref):
    rows = q_ref.shape[1] // GRID_W
    win = NA_ROWS * GRID_W
    first = _lane_is_first_head((GRID_W, LANES))

    group = BLOCKS_PER_STEP
    assert rows % group == 0

    def body(it, carry):
        blocks = []
        for u in range(group):
            r = it * group + u
            r_start = jnp.clip(r - NA_ROWS // 2, 0, rows - NA_ROWS)
            q0 = pl.multiple_of(r * GRID_W, GRID_W)
            ks = pl.multiple_of(r_start * GRID_W, GRID_W)
            blocks.append((q0, ks, r - r_start, q_ref[0, pl.ds(q0, GRID_W), :], k_ref[0, pl.ds(ks, win), :]))
        scores = [[_dot_nt(_keep_head(q, first, j), k) for j in range(2)] for (_, _, _, q, k) in blocks]
        probs = []
        for (_, _, variant, _, _), s2 in zip(blocks, scores):
            row = []
            for j in range(2):
                s = s2[j] + bias_ref[j, variant]
                m = jnp.max(s, axis=-1, keepdims=True)
                p = jnp.exp(s - m)
                row.append((p.astype(BF16), jnp.sum(p, axis=-1, keepdims=True)))
            probs.append(row)
        for (q0, ks, _, _, _), row in zip(blocks, probs):
            v = v_ref[0, pl.ds(ks, win), :]
            outs = [_dot(p, v) / den for (p, den) in row]
            o_ref[0, pl.ds(q0, GRID_W), :] = jnp.where(first, outs[0], outs[1]).astype(o_ref.dtype)
        return carry

    lax.fori_loop(0, rows // group, body, 0)


def _attn_c_call(q, k, v, bias):
    b, s, w = q.shape
    pairs = w // LANES
    spec = pl.BlockSpec((1, s, LANES), lambda bi, p: (bi, 0, p))
    return pl.pallas_call(
        _attn_c_kernel,
        grid=(b, pairs),
        in_specs=[spec, spec, spec,
                  pl.BlockSpec((2,) + bias.shape[1:], lambda bi, p: (p, 0, 0, 0))],
        out_specs=spec,
        out_shape=jax.ShapeDtypeStruct((b, s, w), BF16),
        compiler_params=pltpu.CompilerParams(dimension_semantics=("parallel", "parallel"),
                                             vmem_limit_bytes=VMEM_LIMIT),
    )(q, k, v, bias)


def _out_kernel(x_ref, oa_ref, ob1_ref, l1_ref, *rest):
    nd = len(_DILATIONS)
    dil_refs = rest[:2 * nd]
    oc_ref, ga_ref, gb_ref, gc_ref, w_ref, o_ref, stage_ref = rest[2 * nd:]
    tm = x_ref.shape[0]
    nblk = WIDTH_B // LANES
    na = _rms(oa_ref[...].astype(F32), ga_ref[...]).astype(BF16)
    outs, lses = [ob1_ref[...].astype(F32)], [l1_ref[...]]
    for di, dil in enumerate(_DILATIONS):
        rows = tm // dil
        for which, acc in ((0, outs), (1, lses)):
            src = dil_refs[2 * di + which]
            slot = (2 * di + which) * nblk
            for r in range(dil):
                for blk in range(nblk):
                    stage_ref[slot + blk, pl.ds(r, rows, stride=dil), :] = (
                        src[0, r, :, blk * LANES:(blk + 1) * LANES].astype(F32))
            acc.append(jnp.concatenate([stage_ref[slot + blk] for blk in range(nblk)], axis=-1))
    lmax = functools.reduce(jnp.maximum, lses)
    es = [jnp.exp(l - lmax) for l in lses]
    ob = sum(e * o for e, o in zip(es, outs)) / sum(es)
    nb = _rms(ob, gb_ref[...]).astype(BF16)
    nc = _rms(oc_ref[...].astype(F32), gc_ref[...]).astype(BF16)
    acc = _dot(na, w_ref[0:WIDTH_A, :])
    acc += _dot(nb, w_ref[WIDTH_A:WIDTH_A + WIDTH_B, :])
    acc += _dot(nc, w_ref[WIDTH_A + WIDTH_B:, :])
    o_ref[...] = x_ref[...] + acc


def _out_call(x2, oa, ob1, l1, dilated, oc, ga, gb, gc, w, seq, tm):
    t, d = x2.shape
    nseq = seq // tm
    row = lambda i: (i, 0)
    const = lambda i: (0, 0)
    rs = lambda width: pl.BlockSpec((tm, width), row)
    cs = lambda width: pl.BlockSpec((1, width), const)
    dil_specs, dil_args = [], []
    for dil, pair in zip(_DILATIONS, dilated):
        for a in pair:
            dil_specs.append(pl.BlockSpec((1, dil, tm // dil, WIDTH_B), lambda i: (i // nseq, 0, i % nseq, 0)))
            dil_args.append(a)
    return pl.pallas_call(
        _out_kernel,
        grid=(t // tm,),
        in_specs=[rs(d), rs(WIDTH_A), rs(WIDTH_B), rs(WIDTH_B), *dil_specs,
                  rs(WIDTH_C), cs(WIDTH_A), cs(WIDTH_B), cs(WIDTH_C), pl.BlockSpec(w.shape, const)],
        out_specs=rs(d),
        out_shape=jax.ShapeDtypeStruct((t, d), F32),
        scratch_shapes=[pltpu.VMEM((2 * len(_DILATIONS) * WIDTH_B // LANES, tm, LANES), F32)],
        compiler_params=pltpu.CompilerParams(dimension_semantics=("parallel",),
                                             vmem_limit_bytes=VMEM_LIMIT),
    )(x2, oa, ob1, l1, *dil_args, oc, ga, gb, gc, w)


def _mlp_kernel(x_ref, g_ref, w1_ref, w2_ref, gf_ref, o_ref, h_ref, acc_ref, *, final_norm):
    f = pl.program_id(1)

    @pl.when(f == 0)
    def _():
        h_ref[...] = _rms(x_ref[...], g_ref[...]).astype(BF16)
        acc_ref[...] = jnp.zeros_like(acc_ref)

    u = jnp.maximum(_dot(h_ref[...], w1_ref[...]), 0.0)
    acc_ref[...] += _dot((u * u).astype(BF16), w2_ref[...])

    @pl.when(f == pl.num_programs(1) - 1)
    def _():
        y = x_ref[...] + acc_ref[...]
        if final_norm:
            y = _rms(y, gf_ref[...])
        o_ref[...] = y


def _mlp_call(x2, g, w1, w2, gf, final_norm, tm, tf):
    t, d = x2.shape
    dff = w1.shape[1]
    return pl.pallas_call(
        functools.partial(_mlp_kernel, final_norm=final_norm),
        grid=(t // tm, dff // tf),
        in_specs=[
            pl.BlockSpec((tm, d), lambda i, f: (i, 0)),
            pl.BlockSpec((1, d), lambda i, f: (0, 0)),
            pl.BlockSpec((d, tf), lambda i, f: (0, f)),
            pl.BlockSpec((tf, d), lambda i, f: (f, 0)),
            pl.BlockSpec((1, d), lambda i, f: (0, 0)),
        ],
        out_specs=pl.BlockSpec((tm, d), lambda i, f: (i, 0)),
        out_shape=jax.ShapeDtypeStruct((t, d), F32),
        scratch_shapes=[pltpu.VMEM((tm, d), BF16), pltpu.VMEM((tm, d), F32)],
        compiler_params=pltpu.CompilerParams(dimension_semantics=("parallel", "arbitrary"),
                                             vmem_limit_bytes=VMEM_LIMIT),
    )(x2, g, w1, w2, gf)


def _rotate_half_cols(w, half):
    return jnp.concatenate([-w[..., half:], w[..., :half]], axis=-1)


def _rope_tables(seq):
    pos = jnp.arange(seq, dtype=F32)

    def cos_sin(half):
        inv_freq = ROPE_THETA ** (-jnp.arange(half, dtype=F32) / half)
        ang = pos[:, None] * inv_freq[None, :]
        return jnp.cos(ang), jnp.sin(ang)

    ca, sa = cos_sin(QK_ROPE // 2)
    ca2, sa2 = jnp.concatenate([ca, ca], -1), jnp.concatenate([sa, sa], -1)
    ones = jnp.ones((seq, QK_NOPE), F32)
    zeros = jnp.zeros((seq, QK_NOPE), F32)
    tail = jnp.zeros((seq, LANES - QK_NOPE - QK_ROPE), F32)
    scale_a = (QK_NOPE + QK_ROPE) ** -0.5 * LOG2_E
    cosq = jnp.concatenate([ones, ca2, tail], -1) * scale_a
    sinq = jnp.concatenate([zeros, sa2, tail], -1) * scale_a
    cosk = jnp.concatenate([zeros, ca2, tail], -1)
    sink = jnp.concatenate([zeros, sa2, tail], -1)
    cb, sb = cos_sin(HEAD_DIM // 2)
    cosb = jnp.concatenate([cb, cb, cb, cb], -1)
    sinb = jnp.concatenate([-sb, sb, -sb, sb], -1)
    return jnp.concatenate([cosq, sinq, cosk, sink, cosb, sinb], -1)


def _layer_weights(w_in, w_uq, w_ukv):
    d = w_in.shape[0]
    scale = HEAD_DIM ** -0.5
    c_b = Q_LORA + KV_LORA + QK_ROPE
    c_c = c_b + 3 * WIDTH_B
    w_kpe = w_in[:, Q_LORA + KV_LORA:c_b]

    def place(w):
        return jnp.concatenate([jnp.zeros((d, QK_NOPE), F32), w, jnp.zeros((d, LANES - QK_NOPE - QK_ROPE), F32)], -1)

    wbig = jnp.concatenate([
        w_in[:, :Q_LORA + KV_LORA],
        place(w_kpe), place(_rotate_half_cols(w_kpe, QK_ROPE // 2)),
        w_in[:, c_b:c_b + WIDTH_B] * scale, w_in[:, c_b + WIDTH_B:c_c],
        w_in[:, c_c:c_c + WIDTH_C] * scale, w_in[:, c_c + WIDTH_C:],
    ], -1).astype(BF16)

    uq = w_uq.reshape(Q_LORA, HEADS_A, QK_NOPE + QK_ROPE)
    pad = jnp.zeros((Q_LORA, HEADS_A, LANES - QK_NOPE - QK_ROPE), F32)
    uq_pad = jnp.concatenate([uq, pad], -1)
    uq_rot = jnp.concatenate([jnp.zeros((Q_LORA, HEADS_A, QK_NOPE), F32),
                              _rotate_half_cols(uq[..., QK_NOPE:], QK_ROPE // 2), pad], -1)
    wuq = jnp.concatenate([uq_pad.reshape(Q_LORA, QK_A_PAD), uq_rot.reshape(Q_LORA, QK_A_PAD)], -1).astype(BF16)

    ukv = w_ukv.reshape(KV_LORA, HEADS_A, QK_NOPE + HEAD_DIM)
    uk_pad = jnp.concatenate([ukv[..., :QK_NOPE], jnp.zeros((KV_LORA, HEADS_A, LANES - QK_NOPE), F32)], -1)
    wukv = jnp.concatenate([uk_pad.reshape(KV_LORA, QK_A_PAD), ukv[..., QK_NOPE:].reshape(KV_LORA, WIDTH_A)],
                           -1).astype(BF16)
    return wbig, wuq, wukv


def kernel(x, g_mix, w_in, q_norm, w_uq, kv_norm, w_ukv, rpb, out_norm_a, out_norm_b, out_norm_c, w_out, g_mlp,
           w_mlp_in, w_mlp_out, g_final):
    b, s, d = x.shape
    depth = w_in.shape[0]
    t = b * s
    assert s % GRID_W == 0 and s // GRID_W >= NA_ROWS
    tm_proj = min(512, s)
    tm_out = min(512, s)
    tm_mlp = min(1024, t)
    tq_a = min(256, s)
    tk_a = min(512, s)

    tabs = _rope_tables(s)
    na_bias = _na_bias_call(rpb).reshape(depth, HEADS_C, NA_ROWS, GRID_W, NA_ROWS * GRID_W)
    row = lambda a: a.reshape(1, -1)

    x2 = x.reshape(t, d)
    for l in range(depth):
        wbig, wuq, wukv = _layer_weights(w_in[l], w_uq[l], w_ukv[l])
        qa, ka, va, qb, kb, vb, qc, kc, vc, *dil_qkv = _proj_call(
            x2, row(g_mix[l]), wbig, row(q_norm[l]), wuq, row(kv_norm[l]), wukv, tabs, s, tm_proj)
        seq3 = lambda a: a.reshape(b, s, a.shape[-1])
        oa = _attn_a_call(seq3(qa), seq3(ka), seq3(va), tq_a, tk_a).reshape(t, WIDTH_A)
        ob1, l1, dilated = None, None, []
        for window, dil in DILATED_PAIRS:
            half = window // (2 * dil)
            if dil == 1:
                unit = lambda a: a.reshape(b, 1, s, WIDTH_B)
                o_i, lse_i = _attn_b_call(unit(qb), unit(kb), unit(vb), half, 128)
                ob1, l1 = o_i.reshape(t, WIDTH_B), lse_i.reshape(t, WIDTH_B)
            else:
                di = 3 * _DILATIONS.index(dil)
                dilated.append(_attn_b_call(*dil_qkv[di:di + 3], half, 128))
        oc = _attn_c_call(seq3(qc), seq3(kc), seq3(vc), na_bias[l]).reshape(t, WIDTH_C)
        x2 = _out_call(x2, oa, ob1, l1, dilated, oc, row(out_norm_a[l]), row(out_norm_b[l]), row(out_norm_c[l]),
                       w_out[l].astype(BF16), s, tm_out)
        x2 = _mlp_call(x2, row(g_mlp[l]), w_mlp_in[l].astype(BF16), w_mlp_out[l].astype(BF16), row(g_final),
                       l == depth - 1, tm_mlp, min(1024, w_mlp_in.shape[2]))
    return x2.reshape(b, s, d)
```

```python
import functools

import jax
import jax.numpy as jnp
from jax import lax
from jax.experimental import pallas as pl
from jax.experimental.pallas import tpu as pltpu

HEAD_DIM = 64
LANES = 128
BF16_ROWS = 16
HEADS_A = 6
HEADS_B = 6
HEADS_C = 4
Q_LORA = 256
KV_LORA = 128
QK_NOPE = 64
QK_ROPE = 32
DILATED_PAIRS = ((128, 1), (512, 4), (2048, 16))
_DILATIONS = tuple(dil for _, dil in DILATED_PAIRS if dil > 1)
GRID_W = 64
NA_ROWS = 8
NA_COLS = 16
ROPE_THETA = 10000.0
NORM_EPS = 1e-6
NEG_INF = -1e30
LOG2_E = 1.4426950408889634
BLOCKS_PER_STEP = 4

WIDTH_A = HEADS_A * HEAD_DIM
WIDTH_B = HEADS_B * HEAD_DIM
WIDTH_C = HEADS_C * HEAD_DIM
QK_A_PAD = HEADS_A * LANES
RPB_PER_HEAD = (2 * NA_ROWS - 1) * (2 * NA_COLS - 1)

VMEM_LIMIT = 56 * 1024 * 1024

BF16 = jnp.bfloat16
F32 = jnp.float32


def _rms(x, g):
    return x * lax.rsqrt(jnp.mean(x * x, axis=-1, keepdims=True) + NORM_EPS) * g


def _dot(a, b):
    return jnp.dot(a, b, preferred_element_type=F32)


def _dot_nt(a, b):
    return lax.dot_general(a, b, (((1,), (1,)), ((), ())), preferred_element_type=F32)


def _lane_is_first_head(shape):
    return lax.broadcasted_iota(jnp.int32, shape, len(shape) - 1) < HEAD_DIM


def _keep_head(q, first, j):
    zero = jnp.zeros_like(q)
    return jnp.where(first, q, zero) if j == 0 else jnp.where(first, zero, q)


_C_CQ = 0
_C_CKV = _C_CQ + Q_LORA
_C_KPE = _C_CKV + KV_LORA
_C_KPR = _C_KPE + LANES
_C_QB = _C_KPR + LANES
_C_KB = _C_QB + WIDTH_B
_C_VB = _C_KB + WIDTH_B
_C_QC = _C_VB + WIDTH_B
_C_KC = _C_QC + WIDTH_C
_C_VC = _C_KC + WIDTH_C
_C_END = _C_VC + WIDTH_C

_T_COSQ, _T_SINQ, _T_COSK, _T_SINK, _T_COSB, _T_SINB = range(6)


def _proj_kernel(x_ref, g_ref, wbig_ref, qn_ref, wuq_ref, kvn_ref, wukv_ref, tab_ref,
                 qa_ref, ka_ref, va_ref, qb_ref, kb_ref, vb_ref, qc_ref, kc_ref, vc_ref, *rest):
    dil_refs, stage_ref = rest[:-1], rest[-1]
    tm = x_ref.shape[0]

    def tab(i):
        return tab_ref[:, i * LANES:(i + 1) * LANES]

    h = _rms(x_ref[...], g_ref[...]).astype(BF16)
    proj = _dot(h, wbig_ref[...])

    cqn = _rms(proj[:, _C_CQ:_C_CKV], qn_ref[...]).astype(BF16)
    qa2 = _dot(cqn, wuq_ref[...])
    ckvn = _rms(proj[:, _C_CKV:_C_KPE], kvn_ref[...]).astype(BF16)
    kv2 = _dot(ckvn, wukv_ref[...])
    kpe = proj[:, _C_KPE:_C_KPR] * tab(_T_COSK) + proj[:, _C_KPR:_C_QB] * tab(_T_SINK)
    cosq, sinq = tab(_T_COSQ), tab(_T_SINQ)
    for hd in range(HEADS_A):
        sl = slice(hd * LANES, (hd + 1) * LANES)
        rot = slice(QK_A_PAD + hd * LANES, QK_A_PAD + (hd + 1) * LANES)
        qa_ref[0, sl, :] = (qa2[:, sl] * cosq + qa2[:, rot] * sinq).T.astype(BF16)
        ka_ref[:, sl] = (kv2[:, sl] + kpe).astype(BF16)
    for blk in range(WIDTH_A // LANES):
        sl = slice(blk * LANES, (blk + 1) * LANES)
        va_ref[0, sl, :] = kv2[:, QK_A_PAD + blk * LANES:QK_A_PAD + (blk + 1) * LANES].T.astype(BF16)

    cosb, sinb = tab(_T_COSB), tab(_T_SINB)
    first_half = (lax.broadcasted_iota(jnp.int32, cosb.shape, 1) % HEAD_DIM) < HEAD_DIM // 2
    nblk = WIDTH_B // LANES
    for ti, (src, dst) in enumerate(((_C_QB, qb_ref), (_C_KB, kb_ref), (_C_VB, vb_ref))):
        for blk in range(nblk):
            xb = proj[:, src + blk * LANES:src + (blk + 1) * LANES]
            if dst is not vb_ref:
                swapped = jnp.where(first_half, pltpu.roll(xb, LANES - HEAD_DIM // 2, 1),
                                    pltpu.roll(xb, HEAD_DIM // 2, 1))
                xb = xb * cosb + swapped * sinb
                if dst is qb_ref:
                    xb = xb * LOG2_E
            dst[:, blk * LANES:(blk + 1) * LANES] = xb.astype(BF16)
            stage_ref[ti * nblk + blk] = xb
    for di, dil in enumerate(_DILATIONS):
        rows = tm // dil
        for ti in range(3):
            dref = dil_refs[di * 3 + ti]
            for r in range(dil):
                for blk in range(nblk):
                    dref[0, r, :, blk * LANES:(blk + 1) * LANES] = (
                        stage_ref[ti * nblk + blk, pl.ds(r, rows, stride=dil), :].astype(BF16))

    qc_ref[...] = proj[:, _C_QC:_C_KC].astype(BF16)
    kc_ref[...] = proj[:, _C_KC:_C_VC].astype(BF16)
    vc_ref[...] = proj[:, _C_VC:_C_END].astype(BF16)


def _proj_call(x2, g, wbig, qn, wuq, kvn, wukv, tabs, seq, tm):
    t, d = x2.shape
    nseq = seq // tm
    row = lambda i: (i, 0)
    const = lambda i: (0, 0)
    widths = (QK_A_PAD, QK_A_PAD, WIDTH_A, WIDTH_B, WIDTH_B, WIDTH_B, WIDTH_C, WIDTH_C, WIDTH_C)
    out_specs = [pl.BlockSpec((tm, w), row) for w in widths]
    out_shape = [jax.ShapeDtypeStruct((t, w), BF16) for w in widths]
    for idx in (0, 2):
        out_specs[idx] = pl.BlockSpec((1, widths[idx], tm), lambda i: (i // nseq, 0, i % nseq))
        out_shape[idx] = jax.ShapeDtypeStruct((t // seq, widths[idx], seq), BF16)
    for dil in _DILATIONS:
        assert tm % (16 * dil) == 0
        for _ in range(3):
            out_specs.append(pl.BlockSpec((1, dil, tm // dil, WIDTH_B), lambda i: (i // nseq, 0, i % nseq, 0)))
            out_shape.append(jax.ShapeDtypeStruct((t // seq, dil, seq // dil, WIDTH_B), BF16))
    return pl.pallas_call(
        _proj_kernel,
        grid=(t // tm,),
        in_specs=[
            pl.BlockSpec((tm, d), row),
            pl.BlockSpec((1, d), const),
            pl.BlockSpec(wbig.shape, const),
            pl.BlockSpec((1, Q_LORA), const),
            pl.BlockSpec(wuq.shape, const),
            pl.BlockSpec((1, KV_LORA), const),
            pl.BlockSpec(wukv.shape, const),
            pl.BlockSpec((tm, tabs.shape[1]), lambda i: (i % nseq, 0)),
        ],
        out_specs=out_specs,
        out_shape=out_shape,
        scratch_shapes=[pltpu.VMEM((3 * WIDTH_B // LANES, tm, LANES), F32)],
        compiler_params=pltpu.CompilerParams(dimension_semantics=("parallel",),
                                             vmem_limit_bytes=VMEM_LIMIT),
    )(x2, g, wbig, qn, wuq, kvn, wukv, tabs)


def _attn_a_kernel(qt_ref, k_ref, vt_ref, o_ref, a0_ref, a1_ref, b0_ref, b1_ref, *, tq, tk):
    s = k_ref.shape[1]
    nk, nq = s // tk, s // tq
    st_a, st_b = (a0_ref, a1_ref), (b0_ref, b1_ref)
    ones = jnp.ones((BF16_ROWS, tk), BF16)

    def scores(qb, c, st_ref):
        q0 = pl.multiple_of(qb * tq, tq)
        ks = pl.multiple_of(c * tk, tk)
        for j in range(2):
            st_ref[j] = _dot(k_ref[0, pl.ds(ks, tk), j * LANES:(j + 1) * LANES],
                             qt_ref[0, j * LANES:(j + 1) * LANES, pl.ds(q0, tq)])

    def accumulate(c, st_ref, carry):
        ks = pl.multiple_of(c * tk, tk)
        stats = []
        for j in range(2):
            m = carry[j][0]
            st = st_ref[j]
            m_new = jnp.maximum(m, jnp.max(st, axis=0, keepdims=True))
            stats.append((m_new, jnp.exp2(m - m_new), jnp.exp2(st - m_new).astype(BF16)))
        new = []
        for j in range(2):
            m_new, alpha, pt = stats[j]
            vt = jnp.concatenate([vt_ref[0, j * HEAD_DIM:(j + 1) * HEAD_DIM, pl.ds(ks, tk)], ones], axis=0)
            new.append((m_new, alpha * carry[j][1] + _dot(vt, pt)))
        return tuple(new)

    def half_step(qb, c, src, dst, carry, wrap=False):
        for u in range(2):
            if wrap:
                scores(jnp.minimum(qb + 1, nq - 1), u, dst[u])
            else:
                scores(qb, c + 2 + u, dst[u])
            carry = accumulate(c + u, src[u], carry)
        return carry

    def q_block(qb, _):
        def body(i, carry):
            carry = half_step(qb, 4 * i, st_a, st_b, carry)
            return half_step(qb, 4 * i + 2, st_b, st_a, carry)

        init = tuple((jnp.full((1, tq), NEG_INF, F32), jnp.zeros((HEAD_DIM + BF16_ROWS, tq), F32))
                     for _ in range(2))
        carry = lax.fori_loop(0, nk // 4 - 1, body, init)
        carry = half_step(qb, nk - 4, st_a, st_b, carry)
        res = half_step(qb, nk - 2, st_b, st_a, carry, wrap=True)
        q0 = pl.multiple_of(qb * tq, tq)
        for j in range(2):
            acc = res[j][1]
            o_ref[0, j * HEAD_DIM:(j + 1) * HEAD_DIM, pl.ds(q0, tq)] = (
                acc[:HEAD_DIM] / acc[HEAD_DIM:HEAD_DIM + 1]).astype(o_ref.dtype)
        return 0

    scores(0, 0, st_a[0])
    scores(0, 1, st_a[1])
    lax.fori_loop(0, nq, q_block, 0)


def _attn_a_call(qt, ka, vt, tq, tk):
    b, s, _ = ka.shape
    pairs = HEADS_A // 2
    assert s % (4 * tk) == 0
    return pl.pallas_call(
        functools.partial(_attn_a_kernel, tq=tq, tk=tk),
        grid=(b, pairs),
        in_specs=[
            pl.BlockSpec((1, 2 * LANES, s), lambda bi, p: (bi, p, 0)),
            pl.BlockSpec((1, s, 2 * LANES), lambda bi, p: (bi, 0, p)),
            pl.BlockSpec((1, LANES, s), lambda bi, p: (bi, p, 0)),
        ],
        out_specs=pl.BlockSpec((1, LANES, s), lambda bi, p: (bi, p, 0)),
        out_shape=jax.ShapeDtypeStruct((b, WIDTH_A, s), BF16),
        scratch_shapes=[pltpu.VMEM((2, tk, tq), F32) for _ in range(4)],
        compiler_params=pltpu.CompilerParams(dimension_semantics=("parallel", "parallel"),
                                             vmem_limit_bytes=VMEM_LIMIT),
    )(qt, ka, vt)


def _attn_b_kernel(q_ref, k_ref, v_ref, o_ref, lse_ref, band_ref, *, tq, half, n):
    total = q_ref.shape[0]
    kw = min(tq + 2 * half, n)
    first = _lane_is_first_head((tq, LANES))

    group = min(BLOCKS_PER_STEP, total // tq)
    diff = (lax.broadcasted_iota(jnp.int32, (tq, kw), 0) - lax.broadcasted_iota(jnp.int32, (tq, kw), 1))
    for var in range(band_ref.shape[0]):
        band_ref[var] = jnp.where(jnp.abs(diff + var * half) <= half, 0.0, NEG_INF)

    def body(it, carry):
        blocks = []
        for u in range(group):
            q0 = pl.multiple_of((it * group + u) * tq, tq)
            seg0 = (q0 // n) * n
            ks = pl.multiple_of(jnp.clip(q0 - half, seg0, seg0 + n - kw), half)
            blocks.append((q0, ks, q_ref[pl.ds(q0, tq), :], k_ref[pl.ds(ks, kw), :]))
        scores = [[_dot_nt(_keep_head(q, first, j), k) for j in range(2)] for (_, _, q, k) in blocks]
        probs = []
        for (q0, ks, _, _), s2 in zip(blocks, scores):
            band = band_ref[(q0 - ks) // half]
            row = []
            for j in range(2):
                s = s2[j] + band
                m = jnp.max(s, axis=-1, keepdims=True)
                p = jnp.exp2(s - m)
                den = jnp.sum(p, axis=-1, keepdims=True)
                row.append((p.astype(BF16), den, m + jnp.log2(den)))
            probs.append(row)
        for (q0, ks, _, _), row in zip(blocks, probs):
            v = v_ref[pl.ds(ks, kw), :]
            outs = [_dot(p, v) / den for (p, den, _) in row]
            lses = [jnp.broadcast_to(lse, (tq, LANES)) for (_, _, lse) in row]
            o_ref[pl.ds(q0, tq), :] = jnp.where(first, outs[0], outs[1]).astype(o_ref.dtype)
            lse_ref[pl.ds(q0, tq), :] = jnp.where(first, lses[0], lses[1])
        return carry

    lax.fori_loop(0, total // (tq * group), body, 0)


def _attn_b_call(q, k, v, half, tq):
    b, dil, n, w = q.shape
    pairs = w // LANES
    tq = min(tq, n)
    kw = min(tq + 2 * half, n)
    assert n % tq == 0 and (kw - tq) % half == 0
    flat = lambda a: a.reshape(b, dil * n, w)
    spec = pl.BlockSpec((None, dil * n, LANES), lambda bi, p: (bi, 0, p))
    o, lse = pl.pallas_call(
        functools.partial(_attn_b_kernel, tq=tq, half=half, n=n),
        grid=(b, pairs),
        in_specs=[spec, spec, spec],
        out_specs=[spec, spec],
        out_shape=[jax.ShapeDtypeStruct((b, dil * n, w), BF16), jax.ShapeDtypeStruct((b, dil * n, w), F32)],
        scratch_shapes=[pltpu.VMEM(((kw - tq) // half + 1, tq, kw), F32)],
        compiler_params=pltpu.CompilerParams(dimension_semantics=("parallel", "parallel"),
                                             vmem_limit_bytes=VMEM_LIMIT),
    )(flat(q), flat(k), flat(v))
    return o.reshape(q.shape), lse.reshape(q.shape)


def _na_bias_kernel(rpb_ref, o_ref):
    base = pl.program_id(0) * RPB_PER_HEAD
    shape = (GRID_W, LANES)
    lane = lax.broadcasted_iota(jnp.int32, shape, 1)
    p = lax.broadcasted_iota(jnp.int32, shape, 0)
    c = lane % GRID_W
    upper = lane >= GRID_W
    c_start = jnp.clip(p - NA_COLS // 2, 0, GRID_W - NA_COLS)
    col_ok = (c >= c_start) & (c < c_start + NA_COLS)
    dc = c - p + (NA_COLS - 1)
    n_dc = 2 * NA_COLS - 1
    for v in range(NA_ROWS):
        for m in range(NA_ROWS * GRID_W // LANES):
            a_lo = 2 * m - v + (NA_ROWS - 1)
            acc = jnp.full(shape, NEG_INF, F32)
            for b in range(n_dc):
                val = jnp.where(upper, rpb_ref[base + (a_lo + 1) * n_dc + b], rpb_ref[base + a_lo * n_dc + b])
                acc = jnp.where(dc == b, val, acc)
            o_ref[0, v, :, m * LANES:(m + 1) * LANES] = jnp.where(col_ok, acc, NEG_INF)


def _na_bias_call(rpb):
    nh = rpb.shape[0] * rpb.shape[1]
    return pl.pallas_call(
        _na_bias_kernel,
        grid=(nh,),
        in_specs=[pl.BlockSpec(memory_space=pltpu.SMEM)],
        out_specs=pl.BlockSpec((1, NA_ROWS, GRID_W, NA_ROWS * GRID_W), lambda g: (g, 0, 0, 0)),
        out_shape=jax.ShapeDtypeStruct((nh, NA_ROWS, GRID_W, NA_ROWS * GRID_W), F32),
    )(rpb.reshape(-1))


def _attn_c_kernel(q_ref, k_ref, v_ref, bias_ref, o_ref):
    rows = q_ref.shape[1] // GRID_W
    win = NA_ROWS * GRID_W
    first = _lane_is_first_head((GRID_W, LANES))

    group = BLOCKS_PER_STEP
    assert rows % group == 0

    def body(it, carry):
        blocks = []
        for u in range(group):
            r = it * group + u
            r_start = jnp.clip(r - NA_ROWS // 2, 0, rows - NA_ROWS)
            q0 = pl.multiple_of(r * GRID_W, GRID_W)
            ks = pl.multiple_of(r_start * GRID_W, GRID_W)
            blocks.append((q0, ks, r - r_start, q_ref[0, pl.ds(q0, GRID_W), :], k_ref[0, pl.ds(ks, win), :]))
        scores = [[_dot_nt(_keep_head(q, first, j), k) for j in range(2)] for (_, _, _, q, k) in blocks]
        probs = []
        for (_, _, variant, _, _), s2 in zip(blocks, scores):
            row = []
            for j in range(2):
                s = s2[j] + bias_ref[j, variant]
                m = jnp.max(s, axis=-1, keepdims=True)
                p = jnp.exp(s - m)
                row.append((p.astype(BF16), jnp.sum(p, axis=-1, keepdims=True)))
            probs.append(row)
        for (q0, ks, _, _, _), row in zip(blocks, probs):
            v = v_ref[0, pl.ds(ks, win), :]
            outs = [_dot(p, v) / den for (p, den) in row]
            o_ref[0, pl.ds(q0, GRID_W), :] = jnp.where(first, outs[0], outs[1]).astype(o_ref.dtype)
        return carry

    lax.fori_loop(0, rows // group, body, 0)


def _attn_c_call(q, k, v, bias):
    b, s, w = q.shape
    pairs = w // LANES
    spec = pl.BlockSpec((1, s, LANES), lambda bi, p: (bi, 0, p))
    return pl.pallas_call(
        _attn_c_kernel,
        grid=(b, pairs),
        in_specs=[spec, spec, spec,
                  pl.BlockSpec((2,) + bias.shape[1:], lambda bi, p: (p, 0, 0, 0))],
        out_specs=spec,
        out_shape=jax.ShapeDtypeStruct((b, s, w), BF16),
        compiler_params=pltpu.CompilerParams(dimension_semantics=("parallel", "parallel"),
                                             vmem_limit_bytes=VMEM_LIMIT),
    )(q, k, v, bias)


def _out_kernel(x_ref, oa_ref, ob1_ref, l1_ref, *rest):
    nd = len(_DILATIONS)
    dil_refs = rest[:2 * nd]
    oc_ref, ga_ref, gb_ref, gc_ref, w_ref, o_ref, stage_ref = rest[2 * nd:]
    tm = x_ref.shape[0]
    nblk = WIDTH_B // LANES
    oa = jnp.concatenate([oa_ref[0, blk * LANES:(blk + 1) * LANES, :].astype(F32).T
                          for blk in range(WIDTH_A // LANES)], axis=-1)
    na = _rms(oa, ga_ref[...]).astype(BF16)
    outs, lses = [ob1_ref[...].astype(F32)], [l1_ref[...]]
    for di, dil in enumerate(_DILATIONS):
        rows = tm // dil
        for which, acc in ((0, outs), (1, lses)):
            src = dil_refs[2 * di + which]
            slot = (2 * di + which) * nblk
            for r in range(dil):
                for blk in range(nblk):
                    stage_ref[slot + blk, pl.ds(r, rows, stride=dil), :] = (
                        src[0, r, :, blk * LANES:(blk + 1) * LANES].astype(F32))
            acc.append(jnp.concatenate([stage_ref[slot + blk] for blk in range(nblk)], axis=-1))
    lmax = functools.reduce(jnp.maximum, lses)
    es = [jnp.exp2(l - lmax) for l in lses]
    ob = sum(e * o for e, o in zip(es, outs)) / sum(es)
    nb = _rms(ob, gb_ref[...]).astype(BF16)
    nc = _rms(oc_ref[...].astype(F32), gc_ref[...]).astype(BF16)
    acc = _dot(na, w_ref[0:WIDTH_A, :])
    acc += _dot(nb, w_ref[WIDTH_A:WIDTH_A + WIDTH_B, :])
    acc += _dot(nc, w_ref[WIDTH_A + WIDTH_B:, :])
    o_ref[...] = x_ref[...] + acc


def _out_call(x2, oa, ob1, l1, dilated, oc, ga, gb, gc, w, seq, tm):
    t, d = x2.shape
    nseq = seq // tm
    row = lambda i: (i, 0)
    const = lambda i: (0, 0)
    rs = lambda width: pl.BlockSpec((tm, width), row)
    cs = lambda width: pl.BlockSpec((1, width), const)
    dil_specs, dil_args = [], []
    for dil, pair in zip(_DILATIONS, dilated):
        for a in pair:
            dil_specs.append(pl.BlockSpec((1, dil, tm // dil, WIDTH_B), lambda i: (i // nseq, 0, i % nseq, 0)))
            dil_args.append(a)
    return pl.pallas_call(
        _out_kernel,
        grid=(t // tm,),
        in_specs=[rs(d), pl.BlockSpec((1, WIDTH_A, tm), lambda i: (i // nseq, 0, i % nseq)),
                  rs(WIDTH_B), rs(WIDTH_B), *dil_specs,
                  rs(WIDTH_C), cs(WIDTH_A), cs(WIDTH_B), cs(WIDTH_C), pl.BlockSpec(w.shape, const)],
        out_specs=rs(d),
        out_shape=jax.ShapeDtypeStruct((t, d), F32),
        scratch_shapes=[pltpu.VMEM((2 * len(_DILATIONS) * WIDTH_B // LANES, tm, LANES), F32)],
        compiler_params=pltpu.CompilerParams(dimension_semantics=("parallel",),
                                             vmem_limit_bytes=VMEM_LIMIT),
    )(x2, oa, ob1, l1, *dil_args, oc, ga, gb, gc, w)


def _mlp_kernel(x_ref, g_ref, w1_ref, w2_ref, gf_ref, o_ref, h_ref, acc_ref, *, final_norm):
    f = pl.program_id(1)

    @pl.when(f == 0)
    def _():
        h_ref[...] = _rms(x_ref[...], g_ref[...]).astype(BF16)
        acc_ref[...] = jnp.zeros_like(acc_ref)

    u = jnp.maximum(_dot(h_ref[...], w1_ref[...]), 0.0)
    acc_ref[...] += _dot((u * u).astype(BF16), w2_ref[...])

    @pl.when(f == pl.num_programs(1) - 1)
    def _():
        y = x_ref[...] + acc_ref[...]
        if final_norm:
            y = _rms(y, gf_ref[...])
        o_ref[...] = y


def _mlp_call(x2, g, w1, w2, gf, final_norm, tm, tf):
    t, d = x2.shape
    dff = w1.shape[1]
    return pl.pallas_call(
        functools.partial(_mlp_kernel, final_norm=final_norm),
        grid=(t // tm, dff // tf),
        in_specs=[
            pl.BlockSpec((tm, d), lambda i, f: (i, 0)),
            pl.BlockSpec((1, d), lambda i, f: (0, 0)),
            pl.BlockSpec((d, tf), lambda i, f: (0, f)),
            pl.BlockSpec((tf, d), lambda i, f: (f, 0)),
            pl.BlockSpec((1, d), lambda i, f: (0, 0)),
        ],
        out_specs=pl.BlockSpec((tm, d), lambda i, f: (i, 0)),
        out_shape=jax.ShapeDtypeStruct((t, d), F32),
        scratch_shapes=[pltpu.VMEM((tm, d), BF16), pltpu.VMEM((tm, d), F32)],
        compiler_params=pltpu.CompilerParams(dimension_semantics=("parallel", "arbitrary"),
                                             vmem_limit_bytes=VMEM_LIMIT),
    )(x2, g, w1, w2, gf)


def _rotate_half_cols(w, half):
    return jnp.concatenate([-w[..., half:], w[..., :half]], axis=-1)


def _rope_tables(seq):
    pos = jnp.arange(seq, dtype=F32)

    def cos_sin(half):
        inv_freq = ROPE_THETA ** (-jnp.arange(half, dtype=F32) / half)
        ang = pos[:, None] * inv_freq[None, :]
        return jnp.cos(ang), jnp.sin(ang)

    ca, sa = cos_sin(QK_ROPE // 2)
    ca2, sa2 = jnp.concatenate([ca, ca], -1), jnp.concatenate([sa, sa], -1)
    ones = jnp.ones((seq, QK_NOPE), F32)
    zeros = jnp.zeros((seq, QK_NOPE), F32)
    tail = jnp.zeros((seq, LANES - QK_NOPE - QK_ROPE), F32)
    scale_a = (QK_NOPE + QK_ROPE) ** -0.5 * LOG2_E
    cosq = jnp.concatenate([ones, ca2, tail], -1) * scale_a
    sinq = jnp.concatenate([zeros, sa2, tail], -1) * scale_a
    cosk = jnp.concatenate([zeros, ca2, tail], -1)
    sink = jnp.concatenate([zeros, sa2, tail], -1)
    cb, sb = cos_sin(HEAD_DIM // 2)
    cosb = jnp.concatenate([cb, cb, cb, cb], -1)
    sinb = jnp.concatenate([-sb, sb, -sb, sb], -1)
    return jnp.concatenate([cosq, sinq, cosk, sink, cosb, sinb], -1)


def _layer_weights(w_in, w_uq, w_ukv):
    d = w_in.shape[0]
    scale = HEAD_DIM ** -0.5
    c_b = Q_LORA + KV_LORA + QK_ROPE
    c_c = c_b + 3 * WIDTH_B
    w_kpe = w_in[:, Q_LORA + KV_LORA:c_b]

    def place(w):
        return jnp.concatenate([jnp.zeros((d, QK_NOPE), F32), w, jnp.zeros((d, LANES - QK_NOPE - QK_ROPE), F32)], -1)

    wbig = jnp.concatenate([
        w_in[:, :Q_LORA + KV_LORA],
        place(w_kpe), place(_rotate_half_cols(w_kpe, QK_ROPE // 2)),
        w_in[:, c_b:c_b + WIDTH_B] * scale, w_in[:, c_b + WIDTH_B:c_c],
        w_in[:, c_c:c_c + WIDTH_C] * scale, w_in[:, c_c + WIDTH_C:],
    ], -1).astype(BF16)

    uq = w_uq.reshape(Q_LORA, HEADS_A, QK_NOPE + QK_ROPE)
    pad = jnp.zeros((Q_LORA, HEADS_A, LANES - QK_NOPE - QK_ROPE), F32)
    uq_pad = jnp.concatenate([uq, pad], -1)
    uq_rot = jnp.concatenate([jnp.zeros((Q_LORA, HEADS_A, QK_NOPE), F32),
                              _rotate_half_cols(uq[..., QK_NOPE:], QK_ROPE // 2), pad], -1)
    wuq = jnp.concatenate([uq_pad.reshape(Q_LORA, QK_A_PAD), uq_rot.reshape(Q_LORA, QK_A_PAD)], -1).astype(BF16)

    ukv = w_ukv.reshape(KV_LORA, HEADS_A, QK_NOPE + HEAD_DIM)
    uk_pad = jnp.concatenate([ukv[..., :QK_NOPE], jnp.zeros((KV_LORA, HEADS_A, LANES - QK_NOPE), F32)], -1)
    wukv = jnp.concatenate([uk_pad.reshape(KV_LORA, QK_A_PAD), ukv[..., QK_NOPE:].reshape(KV_LORA, WIDTH_A)],
                           -1).astype(BF16)
    return wbig, wuq, wukv


def kernel(x, g_mix, w_in, q_norm, w_uq, kv_norm, w_ukv, rpb, out_norm_a, out_norm_b, out_norm_c, w_out, g_mlp,
           w_mlp_in, w_mlp_out, g_final):
    b, s, d = x.shape
    depth = w_in.shape[0]
    t = b * s
    assert s % GRID_W == 0 and s // GRID_W >= NA_ROWS
    tm_proj = min(512, s)
    tm_out = min(512, s)
    tm_mlp = min(1024, t)
    tq_a = min(256, s)
    tk_a = min(512, s)

    tabs = _rope_tables(s)
    na_bias = _na_bias_call(rpb).reshape(depth, HEADS_C, NA_ROWS, GRID_W, NA_ROWS * GRID_W)
    row = lambda a: a.reshape(1, -1)

    x2 = x.reshape(t, d)
    for l in range(depth):
        wbig, wuq, wukv = _layer_weights(w_in[l], w_uq[l], w_ukv[l])
        qa, ka, va, qb, kb, vb, qc, kc, vc, *dil_qkv = _proj_call(
            x2, row(g_mix[l]), wbig, row(q_norm[l]), wuq, row(kv_norm[l]), wukv, tabs, s, tm_proj)
        seq3 = lambda a: a.reshape(b, s, a.shape[-1])
        oa = _attn_a_call(qa, seq3(ka), va, tq_a, tk_a)
        ob1, l1, dilated = None, None, []
        for window, dil in DILATED_PAIRS:
            half = window // (2 * dil)
            if dil == 1:
                unit = lambda a: a.reshape(b, 1, s, WIDTH_B)
                o_i, lse_i = _attn_b_call(unit(qb), unit(kb), unit(vb), half, 128)
                ob1, l1 = o_i.reshape(t, WIDTH_B), lse_i.reshape(t, WIDTH_B)
            else:
                di = 3 * _DILATIONS.index(dil)
                dilated.append(_attn_b_call(*dil_qkv[di:di + 3], half, 128))
        oc = _attn_c_call(seq3(qc), seq3(kc), seq3(vc), na_bias[l]).reshape(t, WIDTH_C)
        x2 = _out_call(x2, oa, ob1, l1, dilated, oc, row(out_norm_a[l]), row(out_norm_b[l]), row(out_norm_c[l]),
                       w_out[l].astype(BF16), s, tm_out)
        x2 = _mlp_call(x2, row(g_mlp[l]), w_mlp_in[l].astype(BF16), w_mlp_out[l].astype(BF16), row(g_final),
                       l == depth - 1, tm_mlp, min(1024, w_mlp_in.shape[2]))
    return x2.reshape(b, s, d)
```

```python
import functools

import jax
import jax.numpy as jnp
from jax import lax
from jax.experimental import pallas as pl
from jax.experimental.pallas import tpu as pltpu

HEAD_DIM = 64
LANES = 128
BF16_ROWS = 16
HEADS_A = 6
HEADS_B = 6
HEADS_C = 4
Q_LORA = 256
KV_LORA = 128
QK_NOPE = 64
QK_ROPE = 32
DILATED_PAIRS = ((128, 1), (512, 4), (2048, 16))
_DILATIONS = tuple(dil for _, dil in DILATED_PAIRS if dil > 1)
GRID_W = 64
NA_ROWS = 8
NA_COLS = 16
ROPE_THETA = 10000.0
NORM_EPS = 1e-6
NEG_INF = -1e30
LOG2_E = 1.4426950408889634
BLOCKS_PER_STEP = 4

WIDTH_A = HEADS_A * HEAD_DIM
WIDTH_B = HEADS_B * HEAD_DIM
WIDTH_C = HEADS_C * HEAD_DIM
QK_A_PAD = HEADS_A * LANES
RPB_PER_HEAD = (2 * NA_ROWS - 1) * (2 * NA_COLS - 1)

VMEM_LIMIT = 56 * 1024 * 1024

BF16 = jnp.bfloat16
F32 = jnp.float32


def _rms(x, g):
    return x * lax.rsqrt(jnp.mean(x * x, axis=-1, keepdims=True) + NORM_EPS) * g


def _dot(a, b):
    return jnp.dot(a, b, preferred_element_type=F32)


def _dot_nt(a, b):
    return lax.dot_general(a, b, (((1,), (1,)), ((), ())), preferred_element_type=F32)


def _lane_is_first_head(shape):
    return lax.broadcasted_iota(jnp.int32, shape, len(shape) - 1) < HEAD_DIM


def _stack_heads(q, first):
    return jnp.concatenate([_keep_head(q, first, 0), _keep_head(q, first, 1)], axis=0)


def _keep_head(q, first, j):
    zero = jnp.zeros_like(q)
    return jnp.where(first, q, zero) if j == 0 else jnp.where(first, zero, q)


_C_CQ = 0
_C_CKV = _C_CQ + Q_LORA
_C_KPE = _C_CKV + KV_LORA
_C_KPR = _C_KPE + LANES
_C_QB = _C_KPR + LANES
_C_KB = _C_QB + WIDTH_B
_C_VB = _C_KB + WIDTH_B
_C_QC = _C_VB + WIDTH_B
_C_KC = _C_QC + WIDTH_C
_C_VC = _C_KC + WIDTH_C
_C_END = _C_VC + WIDTH_C

_T_COSQ, _T_SINQ, _T_COSK, _T_SINK, _T_COSB, _T_SINB = range(6)


def _proj_kernel(x_ref, g_ref, wbig_ref, qn_ref, wuq_ref, kvn_ref, wukv_ref, tab_ref,
                 qa_ref, ka_ref, va_ref, qb_ref, kb_ref, vb_ref, qc_ref, kc_ref, vc_ref, *rest):
    dil_refs, stage_ref = rest[:-1], rest[-1]
    tm = x_ref.shape[0]

    def tab(i):
        return tab_ref[:, i * LANES:(i + 1) * LANES]

    h = _rms(x_ref[...], g_ref[...]).astype(BF16)
    proj = _dot(h, wbig_ref[...])

    cqn = _rms(proj[:, _C_CQ:_C_CKV], qn_ref[...]).astype(BF16)
    qa2 = _dot(cqn, wuq_ref[...])
    ckvn = _rms(proj[:, _C_CKV:_C_KPE], kvn_ref[...]).astype(BF16)
    kv2 = _dot(ckvn, wukv_ref[...])
    kpe = proj[:, _C_KPE:_C_KPR] * tab(_T_COSK) + proj[:, _C_KPR:_C_QB] * tab(_T_SINK)
    cosq, sinq = tab(_T_COSQ), tab(_T_SINQ)
    for hd in range(HEADS_A):
        sl = slice(hd * LANES, (hd + 1) * LANES)
        rot = slice(QK_A_PAD + hd * LANES, QK_A_PAD + (hd + 1) * LANES)
        qa_ref[0, sl, :] = (qa2[:, sl] * cosq + qa2[:, rot] * sinq).T.astype(BF16)
        ka_ref[:, sl] = (kv2[:, sl] + kpe).astype(BF16)
    for blk in range(WIDTH_A // LANES):
        sl = slice(blk * LANES, (blk + 1) * LANES)
        va_ref[0, sl, :] = kv2[:, QK_A_PAD + blk * LANES:QK_A_PAD + (blk + 1) * LANES].T.astype(BF16)

    cosb, sinb = tab(_T_COSB), tab(_T_SINB)
    first_half = (lax.broadcasted_iota(jnp.int32, cosb.shape, 1) % HEAD_DIM) < HEAD_DIM // 2
    nblk = WIDTH_B // LANES
    for ti, (src, dst) in enumerate(((_C_QB, qb_ref), (_C_KB, kb_ref), (_C_VB, vb_ref))):
        for blk in range(nblk):
            xb = proj[:, src + blk * LANES:src + (blk + 1) * LANES]
            if dst is not vb_ref:
                swapped = jnp.where(first_half, pltpu.roll(xb, LANES - HEAD_DIM // 2, 1),
                                    pltpu.roll(xb, HEAD_DIM // 2, 1))
                xb = xb * cosb + swapped * sinb
                if dst is qb_ref:
                    xb = xb * LOG2_E
            dst[:, blk * LANES:(blk + 1) * LANES] = xb.astype(BF16)
            stage_ref[ti * nblk + blk] = xb
    for di, dil in enumerate(_DILATIONS):
        rows = tm // dil
        for ti in range(3):
            dref = dil_refs[di * 3 + ti]
            for r in range(dil):
                for blk in range(nblk):
                    dref[0, r, :, blk * LANES:(blk + 1) * LANES] = (
                        stage_ref[ti * nblk + blk, pl.ds(r, rows, stride=dil), :].astype(BF16))

    qc_ref[...] = proj[:, _C_QC:_C_KC].astype(BF16)
    kc_ref[...] = proj[:, _C_KC:_C_VC].astype(BF16)
    vc_ref[...] = proj[:, _C_VC:_C_END].astype(BF16)


def _proj_call(x2, g, wbig, qn, wuq, kvn, wukv, tabs, seq, tm):
    t, d = x2.shape
    nseq = seq // tm
    row = lambda i: (i, 0)
    const = lambda i: (0, 0)
    widths = (QK_A_PAD, QK_A_PAD, WIDTH_A, WIDTH_B, WIDTH_B, WIDTH_B, WIDTH_C, WIDTH_C, WIDTH_C)
    out_specs = [pl.BlockSpec((tm, w), row) for w in widths]
    out_shape = [jax.ShapeDtypeStruct((t, w), BF16) for w in widths]
    for idx in (0, 2):
        out_specs[idx] = pl.BlockSpec((1, widths[idx], tm), lambda i: (i // nseq, 0, i % nseq))
        out_shape[idx] = jax.ShapeDtypeStruct((t // seq, widths[idx], seq), BF16)
    for dil in _DILATIONS:
        assert tm % (16 * dil) == 0
        for _ in range(3):
            out_specs.append(pl.BlockSpec((1, dil, tm // dil, WIDTH_B), lambda i: (i // nseq, 0, i % nseq, 0)))
            out_shape.append(jax.ShapeDtypeStruct((t // seq, dil, seq // dil, WIDTH_B), BF16))
    return pl.pallas_call(
        _proj_kernel,
        grid=(t // tm,),
        in_specs=[
            pl.BlockSpec((tm, d), row),
            pl.BlockSpec((1, d), const),
            pl.BlockSpec(wbig.shape, const),
            pl.BlockSpec((1, Q_LORA), const),
            pl.BlockSpec(wuq.shape, const),
            pl.BlockSpec((1, KV_LORA), const),
            pl.BlockSpec(wukv.shape, const),
            pl.BlockSpec((tm, tabs.shape[1]), lambda i: (i % nseq, 0)),
        ],
        out_specs=out_specs,
        out_shape=out_shape,
        scratch_shapes=[pltpu.VMEM((3 * WIDTH_B // LANES, tm, LANES), F32)],
        compiler_params=pltpu.CompilerParams(dimension_semantics=("parallel",),
                                             vmem_limit_bytes=VMEM_LIMIT),
    )(x2, g, wbig, qn, wuq, kvn, wukv, tabs)


def _attn_a_kernel(qt_ref, k_ref, vt_ref, o_ref, a0_ref, a1_ref, b0_ref, b1_ref, *, tq, tk):
    s = k_ref.shape[1]
    nk, nq = s // tk, s // tq
    st_a, st_b = (a0_ref, a1_ref), (b0_ref, b1_ref)
    ones = jnp.ones((BF16_ROWS, tk), BF16)

    def scores(qb, c, st_ref):
        q0 = pl.multiple_of(qb * tq, tq)
        ks = pl.multiple_of(c * tk, tk)
        for j in range(2):
            st_ref[j] = _dot(k_ref[0, pl.ds(ks, tk), j * LANES:(j + 1) * LANES],
                             qt_ref[0, j * LANES:(j + 1) * LANES, pl.ds(q0, tq)])

    def accumulate(c, st_ref, carry):
        ks = pl.multiple_of(c * tk, tk)
        stats = []
        for j in range(2):
            m = carry[j][0]
            st = st_ref[j]
            m_new = jnp.maximum(m, jnp.max(st, axis=0, keepdims=True))
            stats.append((m_new, jnp.exp2(m - m_new), jnp.exp2(st - m_new).astype(BF16)))
        new = []
        for j in range(2):
            m_new, alpha, pt = stats[j]
            vt = jnp.concatenate([vt_ref[0, j * HEAD_DIM:(j + 1) * HEAD_DIM, pl.ds(ks, tk)], ones], axis=0)
            new.append((m_new, alpha * carry[j][1] + _dot(vt, pt)))
        return tuple(new)

    def half_step(qb, c, src, dst, carry, wrap=False):
        for u in range(2):
            if wrap:
                scores(jnp.minimum(qb + 1, nq - 1), u, dst[u])
            else:
                scores(qb, c + 2 + u, dst[u])
            carry = accumulate(c + u, src[u], carry)
        return carry

    def q_block(qb, _):
        def body(i, carry):
            carry = half_step(qb, 4 * i, st_a, st_b, carry)
            return half_step(qb, 4 * i + 2, st_b, st_a, carry)

        init = tuple((jnp.full((1, tq), NEG_INF, F32), jnp.zeros((HEAD_DIM + BF16_ROWS, tq), F32))
                     for _ in range(2))
        carry = lax.fori_loop(0, nk // 4 - 1, body, init)
        carry = half_step(qb, nk - 4, st_a, st_b, carry)
        res = half_step(qb, nk - 2, st_b, st_a, carry, wrap=True)
        q0 = pl.multiple_of(qb * tq, tq)
        for j in range(2):
            acc = res[j][1]
            o_ref[0, j * HEAD_DIM:(j + 1) * HEAD_DIM, pl.ds(q0, tq)] = (
                acc[:HEAD_DIM] / acc[HEAD_DIM:HEAD_DIM + 1]).astype(o_ref.dtype)
        return 0

    scores(0, 0, st_a[0])
    scores(0, 1, st_a[1])
    lax.fori_loop(0, nq, q_block, 0)


def _attn_a_call(qt, ka, vt, tq, tk):
    b, s, _ = ka.shape
    pairs = HEADS_A // 2
    assert s % (4 * tk) == 0
    return pl.pallas_call(
        functools.partial(_attn_a_kernel, tq=tq, tk=tk),
        grid=(b, pairs),
        in_specs=[
            pl.BlockSpec((1, 2 * LANES, s), lambda bi, p: (bi, p, 0)),
            pl.BlockSpec((1, s, 2 * LANES), lambda bi, p: (bi, 0, p)),
            pl.BlockSpec((1, LANES, s), lambda bi, p: (bi, p, 0)),
        ],
        out_specs=pl.BlockSpec((1, LANES, s), lambda bi, p: (bi, p, 0)),
        out_shape=jax.ShapeDtypeStruct((b, WIDTH_A, s), BF16),
        scratch_shapes=[pltpu.VMEM((2, tk, tq), F32) for _ in range(4)],
        compiler_params=pltpu.CompilerParams(dimension_semantics=("parallel", "parallel"),
                                             vmem_limit_bytes=VMEM_LIMIT),
    )(qt, ka, vt)


def _attn_b_kernel(q_ref, k_ref, v_ref, o_ref, lse_ref, band_ref, sa_ref, sb_ref, *, tq, half, n):
    total = q_ref.shape[0]
    kw = min(tq + 2 * half, n)
    first = _lane_is_first_head((tq, LANES))

    group = min(BLOCKS_PER_STEP, total // tq)
    diff = (lax.broadcasted_iota(jnp.int32, (tq, kw), 0) - lax.broadcasted_iota(jnp.int32, (tq, kw), 1))
    for var in range(band_ref.shape[0]):
        band_ref[var] = jnp.where(jnp.abs(diff + var * half) <= half, 0.0, NEG_INF)

    n_groups = total // (tq * group)

    def window(g, u):
        q0 = pl.multiple_of((g * group + u) * tq, tq)
        seg0 = (q0 // n) * n
        return q0, pl.multiple_of(jnp.clip(q0 - half, seg0, seg0 + n - kw), half)

    def issue(g, st_ref):
        g = jnp.minimum(g, n_groups - 1)
        for u in range(group):
            q0, ks = window(g, u)
            st_ref[u] = _dot_nt(_stack_heads(q_ref[pl.ds(q0, tq), :], first), k_ref[pl.ds(ks, kw), :])

    def consume(g, st_ref):
        probs = []
        for u in range(group):
            q0, ks = window(g, u)
            band = band_ref[(q0 - ks) // half]
            row = []
            for j in range(2):
                s = st_ref[u, j * tq:(j + 1) * tq, :] + band
                m = jnp.max(s, axis=-1, keepdims=True)
                p = jnp.exp2(s - m)
                den = jnp.sum(p, axis=-1, keepdims=True)
                row.append((p.astype(BF16), den, m + jnp.log2(den)))
            probs.append(row)
        for u, row in enumerate(probs):
            q0, ks = window(g, u)
            o2 = _dot(jnp.concatenate([row[0][0], row[1][0]], axis=0), v_ref[pl.ds(ks, kw), :])
            outs = [o2[j * tq:(j + 1) * tq] / row[j][1] for j in range(2)]
            lses = [jnp.broadcast_to(lse, (tq, LANES)) for (_, _, lse) in row]
            o_ref[pl.ds(q0, tq), :] = jnp.where(first, outs[0], outs[1]).astype(o_ref.dtype)
            lse_ref[pl.ds(q0, tq), :] = jnp.where(first, lses[0], lses[1])

    def body(it, carry):
        issue(2 * it + 1, sb_ref)
        consume(2 * it, sa_ref)
        issue(2 * it + 2, sa_ref)
        consume(2 * it + 1, sb_ref)
        return carry

    issue(0, sa_ref)
    lax.fori_loop(0, n_groups // 2, body, 0)


def _attn_b_call(q, k, v, half, tq):
    b, dil, n, w = q.shape
    pairs = w // LANES
    tq = min(tq, n)
    kw = min(tq + 2 * half, n)
    group = min(BLOCKS_PER_STEP, dil * n // tq)
    assert n % tq == 0 and (kw - tq) % half == 0 and (dil * n) % (2 * group * tq) == 0
    flat = lambda a: a.reshape(b, dil * n, w)
    spec = pl.BlockSpec((None, dil * n, LANES), lambda bi, p: (bi, 0, p))
    o, lse = pl.pallas_call(
        functools.partial(_attn_b_kernel, tq=tq, half=half, n=n),
        grid=(b, pairs),
        in_specs=[spec, spec, spec],
        out_specs=[spec, spec],
        out_shape=[jax.ShapeDtypeStruct((b, dil * n, w), BF16), jax.ShapeDtypeStruct((b, dil * n, w), F32)],
        scratch_shapes=[pltpu.VMEM(((kw - tq) // half + 1, tq, kw), F32),
                        pltpu.VMEM((group, 2 * tq, kw), F32), pltpu.VMEM((group, 2 * tq, kw), F32)],
        compiler_params=pltpu.CompilerParams(dimension_semantics=("parallel", "parallel"),
                                             vmem_limit_bytes=VMEM_LIMIT),
    )(flat(q), flat(k), flat(v))
    return o.reshape(q.shape), lse.reshape(q.shape)


def _na_bias_kernel(rpb_ref, o_ref):
    base = pl.program_id(0) * RPB_PER_HEAD
    shape = (GRID_W, LANES)
    lane = lax.broadcasted_iota(jnp.int32, shape, 1)
    p = lax.broadcasted_iota(jnp.int32, shape, 0)
    c = lane % GRID_W
    upper = lane >= GRID_W
    c_start = jnp.clip(p - NA_COLS // 2, 0, GRID_W - NA_COLS)
    col_ok = (c >= c_start) & (c < c_start + NA_COLS)
    dc = c - p + (NA_COLS - 1)
    n_dc = 2 * NA_COLS - 1
    for v in range(NA_ROWS):
        for m in range(NA_ROWS * GRID_W // LANES):
            a_lo = 2 * m - v + (NA_ROWS - 1)
            acc = jnp.full(shape, NEG_INF, F32)
            for b in range(n_dc):
                val = jnp.where(upper, rpb_ref[base + (a_lo + 1) * n_dc + b], rpb_ref[base + a_lo * n_dc + b])
                acc = jnp.where(dc == b, val, acc)
            o_ref[0, v, :, m * LANES:(m + 1) * LANES] = jnp.where(col_ok, acc, NEG_INF)


def _na_bias_call(rpb):
    nh = rpb.shape[0] * rpb.shape[1]
    return pl.pallas_call(
        _na_bias_kernel,
        grid=(nh,),
        in_specs=[pl.BlockSpec(memory_space=pltpu.SMEM)],
        out_specs=pl.BlockSpec((1, NA_ROWS, GRID_W, NA_ROWS * GRID_W), lambda g: (g, 0, 0, 0)),
        out_shape=jax.ShapeDtypeStruct((nh, NA_ROWS, GRID_W, NA_ROWS * GRID_W), F32),
    )(rpb.reshape(-1))


def _attn_c_kernel(q_ref, k_ref, v_ref, bias_ref, o_ref, sa_ref, sb_ref):
    rows = q_ref.shape[1] // GRID_W
    win = NA_ROWS * GRID_W
    first = _lane_is_first_head((GRID_W, LANES))
    group = sa_ref.shape[0]
    n_groups = rows // group

    def window(g, u):
        r = g * group + u
        r_start = jnp.clip(r - NA_ROWS // 2, 0, rows - NA_ROWS)
        return pl.multiple_of(r * GRID_W, GRID_W), pl.multiple_of(r_start * GRID_W, GRID_W), r - r_start

    def issue(g, st_ref):
        g = jnp.minimum(g, n_groups - 1)
        for u in range(group):
            q0, ks, _ = window(g, u)
            st_ref[u] = _dot_nt(_stack_heads(q_ref[0, pl.ds(q0, GRID_W), :], first), k_ref[0, pl.ds(ks, win), :])

    def consume(g, st_ref):
        probs = []
        for u in range(group):
            variant = window(g, u)[2]
            row = []
            for j in range(2):
                s = st_ref[u, j * GRID_W:(j + 1) * GRID_W, :] + bias_ref[j, variant]
                m = jnp.max(s, axis=-1, keepdims=True)
                p = jnp.exp(s - m)
                row.append((p.astype(BF16), jnp.sum(p, axis=-1, keepdims=True)))
            probs.append(row)
        for u, row in enumerate(probs):
            q0, ks, _ = window(g, u)
            o2 = _dot(jnp.concatenate([row[0][0], row[1][0]], axis=0), v_ref[0, pl.ds(ks, win), :])
            outs = [o2[j * GRID_W:(j + 1) * GRID_W] / row[j][1] for j in range(2)]
            o_ref[0, pl.ds(q0, GRID_W), :] = jnp.where(first, outs[0], outs[1]).astype(o_ref.dtype)

    def body(it, carry):
        issue(2 * it + 1, sb_ref)
        consume(2 * it, sa_ref)
        issue(2 * it + 2, sa_ref)
        consume(2 * it + 1, sb_ref)
        return carry

    issue(0, sa_ref)
    lax.fori_loop(0, n_groups // 2, body, 0)


def _attn_c_call(q, k, v, bias):
    b, s, w = q.shape
    pairs = w // LANES
    spec = pl.BlockSpec((1, s, LANES), lambda bi, p: (bi, 0, p))
    return pl.pallas_call(
        _attn_c_kernel,
        grid=(b, pairs),
        in_specs=[spec, spec, spec,
                  pl.BlockSpec((2,) + bias.shape[1:], lambda bi, p: (p, 0, 0, 0))],
        out_specs=spec,
        out_shape=jax.ShapeDtypeStruct((b, s, w), BF16),
        scratch_shapes=[pltpu.VMEM((BLOCKS_PER_STEP, 2 * GRID_W, NA_ROWS * GRID_W), F32) for _ in range(2)],
        compiler_params=pltpu.CompilerParams(dimension_semantics=("parallel", "parallel"),
                                             vmem_limit_bytes=VMEM_LIMIT),
    )(q, k, v, bias)


def _out_kernel(x_ref, oa_ref, ob1_ref, l1_ref, *rest):
    nd = len(_DILATIONS)
    dil_refs = rest[:2 * nd]
    oc_ref, ga_ref, gb_ref, gc_ref, w_ref, o_ref, stage_ref = rest[2 * nd:]
    tm = x_ref.shape[0]
    nblk = WIDTH_B // LANES
    oa = jnp.concatenate([oa_ref[0, blk * LANES:(blk + 1) * LANES, :].astype(F32).T
                          for blk in range(WIDTH_A // LANES)], axis=-1)
    na = _rms(oa, ga_ref[...]).astype(BF16)
    outs, lses = [ob1_ref[...].astype(F32)], [l1_ref[...]]
    for di, dil in enumerate(_DILATIONS):
        rows = tm // dil
        for which, acc in ((0, outs), (1, lses)):
            src = dil_refs[2 * di + which]
            slot = (2 * di + which) * nblk
            for r in range(dil):
                for blk in range(nblk):
                    stage_ref[slot + blk, pl.ds(r, rows, stride=dil), :] = (
                        src[0, r, :, blk * LANES:(blk + 1) * LANES].astype(F32))
            acc.append(jnp.concatenate([stage_ref[slot + blk] for blk in range(nblk)], axis=-1))
    lmax = functools.reduce(jnp.maximum, lses)
    es = [jnp.exp2(l - lmax) for l in lses]
    ob = sum(e * o for e, o in zip(es, outs)) / sum(es)
    nb = _rms(ob, gb_ref[...]).astype(BF16)
    nc = _rms(oc_ref[...].astype(F32), gc_ref[...]).astype(BF16)
    acc = _dot(na, w_ref[0:WIDTH_A, :])
    acc += _dot(nb, w_ref[WIDTH_A:WIDTH_A + WIDTH_B, :])
    acc += _dot(nc, w_ref[WIDTH_A + WIDTH_B:, :])
    o_ref[...] = x_ref[...] + acc


def _out_call(x2, oa, ob1, l1, dilated, oc, ga, gb, gc, w, seq, tm):
    t, d = x2.shape
    nseq = seq // tm
    row = lambda i: (i, 0)
    const = lambda i: (0, 0)
    rs = lambda width: pl.BlockSpec((tm, width), row)
    cs = lambda width: pl.BlockSpec((1, width), const)
    dil_specs, dil_args = [], []
    for dil, pair in zip(_DILATIONS, dilated):
        for a in pair:
            dil_specs.append(pl.BlockSpec((1, dil, tm // dil, WIDTH_B), lambda i: (i // nseq, 0, i % nseq, 0)))
            dil_args.append(a)
    return pl.pallas_call(
        _out_kernel,
        grid=(t // tm,),
        in_specs=[rs(d), pl.BlockSpec((1, WIDTH_A, tm), lambda i: (i // nseq, 0, i % nseq)),
                  rs(WIDTH_B), rs(WIDTH_B), *dil_specs,
                  rs(WIDTH_C), cs(WIDTH_A), cs(WIDTH_B), cs(WIDTH_C), pl.BlockSpec(w.shape, const)],
        out_specs=rs(d),
        out_shape=jax.ShapeDtypeStruct((t, d), F32),
        scratch_shapes=[pltpu.VMEM((2 * len(_DILATIONS) * WIDTH_B // LANES, tm, LANES), F32)],
        compiler_params=pltpu.CompilerParams(dimension_semantics=("parallel",),
                                             vmem_limit_bytes=VMEM_LIMIT),
    )(x2, oa, ob1, l1, *dil_args, oc, ga, gb, gc, w)


def _mlp_kernel(x_ref, g_ref, w1_ref, w2_ref, gf_ref, o_ref, h_ref, acc_ref, *, final_norm):
    f = pl.program_id(1)

    @pl.when(f == 0)
    def _():
        h_ref[...] = _rms(x_ref[...], g_ref[...]).astype(BF16)
        acc_ref[...] = jnp.zeros_like(acc_ref)

    u = jnp.maximum(_dot(h_ref[...], w1_ref[...]), 0.0)
    acc_ref[...] += _dot((u * u).astype(BF16), w2_ref[...])

    @pl.when(f == pl.num_programs(1) - 1)
    def _():
        y = x_ref[...] + acc_ref[...]
        if final_norm:
            y = _rms(y, gf_ref[...])
        o_ref[...] = y


def _mlp_call(x2, g, w1, w2, gf, final_norm, tm, tf):
    t, d = x2.shape
    dff = w1.shape[1]
    return pl.pallas_call(
        functools.partial(_mlp_kernel, final_norm=final_norm),
        grid=(t // tm, dff // tf),
        in_specs=[
            pl.BlockSpec((tm, d), lambda i, f: (i, 0)),
            pl.BlockSpec((1, d), lambda i, f: (0, 0)),
            pl.BlockSpec((d, tf), lambda i, f: (0, f)),
            pl.BlockSpec((tf, d), lambda i, f: (f, 0)),
            pl.BlockSpec((1, d), lambda i, f: (0, 0)),
        ],
        out_specs=pl.BlockSpec((tm, d), lambda i, f: (i, 0)),
        out_shape=jax.ShapeDtypeStruct((t, d), F32),
        scratch_shapes=[pltpu.VMEM((tm, d), BF16), pltpu.VMEM((tm, d), F32)],
        compiler_params=pltpu.CompilerParams(dimension_semantics=("parallel", "arbitrary"),
                                             vmem_limit_bytes=VMEM_LIMIT),
    )(x2, g, w1, w2, gf)


def _rotate_half_cols(w, half):
    return jnp.concatenate([-w[..., half:], w[..., :half]], axis=-1)


def _rope_tables(seq):
    pos = jnp.arange(seq, dtype=F32)

    def cos_sin(half):
        inv_freq = ROPE_THETA ** (-jnp.arange(half, dtype=F32) / half)
        ang = pos[:, None] * inv_freq[None, :]
        return jnp.cos(ang), jnp.sin(ang)

    ca, sa = cos_sin(QK_ROPE // 2)
    ca2, sa2 = jnp.concatenate([ca, ca], -1), jnp.concatenate([sa, sa], -1)
    ones = jnp.ones((seq, QK_NOPE), F32)
    zeros = jnp.zeros((seq, QK_NOPE), F32)
    tail = jnp.zeros((seq, LANES - QK_NOPE - QK_ROPE), F32)
    scale_a = (QK_NOPE + QK_ROPE) ** -0.5 * LOG2_E
    cosq = jnp.concatenate([ones, ca2, tail], -1) * scale_a
    sinq = jnp.concatenate([zeros, sa2, tail], -1) * scale_a
    cosk = jnp.concatenate([zeros, ca2, tail], -1)
    sink = jnp.concatenate([zeros, sa2, tail], -1)
    cb, sb = cos_sin(HEAD_DIM // 2)
    cosb = jnp.concatenate([cb, cb, cb, cb], -1)
    sinb = jnp.concatenate([-sb, sb, -sb, sb], -1)
    return jnp.concatenate([cosq, sinq, cosk, sink, cosb, sinb], -1)


def _layer_weights(w_in, w_uq, w_ukv):
    d = w_in.shape[0]
    scale = HEAD_DIM ** -0.5
    c_b = Q_LORA + KV_LORA + QK_ROPE
    c_c = c_b + 3 * WIDTH_B
    w_kpe = w_in[:, Q_LORA + KV_LORA:c_b]

    def place(w):
        return jnp.concatenate([jnp.zeros((d, QK_NOPE), F32), w, jnp.zeros((d, LANES - QK_NOPE - QK_ROPE), F32)], -1)

    wbig = jnp.concatenate([
        w_in[:, :Q_LORA + KV_LORA],
        place(w_kpe), place(_rotate_half_cols(w_kpe, QK_ROPE // 2)),
        w_in[:, c_b:c_b + WIDTH_B] * scale, w_in[:, c_b + WIDTH_B:c_c],
        w_in[:, c_c:c_c + WIDTH_C] * scale, w_in[:, c_c + WIDTH_C:],
    ], -1).astype(BF16)

    uq = w_uq.reshape(Q_LORA, HEADS_A, QK_NOPE + QK_ROPE)
    pad = jnp.zeros((Q_LORA, HEADS_A, LANES - QK_NOPE - QK_ROPE), F32)
    uq_pad = jnp.concatenate([uq, pad], -1)
    uq_rot = jnp.concatenate([jnp.zeros((Q_LORA, HEADS_A, QK_NOPE), F32),
                              _rotate_half_cols(uq[..., QK_NOPE:], QK_ROPE // 2), pad], -1)
    wuq = jnp.concatenate([uq_pad.reshape(Q_LORA, QK_A_PAD), uq_rot.reshape(Q_LORA, QK_A_PAD)], -1).astype(BF16)

    ukv = w_ukv.reshape(KV_LORA, HEADS_A, QK_NOPE + HEAD_DIM)
    uk_pad = jnp.concatenate([ukv[..., :QK_NOPE], jnp.zeros((KV_LORA, HEADS_A, LANES - QK_NOPE), F32)], -1)
    wukv = jnp.concatenate([uk_pad.reshape(KV_LORA, QK_A_PAD), ukv[..., QK_NOPE:].reshape(KV_LORA, WIDTH_A)],
                           -1).astype(BF16)
    return wbig, wuq, wukv


def kernel(x, g_mix, w_in, q_norm, w_uq, kv_norm, w_ukv, rpb, out_norm_a, out_norm_b, out_norm_c, w_out, g_mlp,
           w_mlp_in, w_mlp_out, g_final):
    b, s, d = x.shape
    depth = w_in.shape[0]
    t = b * s
    assert s % (2 * BLOCKS_PER_STEP * GRID_W) == 0 and s // GRID_W >= NA_ROWS
    tm_proj = min(512, s)
    tm_out = min(512, s)
    tm_mlp = min(1024, t)
    tq_a = min(256, s)
    tk_a = min(512, s)

    tabs = _rope_tables(s)
    na_bias = _na_bias_call(rpb).reshape(depth, HEADS_C, NA_ROWS, GRID_W, NA_ROWS * GRID_W)
    row = lambda a: a.reshape(1, -1)

    x2 = x.reshape(t, d)
    for l in range(depth):
        wbig, wuq, wukv = _layer_weights(w_in[l], w_uq[l], w_ukv[l])
        qa, ka, va, qb, kb, vb, qc, kc, vc, *dil_qkv = _proj_call(
            x2, row(g_mix[l]), wbig, row(q_norm[l]), wuq, row(kv_norm[l]), wukv, tabs, s, tm_proj)
        seq3 = lambda a: a.reshape(b, s, a.shape[-1])
        oa = _attn_a_call(qa, seq3(ka), va, tq_a, tk_a)
        ob1, l1, dilated = None, None, []
        for window, dil in DILATED_PAIRS:
            half = window // (2 * dil)
            if dil == 1:
                unit = lambda a: a.reshape(b, 1, s, WIDTH_B)
                o_i, lse_i = _attn_b_call(unit(qb), unit(kb), unit(vb), half, 128)
                ob1, l1 = o_i.reshape(t, WIDTH_B), lse_i.reshape(t, WIDTH_B)
            else:
                di = 3 * _DILATIONS.index(dil)
                dilated.append(_attn_b_call(*dil_qkv[di:di + 3], half, 128))
        oc = _attn_c_call(seq3(qc), seq3(kc), seq3(vc), na_bias[l]).reshape(t, WIDTH_C)
        x2 = _out_call(x2, oa, ob1, l1, dilated, oc, row(out_norm_a[l]), row(out_norm_b[l]), row(out_norm_c[l]),
                       w_out[l].astype(BF16), s, tm_out)
        x2 = _mlp_call(x2, row(g_mlp[l]), w_mlp_in[l].astype(BF16), w_mlp_out[l].astype(BF16), row(g_final),
                       l == depth - 1, tm_mlp, min(1024, w_mlp_in.shape[2]))
    return x2.reshape(b, s, d)
```

```python
import functools

import jax
import jax.numpy as jnp
from jax import lax
from jax.experimental import pallas as pl
from jax.experimental.pallas import tpu as pltpu

HEAD_DIM = 64
LANES = 128
BF16_ROWS = 16
HEADS_A = 6
HEADS_B = 6
HEADS_C = 4
Q_LORA = 256
KV_LORA = 128
QK_NOPE = 64
QK_ROPE = 32
DILATED_PAIRS = ((128, 1), (512, 4), (2048, 16))
_DILATIONS = tuple(dil for _, dil in DILATED_PAIRS if dil > 1)
GRID_W = 64
NA_ROWS = 8
NA_COLS = 16
ROPE_THETA = 10000.0
NORM_EPS = 1e-6
NEG_INF = -1e30
LOG2_E = 1.4426950408889634
PROJ_SUBTILES = 2
BLOCKS_PER_STEP = 4

WIDTH_A = HEADS_A * HEAD_DIM
WIDTH_B = HEADS_B * HEAD_DIM
WIDTH_C = HEADS_C * HEAD_DIM
QK_A_PAD = HEADS_A * LANES
RPB_PER_HEAD = (2 * NA_ROWS - 1) * (2 * NA_COLS - 1)

VMEM_LIMIT = 56 * 1024 * 1024

BF16 = jnp.bfloat16
F32 = jnp.float32


def _rms(x, g):
    return x * lax.rsqrt(jnp.mean(x * x, axis=-1, keepdims=True) + NORM_EPS) * g


def _dot(a, b):
    return jnp.dot(a, b, preferred_element_type=F32)


def _dot_nt(a, b):
    return lax.dot_general(a, b, (((1,), (1,)), ((), ())), preferred_element_type=F32)


def _lane_is_first_head(shape):
    return lax.broadcasted_iota(jnp.int32, shape, len(shape) - 1) < HEAD_DIM


def _stack_heads(q, first):
    return jnp.concatenate([_keep_head(q, first, 0), _keep_head(q, first, 1)], axis=0)


def _keep_head(q, first, j):
    zero = jnp.zeros_like(q)
    return jnp.where(first, q, zero) if j == 0 else jnp.where(first, zero, q)


_C_CQ = 0
_C_CKV = _C_CQ + Q_LORA
_C_KPE = _C_CKV + KV_LORA
_C_KPR = _C_KPE + LANES
_C_QB = _C_KPR + LANES
_C_KB = _C_QB + WIDTH_B
_C_VB = _C_KB + WIDTH_B
_C_QC = _C_VB + WIDTH_B
_C_KC = _C_QC + WIDTH_C
_C_VC = _C_KC + WIDTH_C
_C_END = _C_VC + WIDTH_C

_T_COSQ, _T_SINQ, _T_COSK, _T_SINK, _T_COSB, _T_SINB = range(6)


def _proj_kernel(x_ref, g_ref, wbig_ref, qn_ref, wuq_ref, kvn_ref, wukv_ref, tab_ref,
                 qa_ref, ka_ref, va_ref, qb_ref, kb_ref, vb_ref, qc_ref, kc_ref, vc_ref, *rest):
    dil_refs, stage_ref = rest[:-1], rest[-1]
    tm = x_ref.shape[0]
    hm = tm // PROJ_SUBTILES
    nblk = WIDTH_B // LANES
    first_half = (lax.broadcasted_iota(jnp.int32, (hm, LANES), 1) % HEAD_DIM) < HEAD_DIM // 2

    for sub in range(PROJ_SUBTILES):
        rs = slice(sub * hm, (sub + 1) * hm)

        def tab(i):
            return tab_ref[rs, i * LANES:(i + 1) * LANES]

        h = _rms(x_ref[rs, :], g_ref[...]).astype(BF16)
        proj = _dot(h, wbig_ref[...])

        cqn = _rms(proj[:, _C_CQ:_C_CKV], qn_ref[...]).astype(BF16)
        qa2 = _dot(cqn, wuq_ref[...])
        ckvn = _rms(proj[:, _C_CKV:_C_KPE], kvn_ref[...]).astype(BF16)
        kv2 = _dot(ckvn, wukv_ref[...])
        kpe = proj[:, _C_KPE:_C_KPR] * tab(_T_COSK) + proj[:, _C_KPR:_C_QB] * tab(_T_SINK)
        cosq, sinq = tab(_T_COSQ), tab(_T_SINQ)
        for hd in range(HEADS_A):
            sl = slice(hd * LANES, (hd + 1) * LANES)
            rot = slice(QK_A_PAD + hd * LANES, QK_A_PAD + (hd + 1) * LANES)
            qa_ref[0, sl, rs] = (qa2[:, sl] * cosq + qa2[:, rot] * sinq).T.astype(BF16)
            ka_ref[rs, sl] = (kv2[:, sl] + kpe).astype(BF16)
        for blk in range(WIDTH_A // LANES):
            sl = slice(blk * LANES, (blk + 1) * LANES)
            va_ref[0, sl, rs] = kv2[:, QK_A_PAD + blk * LANES:QK_A_PAD + (blk + 1) * LANES].T.astype(BF16)

        cosb, sinb = tab(_T_COSB), tab(_T_SINB)
        for ti, (src, dst) in enumerate(((_C_QB, qb_ref), (_C_KB, kb_ref), (_C_VB, vb_ref))):
            for blk in range(nblk):
                xb = proj[:, src + blk * LANES:src + (blk + 1) * LANES]
                if dst is not vb_ref:
                    swapped = jnp.where(first_half, pltpu.roll(xb, LANES - HEAD_DIM // 2, 1),
                                        pltpu.roll(xb, HEAD_DIM // 2, 1))
                    xb = xb * cosb + swapped * sinb
                    if dst is qb_ref:
                        xb = xb * LOG2_E
                dst[rs, blk * LANES:(blk + 1) * LANES] = xb.astype(BF16)
                stage_ref[ti * nblk + blk, rs, :] = xb
        for di, dil in enumerate(_DILATIONS):
            rows = hm // dil
            for ti in range(3):
                dref = dil_refs[di * 3 + ti]
                for r in range(dil):
                    for blk in range(nblk):
                        dref[0, r, sub * rows:(sub + 1) * rows, blk * LANES:(blk + 1) * LANES] = (
                            stage_ref[ti * nblk + blk, pl.ds(sub * hm + r, rows, stride=dil), :].astype(BF16))

        qc_ref[rs, :] = proj[:, _C_QC:_C_KC].astype(BF16)
        kc_ref[rs, :] = proj[:, _C_KC:_C_VC].astype(BF16)
        vc_ref[rs, :] = proj[:, _C_VC:_C_END].astype(BF16)


def _proj_call(x2, g, wbig, qn, wuq, kvn, wukv, tabs, seq, tm):
    t, d = x2.shape
    nseq = seq // tm
    row = lambda i: (i, 0)
    const = lambda i: (0, 0)
    widths = (QK_A_PAD, QK_A_PAD, WIDTH_A, WIDTH_B, WIDTH_B, WIDTH_B, WIDTH_C, WIDTH_C, WIDTH_C)
    out_specs = [pl.BlockSpec((tm, w), row) for w in widths]
    out_shape = [jax.ShapeDtypeStruct((t, w), BF16) for w in widths]
    for idx in (0, 2):
        out_specs[idx] = pl.BlockSpec((1, widths[idx], tm), lambda i: (i // nseq, 0, i % nseq))
        out_shape[idx] = jax.ShapeDtypeStruct((t // seq, widths[idx], seq), BF16)
    for dil in _DILATIONS:
        assert tm % (BF16_ROWS * dil * PROJ_SUBTILES) == 0
        for _ in range(3):
            out_specs.append(pl.BlockSpec((1, dil, tm // dil, WIDTH_B), lambda i: (i // nseq, 0, i % nseq, 0)))
            out_shape.append(jax.ShapeDtypeStruct((t // seq, dil, seq // dil, WIDTH_B), BF16))
    return pl.pallas_call(
        _proj_kernel,
        grid=(t // tm,),
        in_specs=[
            pl.BlockSpec((tm, d), row),
            pl.BlockSpec((1, d), const),
            pl.BlockSpec(wbig.shape, const),
            pl.BlockSpec((1, Q_LORA), const),
            pl.BlockSpec(wuq.shape, const),
            pl.BlockSpec((1, KV_LORA), const),
            pl.BlockSpec(wukv.shape, const),
            pl.BlockSpec((tm, tabs.shape[1]), lambda i: (i % nseq, 0)),
        ],
        out_specs=out_specs,
        out_shape=out_shape,
        scratch_shapes=[pltpu.VMEM((3 * WIDTH_B // LANES, tm, LANES), F32)],
        compiler_params=pltpu.CompilerParams(dimension_semantics=("parallel",),
                                             vmem_limit_bytes=VMEM_LIMIT),
    )(x2, g, wbig, qn, wuq, kvn, wukv, tabs)


def _attn_a_kernel(qt_ref, k_ref, vt_ref, o_ref, a0_ref, a1_ref, b0_ref, b1_ref, *, tq, tk):
    s = k_ref.shape[1]
    nk, nq = s // tk, s // tq
    st_a, st_b = (a0_ref, a1_ref), (b0_ref, b1_ref)
    ones = jnp.ones((BF16_ROWS, tk), BF16)

    def scores(qb, c, st_ref):
        q0 = pl.multiple_of(qb * tq, tq)
        ks = pl.multiple_of(c * tk, tk)
        for j in range(2):
            st_ref[j] = _dot(k_ref[0, pl.ds(ks, tk), j * LANES:(j + 1) * LANES],
                             qt_ref[0, j * LANES:(j + 1) * LANES, pl.ds(q0, tq)])

    def accumulate(c, st_ref, carry):
        ks = pl.multiple_of(c * tk, tk)
        stats = []
        for j in range(2):
            m = carry[j][0]
            st = st_ref[j]
            m_new = jnp.maximum(m, jnp.max(st, axis=0, keepdims=True))
            stats.append((m_new, jnp.exp2(m - m_new), jnp.exp2(st - m_new).astype(BF16)))
        new = []
        for j in range(2):
            m_new, alpha, pt = stats[j]
            vt = jnp.concatenate([vt_ref[0, j * HEAD_DIM:(j + 1) * HEAD_DIM, pl.ds(ks, tk)], ones], axis=0)
            new.append((m_new, alpha * carry[j][1] + _dot(vt, pt)))
        return tuple(new)

    def half_step(qb, c, src, dst, carry, wrap=False):
        for u in range(2):
            if wrap:
                scores(jnp.minimum(qb + 1, nq - 1), u, dst[u])
            else:
                scores(qb, c + 2 + u, dst[u])
            carry = accumulate(c + u, src[u], carry)
        return carry

    def q_block(qb, _):
        def body(i, carry):
            carry = half_step(qb, 4 * i, st_a, st_b, carry)
            return half_step(qb, 4 * i + 2, st_b, st_a, carry)

        init = tuple((jnp.full((1, tq), NEG_INF, F32), jnp.zeros((HEAD_DIM + BF16_ROWS, tq), F32))
                     for _ in range(2))
        carry = lax.fori_loop(0, nk // 4 - 1, body, init)
        carry = half_step(qb, nk - 4, st_a, st_b, carry)
        res = half_step(qb, nk - 2, st_b, st_a, carry, wrap=True)
        q0 = pl.multiple_of(qb * tq, tq)
        for j in range(2):
            acc = res[j][1]
            o_ref[0, j * HEAD_DIM:(j + 1) * HEAD_DIM, pl.ds(q0, tq)] = (
                acc[:HEAD_DIM] / acc[HEAD_DIM:HEAD_DIM + 1]).astype(o_ref.dtype)
        return 0

    scores(0, 0, st_a[0])
    scores(0, 1, st_a[1])
    lax.fori_loop(0, nq, q_block, 0)


def _attn_a_call(qt, ka, vt, tq, tk):
    b, s, _ = ka.shape
    pairs = HEADS_A // 2
    assert s % (4 * tk) == 0
    return pl.pallas_call(
        functools.partial(_attn_a_kernel, tq=tq, tk=tk),
        grid=(b, pairs),
        in_specs=[
            pl.BlockSpec((1, 2 * LANES, s), lambda bi, p: (bi, p, 0)),
            pl.BlockSpec((1, s, 2 * LANES), lambda bi, p: (bi, 0, p)),
            pl.BlockSpec((1, LANES, s), lambda bi, p: (bi, p, 0)),
        ],
        out_specs=pl.BlockSpec((1, LANES, s), lambda bi, p: (bi, p, 0)),
        out_shape=jax.ShapeDtypeStruct((b, WIDTH_A, s), BF16),
        scratch_shapes=[pltpu.VMEM((2, tk, tq), F32) for _ in range(4)],
        compiler_params=pltpu.CompilerParams(dimension_semantics=("parallel", "parallel"),
                                             vmem_limit_bytes=VMEM_LIMIT),
    )(qt, ka, vt)


def _attn_b_kernel(q_ref, k_ref, v_ref, o_ref, lse_ref, band_ref, sa_ref, sb_ref, *, tq, half, n):
    total = q_ref.shape[0]
    kw = min(tq + 2 * half, n)
    first = _lane_is_first_head((tq, LANES))

    group = min(BLOCKS_PER_STEP, total // tq)
    diff = (lax.broadcasted_iota(jnp.int32, (tq, kw), 0) - lax.broadcasted_iota(jnp.int32, (tq, kw), 1))
    for var in range(band_ref.shape[0]):
        band_ref[var] = jnp.where(jnp.abs(diff + var * half) <= half, 0.0, NEG_INF)

    n_groups = total // (tq * group)

    def window(g, u):
        q0 = pl.multiple_of((g * group + u) * tq, tq)
        seg0 = (q0 // n) * n
        return q0, pl.multiple_of(jnp.clip(q0 - half, seg0, seg0 + n - kw), half)

    def issue(g, st_ref):
        g = jnp.minimum(g, n_groups - 1)
        for u in range(group):
            q0, ks = window(g, u)
            st_ref[u] = _dot_nt(_stack_heads(q_ref[pl.ds(q0, tq), :], first), k_ref[pl.ds(ks, kw), :])

    def consume(g, st_ref):
        probs = []
        for u in range(group):
            q0, ks = window(g, u)
            band = band_ref[(q0 - ks) // half]
            row = []
            for j in range(2):
                s = st_ref[u, j * tq:(j + 1) * tq, :] + band
                m = jnp.max(s, axis=-1, keepdims=True)
                p = jnp.exp2(s - m)
                den = jnp.sum(p, axis=-1, keepdims=True)
                row.append((p.astype(BF16), den, m + jnp.log2(den)))
            probs.append(row)
        for u, row in enumerate(probs):
            q0, ks = window(g, u)
            o2 = _dot(jnp.concatenate([row[0][0], row[1][0]], axis=0), v_ref[pl.ds(ks, kw), :])
            outs = [o2[j * tq:(j + 1) * tq] / row[j][1] for j in range(2)]
            lses = [jnp.broadcast_to(lse, (tq, LANES)) for (_, _, lse) in row]
            o_ref[pl.ds(q0, tq), :] = jnp.where(first, outs[0], outs[1]).astype(o_ref.dtype)
            lse_ref[pl.ds(q0, tq), :] = jnp.where(first, lses[0], lses[1])

    def body(it, carry):
        issue(2 * it + 1, sb_ref)
        consume(2 * it, sa_ref)
        issue(2 * it + 2, sa_ref)
        consume(2 * it + 1, sb_ref)
        return carry

    issue(0, sa_ref)
    lax.fori_loop(0, n_groups // 2, body, 0)


def _attn_b_call(q, k, v, half, tq):
    b, dil, n, w = q.shape
    pairs = w // LANES
    tq = min(tq, n)
    kw = min(tq + 2 * half, n)
    group = min(BLOCKS_PER_STEP, dil * n // tq)
    assert n % tq == 0 and (kw - tq) % half == 0 and (dil * n) % (2 * group * tq) == 0
    flat = lambda a: a.reshape(b, dil * n, w)
    spec = pl.BlockSpec((None, dil * n, LANES), lambda bi, p: (bi, 0, p))
    o, lse = pl.pallas_call(
        functools.partial(_attn_b_kernel, tq=tq, half=half, n=n),
        grid=(b, pairs),
        in_specs=[spec, spec, spec],
        out_specs=[spec, spec],
        out_shape=[jax.ShapeDtypeStruct((b, dil * n, w), BF16), jax.ShapeDtypeStruct((b, dil * n, w), F32)],
        scratch_shapes=[pltpu.VMEM(((kw - tq) // half + 1, tq, kw), F32),
                        pltpu.VMEM((group, 2 * tq, kw), F32), pltpu.VMEM((group, 2 * tq, kw), F32)],
        compiler_params=pltpu.CompilerParams(dimension_semantics=("parallel", "parallel"),
                                             vmem_limit_bytes=VMEM_LIMIT),
    )(flat(q), flat(k), flat(v))
    return o.reshape(q.shape), lse.reshape(q.shape)


def _na_bias_kernel(rpb_ref, o_ref):
    base = pl.program_id(0) * RPB_PER_HEAD
    shape = (GRID_W, LANES)
    lane = lax.broadcasted_iota(jnp.int32, shape, 1)
    p = lax.broadcasted_iota(jnp.int32, shape, 0)
    c = lane % GRID_W
    upper = lane >= GRID_W
    c_start = jnp.clip(p - NA_COLS // 2, 0, GRID_W - NA_COLS)
    col_ok = (c >= c_start) & (c < c_start + NA_COLS)
    dc = c - p + (NA_COLS - 1)
    n_dc = 2 * NA_COLS - 1
    for v in range(NA_ROWS):
        for m in range(NA_ROWS * GRID_W // LANES):
            a_lo = 2 * m - v + (NA_ROWS - 1)
            acc = jnp.full(shape, NEG_INF, F32)
            for b in range(n_dc):
                val = jnp.where(upper, rpb_ref[base + (a_lo + 1) * n_dc + b], rpb_ref[base + a_lo * n_dc + b])
                acc = jnp.where(dc == b, val, acc)
            o_ref[0, v, :, m * LANES:(m + 1) * LANES] = jnp.where(col_ok, acc, NEG_INF)


def _na_bias_call(rpb):
    nh = rpb.shape[0] * rpb.shape[1]
    return pl.pallas_call(
        _na_bias_kernel,
        grid=(nh,),
        in_specs=[pl.BlockSpec(memory_space=pltpu.SMEM)],
        out_specs=pl.BlockSpec((1, NA_ROWS, GRID_W, NA_ROWS * GRID_W), lambda g: (g, 0, 0, 0)),
        out_shape=jax.ShapeDtypeStruct((nh, NA_ROWS, GRID_W, NA_ROWS * GRID_W), F32),
    )(rpb.reshape(-1))


def _attn_c_kernel(q_ref, k_ref, v_ref, bias_ref, o_ref, sa_ref, sb_ref):
    rows = q_ref.shape[1] // GRID_W
    win = NA_ROWS * GRID_W
    first = _lane_is_first_head((GRID_W, LANES))
    group = sa_ref.shape[0]
    n_groups = rows // group

    def window(g, u):
        r = g * group + u
        r_start = jnp.clip(r - NA_ROWS // 2, 0, rows - NA_ROWS)
        return pl.multiple_of(r * GRID_W, GRID_W), pl.multiple_of(r_start * GRID_W, GRID_W), r - r_start

    def issue(g, st_ref):
        g = jnp.minimum(g, n_groups - 1)
        for u in range(group):
            q0, ks, _ = window(g, u)
            st_ref[u] = _dot_nt(_stack_heads(q_ref[0, pl.ds(q0, GRID_W), :], first), k_ref[0, pl.ds(ks, win), :])

    def consume(g, st_ref):
        probs = []
        for u in range(group):
            variant = window(g, u)[2]
            row = []
            for j in range(2):
                s = st_ref[u, j * GRID_W:(j + 1) * GRID_W, :] + bias_ref[j, variant]
                m = jnp.max(s, axis=-1, keepdims=True)
                p = jnp.exp(s - m)
                row.append((p.astype(BF16), jnp.sum(p, axis=-1, keepdims=True)))
            probs.append(row)
        for u, row in enumerate(probs):
            q0, ks, _ = window(g, u)
            o2 = _dot(jnp.concatenate([row[0][0], row[1][0]], axis=0), v_ref[0, pl.ds(ks, win), :])
            outs = [o2[j * GRID_W:(j + 1) * GRID_W] / row[j][1] for j in range(2)]
            o_ref[0, pl.ds(q0, GRID_W), :] = jnp.where(first, outs[0], outs[1]).astype(o_ref.dtype)

    def body(it, carry):
        issue(2 * it + 1, sb_ref)
        consume(2 * it, sa_ref)
        issue(2 * it + 2, sa_ref)
        consume(2 * it + 1, sb_ref)
        return carry

    issue(0, sa_ref)
    lax.fori_loop(0, n_groups // 2, body, 0)


def _attn_c_call(q, k, v, bias):
    b, s, w = q.shape
    pairs = w // LANES
    spec = pl.BlockSpec((1, s, LANES), lambda bi, p: (bi, 0, p))
    return pl.pallas_call(
        _attn_c_kernel,
        grid=(b, pairs),
        in_specs=[spec, spec, spec,
                  pl.BlockSpec((2,) + bias.shape[1:], lambda bi, p: (p, 0, 0, 0))],
        out_specs=spec,
        out_shape=jax.ShapeDtypeStruct((b, s, w), BF16),
        scratch_shapes=[pltpu.VMEM((BLOCKS_PER_STEP, 2 * GRID_W, NA_ROWS * GRID_W), F32) for _ in range(2)],
        compiler_params=pltpu.CompilerParams(dimension_semantics=("parallel", "parallel"),
                                             vmem_limit_bytes=VMEM_LIMIT),
    )(q, k, v, bias)


def _out_kernel(x_ref, oa_ref, ob1_ref, l1_ref, *rest):
    nd = len(_DILATIONS)
    dil_refs = rest[:2 * nd]
    oc_ref, ga_ref, gb_ref, gc_ref, w_ref, o_ref, stage_ref = rest[2 * nd:]
    tm = x_ref.shape[0]
    nblk = WIDTH_B // LANES
    oa = jnp.concatenate([oa_ref[0, blk * LANES:(blk + 1) * LANES, :].astype(F32).T
                          for blk in range(WIDTH_A // LANES)], axis=-1)
    na = _rms(oa, ga_ref[...]).astype(BF16)
    outs, lses = [ob1_ref[...].astype(F32)], [l1_ref[...]]
    for di, dil in enumerate(_DILATIONS):
        rows = tm // dil
        for which, acc in ((0, outs), (1, lses)):
            src = dil_refs[2 * di + which]
            slot = (2 * di + which) * nblk
            for r in range(dil):
                for blk in range(nblk):
                    stage_ref[slot + blk, pl.ds(r, rows, stride=dil), :] = (
                        src[0, r, :, blk * LANES:(blk + 1) * LANES].astype(F32))
            acc.append(jnp.concatenate([stage_ref[slot + blk] for blk in range(nblk)], axis=-1))
    lmax = functools.reduce(jnp.maximum, lses)
    es = [jnp.exp2(l - lmax) for l in lses]
    ob = sum(e * o for e, o in zip(es, outs)) / sum(es)
    nb = _rms(ob, gb_ref[...]).astype(BF16)
    nc = _rms(oc_ref[...].astype(F32), gc_ref[...]).astype(BF16)
    acc = _dot(na, w_ref[0:WIDTH_A, :])
    acc += _dot(nb, w_ref[WIDTH_A:WIDTH_A + WIDTH_B, :])
    acc += _dot(nc, w_ref[WIDTH_A + WIDTH_B:, :])
    o_ref[...] = x_ref[...] + acc


def _out_call(x2, oa, ob1, l1, dilated, oc, ga, gb, gc, w, seq, tm):
    t, d = x2.shape
    nseq = seq // tm
    row = lambda i: (i, 0)
    const = lambda i: (0, 0)
    rs = lambda width: pl.BlockSpec((tm, width), row)
    cs = lambda width: pl.BlockSpec((1, width), const)
    dil_specs, dil_args = [], []
    for dil, pair in zip(_DILATIONS, dilated):
        for a in pair:
            dil_specs.append(pl.BlockSpec((1, dil, tm // dil, WIDTH_B), lambda i: (i // nseq, 0, i % nseq, 0)))
            dil_args.append(a)
    return pl.pallas_call(
        _out_kernel,
        grid=(t // tm,),
        in_specs=[rs(d), pl.BlockSpec((1, WIDTH_A, tm), lambda i: (i // nseq, 0, i % nseq)),
                  rs(WIDTH_B), rs(WIDTH_B), *dil_specs,
                  rs(WIDTH_C), cs(WIDTH_A), cs(WIDTH_B), cs(WIDTH_C), pl.BlockSpec(w.shape, const)],
        out_specs=rs(d),
        out_shape=jax.ShapeDtypeStruct((t, d), F32),
        scratch_shapes=[pltpu.VMEM((2 * len(_DILATIONS) * WIDTH_B // LANES, tm, LANES), F32)],
        compiler_params=pltpu.CompilerParams(dimension_semantics=("parallel",),
                                             vmem_limit_bytes=VMEM_LIMIT),
    )(x2, oa, ob1, l1, *dil_args, oc, ga, gb, gc, w)


def _mlp_kernel(x_ref, g_ref, w1_ref, w2_ref, gf_ref, o_ref, h_ref, acc_ref, *, final_norm):
    f = pl.program_id(1)

    @pl.when(f == 0)
    def _():
        h_ref[...] = _rms(x_ref[...], g_ref[...]).astype(BF16)
        acc_ref[...] = jnp.zeros_like(acc_ref)

    u = jnp.maximum(_dot(h_ref[...], w1_ref[...]), 0.0)
    acc_ref[...] += _dot((u * u).astype(BF16), w2_ref[...])

    @pl.when(f == pl.num_programs(1) - 1)
    def _():
        y = x_ref[...] + acc_ref[...]
        if final_norm:
            y = _rms(y, gf_ref[...])
        o_ref[...] = y


def _mlp_call(x2, g, w1, w2, gf, final_norm, tm, tf):
    t, d = x2.shape
    dff = w1.shape[1]
    return pl.pallas_call(
        functools.partial(_mlp_kernel, final_norm=final_norm),
        grid=(t // tm, dff // tf),
        in_specs=[
            pl.BlockSpec((tm, d), lambda i, f: (i, 0)),
            pl.BlockSpec((1, d), lambda i, f: (0, 0)),
            pl.BlockSpec((d, tf), lambda i, f: (0, f)),
            pl.BlockSpec((tf, d), lambda i, f: (f, 0)),
            pl.BlockSpec((1, d), lambda i, f: (0, 0)),
        ],
        out_specs=pl.BlockSpec((tm, d), lambda i, f: (i, 0)),
        out_shape=jax.ShapeDtypeStruct((t, d), F32),
        scratch_shapes=[pltpu.VMEM((tm, d), BF16), pltpu.VMEM((tm, d), F32)],
        compiler_params=pltpu.CompilerParams(dimension_semantics=("parallel", "arbitrary"),
                                             vmem_limit_bytes=VMEM_LIMIT),
    )(x2, g, w1, w2, gf)


def _rotate_half_cols(w, half):
    return jnp.concatenate([-w[..., half:], w[..., :half]], axis=-1)


def _rope_tables(seq):
    pos = jnp.arange(seq, dtype=F32)

    def cos_sin(half):
        inv_freq = ROPE_THETA ** (-jnp.arange(half, dtype=F32) / half)
        ang = pos[:, None] * inv_freq[None, :]
        return jnp.cos(ang), jnp.sin(ang)

    ca, sa = cos_sin(QK_ROPE // 2)
    ca2, sa2 = jnp.concatenate([ca, ca], -1), jnp.concatenate([sa, sa], -1)
    ones = jnp.ones((seq, QK_NOPE), F32)
    zeros = jnp.zeros((seq, QK_NOPE), F32)
    tail = jnp.zeros((seq, LANES - QK_NOPE - QK_ROPE), F32)
    scale_a = (QK_NOPE + QK_ROPE) ** -0.5 * LOG2_E
    cosq = jnp.concatenate([ones, ca2, tail], -1) * scale_a
    sinq = jnp.concatenate([zeros, sa2, tail], -1) * scale_a
    cosk = jnp.concatenate([zeros, ca2, tail], -1)
    sink = jnp.concatenate([zeros, sa2, tail], -1)
    cb, sb = cos_sin(HEAD_DIM // 2)
    cosb = jnp.concatenate([cb, cb, cb, cb], -1)
    sinb = jnp.concatenate([-sb, sb, -sb, sb], -1)
    return jnp.concatenate([cosq, sinq, cosk, sink, cosb, sinb], -1)


def _layer_weights(w_in, w_uq, w_ukv):
    d = w_in.shape[0]
    scale = HEAD_DIM ** -0.5
    c_b = Q_LORA + KV_LORA + QK_ROPE
    c_c = c_b + 3 * WIDTH_B
    w_kpe = w_in[:, Q_LORA + KV_LORA:c_b]

    def place(w):
        return jnp.concatenate([jnp.zeros((d, QK_NOPE), F32), w, jnp.zeros((d, LANES - QK_NOPE - QK_ROPE), F32)], -1)

    wbig = jnp.concatenate([
        w_in[:, :Q_LORA + KV_LORA],
        place(w_kpe), place(_rotate_half_cols(w_kpe, QK_ROPE // 2)),
        w_in[:, c_b:c_b + WIDTH_B] * scale, w_in[:, c_b + WIDTH_B:c_c],
        w_in[:, c_c:c_c + WIDTH_C] * scale, w_in[:, c_c + WIDTH_C:],
    ], -1).astype(BF16)

    uq = w_uq.reshape(Q_LORA, HEADS_A, QK_NOPE + QK_ROPE)
    pad = jnp.zeros((Q_LORA, HEADS_A, LANES - QK_NOPE - QK_ROPE), F32)
    uq_pad = jnp.concatenate([uq, pad], -1)
    uq_rot = jnp.concatenate([jnp.zeros((Q_LORA, HEADS_A, QK_NOPE), F32),
                              _rotate_half_cols(uq[..., QK_NOPE:], QK_ROPE // 2), pad], -1)
    wuq = jnp.concatenate([uq_pad.reshape(Q_LORA, QK_A_PAD), uq_rot.reshape(Q_LORA, QK_A_PAD)], -1).astype(BF16)

    ukv = w_ukv.reshape(KV_LORA, HEADS_A, QK_NOPE + HEAD_DIM)
    uk_pad = jnp.concatenate([ukv[..., :QK_NOPE], jnp.zeros((KV_LORA, HEADS_A, LANES - QK_NOPE), F32)], -1)
    wukv = jnp.concatenate([uk_pad.reshape(KV_LORA, QK_A_PAD), ukv[..., QK_NOPE:].reshape(KV_LORA, WIDTH_A)],
                           -1).astype(BF16)
    return wbig, wuq, wukv


def kernel(x, g_mix, w_in, q_norm, w_uq, kv_norm, w_ukv, rpb, out_norm_a, out_norm_b, out_norm_c, w_out, g_mlp,
           w_mlp_in, w_mlp_out, g_final):
    b, s, d = x.shape
    depth = w_in.shape[0]
    t = b * s
    assert s % (2 * BLOCKS_PER_STEP * GRID_W) == 0 and s // GRID_W >= NA_ROWS
    tm_proj = min(512, s)
    tm_out = min(512, s)
    tm_mlp = min(1024, t)
    tq_a = min(256, s)
    tk_a = min(512, s)

    tabs = _rope_tables(s)
    na_bias = _na_bias_call(rpb).reshape(depth, HEADS_C, NA_ROWS, GRID_W, NA_ROWS * GRID_W)
    row = lambda a: a.reshape(1, -1)

    x2 = x.reshape(t, d)
    for l in range(depth):
        wbig, wuq, wukv = _layer_weights(w_in[l], w_uq[l], w_ukv[l])
        qa, ka, va, qb, kb, vb, qc, kc, vc, *dil_qkv = _proj_call(
            x2, row(g_mix[l]), wbig, row(q_norm[l]), wuq, row(kv_norm[l]), wukv, tabs, s, tm_proj)
        seq3 = lambda a: a.reshape(b, s, a.shape[-1])
        oa = _attn_a_call(qa, seq3(ka), va, tq_a, tk_a)
        ob1, l1, dilated = None, None, []
        for window, dil in DILATED_PAIRS:
            half = window // (2 * dil)
            if dil == 1:
                unit = lambda a: a.reshape(b, 1, s, WIDTH_B)
                o_i, lse_i = _attn_b_call(unit(qb), unit(kb), unit(vb), half, 128)
                ob1, l1 = o_i.reshape(t, WIDTH_B), lse_i.reshape(t, WIDTH_B)
            else:
                di = 3 * _DILATIONS.index(dil)
                dilated.append(_attn_b_call(*dil_qkv[di:di + 3], half, 128))
        oc = _attn_c_call(seq3(qc), seq3(kc), seq3(vc), na_bias[l]).reshape(t, WIDTH_C)
        x2 = _out_call(x2, oa, ob1, l1, dilated, oc, row(out_norm_a[l]), row(out_norm_b[l]), row(out_norm_c[l]),
                       w_out[l].astype(BF16), s, tm_out)
        x2 = _mlp_call(x2, row(g_mlp[l]), w_mlp_in[l].astype(BF16), w_mlp_out[l].astype(BF16), row(g_final),
                       l == depth - 1, tm_mlp, min(1024, w_mlp_in.shape[2]))
    return x2.reshape(b, s, d)
```

```python
import functools

import jax
import jax.numpy as jnp
from jax import lax
from jax.experimental import pallas as pl
from jax.experimental.pallas import tpu as pltpu

HEAD_DIM = 64
LANES = 128
BF16_ROWS = 16
HEADS_A = 6
HEADS_B = 6
HEADS_C = 4
Q_LORA = 256
KV_LORA = 128
QK_NOPE = 64
QK_ROPE = 32
DILATED_PAIRS = ((128, 1), (512, 4), (2048, 16))
_DILATIONS = tuple(dil for _, dil in DILATED_PAIRS if dil > 1)
GRID_W = 64
NA_ROWS = 8
NA_COLS = 16
ROPE_THETA = 10000.0
NORM_EPS = 1e-6
NEG_INF = -1e30
LOG2_E = 1.4426950408889634
PROJ_SUBTILES = 2
MLP_SUBTILES = 2
BLOCKS_PER_STEP = 4

WIDTH_A = HEADS_A * HEAD_DIM
WIDTH_B = HEADS_B * HEAD_DIM
WIDTH_C = HEADS_C * HEAD_DIM
QK_A_PAD = HEADS_A * LANES
RPB_PER_HEAD = (2 * NA_ROWS - 1) * (2 * NA_COLS - 1)

VMEM_LIMIT = 56 * 1024 * 1024

BF16 = jnp.bfloat16
F32 = jnp.float32


def _rms(x, g):
    return x * lax.rsqrt(jnp.mean(x * x, axis=-1, keepdims=True) + NORM_EPS) * g


def _dot(a, b):
    return jnp.dot(a, b, preferred_element_type=F32)


def _dot_nt(a, b):
    return lax.dot_general(a, b, (((1,), (1,)), ((), ())), preferred_element_type=F32)


def _lane_is_first_head(shape):
    return lax.broadcasted_iota(jnp.int32, shape, len(shape) - 1) < HEAD_DIM


def _stack_heads(q, first):
    return jnp.concatenate([_keep_head(q, first, 0), _keep_head(q, first, 1)], axis=0)


def _keep_head(q, first, j):
    zero = jnp.zeros_like(q)
    return jnp.where(first, q, zero) if j == 0 else jnp.where(first, zero, q)


_C_CQ = 0
_C_CKV = _C_CQ + Q_LORA
_C_KPE = _C_CKV + KV_LORA
_C_KPR = _C_KPE + LANES
_C_QB = _C_KPR + LANES
_C_KB = _C_QB + WIDTH_B
_C_VB = _C_KB + WIDTH_B
_C_QC = _C_VB + WIDTH_B
_C_KC = _C_QC + WIDTH_C
_C_VC = _C_KC + WIDTH_C
_C_END = _C_VC + WIDTH_C

_T_COSQ, _T_SINQ, _T_COSK, _T_SINK, _T_COSB, _T_SINB = range(6)


def _proj_kernel(x_ref, g_ref, wbig_ref, qn_ref, wuq_ref, kvn_ref, wukv_ref, tab_ref,
                 qa_ref, ka_ref, va_ref, qb_ref, kb_ref, vb_ref, qc_ref, kc_ref, vc_ref, *rest):
    dil_refs, stage_ref = rest[:-1], rest[-1]
    tm = x_ref.shape[0]
    hm = tm // PROJ_SUBTILES
    nblk = WIDTH_B // LANES
    first_half = (lax.broadcasted_iota(jnp.int32, (hm, LANES), 1) % HEAD_DIM) < HEAD_DIM // 2

    for sub in range(PROJ_SUBTILES):
        rs = slice(sub * hm, (sub + 1) * hm)

        def tab(i):
            return tab_ref[rs, i * LANES:(i + 1) * LANES]

        h = _rms(x_ref[rs, :], g_ref[...]).astype(BF16)
        proj = _dot(h, wbig_ref[...])

        cqn = _rms(proj[:, _C_CQ:_C_CKV], qn_ref[...]).astype(BF16)
        qa2 = _dot(cqn, wuq_ref[...])
        ckvn = _rms(proj[:, _C_CKV:_C_KPE], kvn_ref[...]).astype(BF16)
        kv2 = _dot(ckvn, wukv_ref[...])
        kpe = proj[:, _C_KPE:_C_KPR] * tab(_T_COSK) + proj[:, _C_KPR:_C_QB] * tab(_T_SINK)
        cosq, sinq = tab(_T_COSQ), tab(_T_SINQ)
        for hd in range(HEADS_A):
            sl = slice(hd * LANES, (hd + 1) * LANES)
            rot = slice(QK_A_PAD + hd * LANES, QK_A_PAD + (hd + 1) * LANES)
            qa_ref[0, sl, rs] = (qa2[:, sl] * cosq + qa2[:, rot] * sinq).T.astype(BF16)
            ka_ref[rs, sl] = (kv2[:, sl] + kpe).astype(BF16)
        for blk in range(WIDTH_A // LANES):
            sl = slice(blk * LANES, (blk + 1) * LANES)
            va_ref[0, sl, rs] = kv2[:, QK_A_PAD + blk * LANES:QK_A_PAD + (blk + 1) * LANES].T.astype(BF16)

        cosb, sinb = tab(_T_COSB), tab(_T_SINB)
        for ti, (src, dst) in enumerate(((_C_QB, qb_ref), (_C_KB, kb_ref), (_C_VB, vb_ref))):
            for blk in range(nblk):
                xb = proj[:, src + blk * LANES:src + (blk + 1) * LANES]
                if dst is not vb_ref:
                    swapped = jnp.where(first_half, pltpu.roll(xb, LANES - HEAD_DIM // 2, 1),
                                        pltpu.roll(xb, HEAD_DIM // 2, 1))
                    xb = xb * cosb + swapped * sinb
                    if dst is qb_ref:
                        xb = xb * LOG2_E
                dst[rs, blk * LANES:(blk + 1) * LANES] = xb.astype(BF16)
                stage_ref[ti * nblk + blk, rs, :] = xb
        for di, dil in enumerate(_DILATIONS):
            rows = hm // dil
            for ti in range(3):
                dref = dil_refs[di * 3 + ti]
                for r in range(dil):
                    for blk in range(nblk):
                        dref[0, r, sub * rows:(sub + 1) * rows, blk * LANES:(blk + 1) * LANES] = (
                            stage_ref[ti * nblk + blk, pl.ds(sub * hm + r, rows, stride=dil), :].astype(BF16))

        qc_ref[rs, :] = proj[:, _C_QC:_C_KC].astype(BF16)
        kc_ref[rs, :] = proj[:, _C_KC:_C_VC].astype(BF16)
        vc_ref[rs, :] = proj[:, _C_VC:_C_END].astype(BF16)


def _proj_call(x2, g, wbig, qn, wuq, kvn, wukv, tabs, seq, tm):
    t, d = x2.shape
    nseq = seq // tm
    row = lambda i: (i, 0)
    const = lambda i: (0, 0)
    widths = (QK_A_PAD, QK_A_PAD, WIDTH_A, WIDTH_B, WIDTH_B, WIDTH_B, WIDTH_C, WIDTH_C, WIDTH_C)
    out_specs = [pl.BlockSpec((tm, w), row) for w in widths]
    out_shape = [jax.ShapeDtypeStruct((t, w), BF16) for w in widths]
    for idx in (0, 2):
        out_specs[idx] = pl.BlockSpec((1, widths[idx], tm), lambda i: (i // nseq, 0, i % nseq))
        out_shape[idx] = jax.ShapeDtypeStruct((t // seq, widths[idx], seq), BF16)
    for dil in _DILATIONS:
        assert tm % (BF16_ROWS * dil * PROJ_SUBTILES) == 0
        for _ in range(3):
            out_specs.append(pl.BlockSpec((1, dil, tm // dil, WIDTH_B), lambda i: (i // nseq, 0, i % nseq, 0)))
            out_shape.append(jax.ShapeDtypeStruct((t // seq, dil, seq // dil, WIDTH_B), BF16))
    return pl.pallas_call(
        _proj_kernel,
        grid=(t // tm,),
        in_specs=[
            pl.BlockSpec((tm, d), row),
            pl.BlockSpec((1, d), const),
            pl.BlockSpec(wbig.shape, const),
            pl.BlockSpec((1, Q_LORA), const),
            pl.BlockSpec(wuq.shape, const),
            pl.BlockSpec((1, KV_LORA), const),
            pl.BlockSpec(wukv.shape, const),
            pl.BlockSpec((tm, tabs.shape[1]), lambda i: (i % nseq, 0)),
        ],
        out_specs=out_specs,
        out_shape=out_shape,
        scratch_shapes=[pltpu.VMEM((3 * WIDTH_B // LANES, tm, LANES), F32)],
        compiler_params=pltpu.CompilerParams(dimension_semantics=("parallel",),
                                             vmem_limit_bytes=VMEM_LIMIT),
    )(x2, g, wbig, qn, wuq, kvn, wukv, tabs)


def _attn_a_kernel(qt_ref, k_ref, vt_ref, o_ref, a0_ref, a1_ref, b0_ref, b1_ref, *, tq, tk):
    s = k_ref.shape[1]
    nk, nq = s // tk, s // tq
    st_a, st_b = (a0_ref, a1_ref), (b0_ref, b1_ref)
    ones = jnp.ones((BF16_ROWS, tk), BF16)

    def scores(qb, c, st_ref):
        q0 = pl.multiple_of(qb * tq, tq)
        ks = pl.multiple_of(c * tk, tk)
        for j in range(2):
            st_ref[j] = _dot(k_ref[0, pl.ds(ks, tk), j * LANES:(j + 1) * LANES],
                             qt_ref[0, j * LANES:(j + 1) * LANES, pl.ds(q0, tq)])

    def accumulate(c, st_ref, carry):
        ks = pl.multiple_of(c * tk, tk)
        stats = []
        for j in range(2):
            m = carry[j][0]
            st = st_ref[j]
            m_new = jnp.maximum(m, jnp.max(st, axis=0, keepdims=True))
            stats.append((m_new, jnp.exp2(m - m_new), jnp.exp2(st - m_new).astype(BF16)))
        new = []
        for j in range(2):
            m_new, alpha, pt = stats[j]
            vt = jnp.concatenate([vt_ref[0, j * HEAD_DIM:(j + 1) * HEAD_DIM, pl.ds(ks, tk)], ones], axis=0)
            new.append((m_new, alpha * carry[j][1] + _dot(vt, pt)))
        return tuple(new)

    def half_step(qb, c, src, dst, carry, wrap=False):
        for u in range(2):
            if wrap:
                scores(jnp.minimum(qb + 1, nq - 1), u, dst[u])
            else:
                scores(qb, c + 2 + u, dst[u])
            carry = accumulate(c + u, src[u], carry)
        return carry

    def q_block(qb, _):
        def body(i, carry):
            carry = half_step(qb, 4 * i, st_a, st_b, carry)
            return half_step(qb, 4 * i + 2, st_b, st_a, carry)

        init = tuple((jnp.full((1, tq), NEG_INF, F32), jnp.zeros((HEAD_DIM + BF16_ROWS, tq), F32))
                     for _ in range(2))
        carry = lax.fori_loop(0, nk // 4 - 1, body, init)
        carry = half_step(qb, nk - 4, st_a, st_b, carry)
        res = half_step(qb, nk - 2, st_b, st_a, carry, wrap=True)
        q0 = pl.multiple_of(qb * tq, tq)
        for j in range(2):
            acc = res[j][1]
            o_ref[0, j * HEAD_DIM:(j + 1) * HEAD_DIM, pl.ds(q0, tq)] = (
                acc[:HEAD_DIM] / acc[HEAD_DIM:HEAD_DIM + 1]).astype(o_ref.dtype)
        return 0

    scores(0, 0, st_a[0])
    scores(0, 1, st_a[1])
    lax.fori_loop(0, nq, q_block, 0)


def _attn_a_call(qt, ka, vt, tq, tk):
    b, s, _ = ka.shape
    pairs = HEADS_A // 2
    assert s % (4 * tk) == 0
    return pl.pallas_call(
        functools.partial(_attn_a_kernel, tq=tq, tk=tk),
        grid=(b, pairs),
        in_specs=[
            pl.BlockSpec((1, 2 * LANES, s), lambda bi, p: (bi, p, 0)),
            pl.BlockSpec((1, s, 2 * LANES), lambda bi, p: (bi, 0, p)),
            pl.BlockSpec((1, LANES, s), lambda bi, p: (bi, p, 0)),
        ],
        out_specs=pl.BlockSpec((1, LANES, s), lambda bi, p: (bi, p, 0)),
        out_shape=jax.ShapeDtypeStruct((b, WIDTH_A, s), BF16),
        scratch_shapes=[pltpu.VMEM((2, tk, tq), F32) for _ in range(4)],
        compiler_params=pltpu.CompilerParams(dimension_semantics=("parallel", "parallel"),
                                             vmem_limit_bytes=VMEM_LIMIT),
    )(qt, ka, vt)


def _attn_b_kernel(q_ref, k_ref, v_ref, o_ref, lse_ref, band_ref, sa_ref, sb_ref, *, tq, half, n):
    total = q_ref.shape[0]
    kw = min(tq + 2 * half, n)
    first = _lane_is_first_head((tq, LANES))

    group = min(BLOCKS_PER_STEP, total // tq)
    diff = (lax.broadcasted_iota(jnp.int32, (tq, kw), 0) - lax.broadcasted_iota(jnp.int32, (tq, kw), 1))
    for var in range(band_ref.shape[0]):
        band_ref[var] = jnp.where(jnp.abs(diff + var * half) <= half, 0.0, NEG_INF)

    n_groups = total // (tq * group)

    def window(g, u):
        q0 = pl.multiple_of((g * group + u) * tq, tq)
        seg0 = (q0 // n) * n
        return q0, pl.multiple_of(jnp.clip(q0 - half, seg0, seg0 + n - kw), half)

    def issue(g, st_ref):
        g = jnp.minimum(g, n_groups - 1)
        for u in range(group):
            q0, ks = window(g, u)
            st_ref[u] = _dot_nt(_stack_heads(q_ref[pl.ds(q0, tq), :], first), k_ref[pl.ds(ks, kw), :])

    def consume(g, st_ref):
        probs = []
        for u in range(group):
            q0, ks = window(g, u)
            band = band_ref[(q0 - ks) // half]
            row = []
            for j in range(2):
                s = st_ref[u, j * tq:(j + 1) * tq, :] + band
                m = jnp.max(s, axis=-1, keepdims=True)
                p = jnp.exp2(s - m)
                den = jnp.sum(p, axis=-1, keepdims=True)
                row.append((p.astype(BF16), den, m + jnp.log2(den)))
            probs.append(row)
        for u, row in enumerate(probs):
            q0, ks = window(g, u)
            o2 = _dot(jnp.concatenate([row[0][0], row[1][0]], axis=0), v_ref[pl.ds(ks, kw), :])
            outs = [o2[j * tq:(j + 1) * tq] / row[j][1] for j in range(2)]
            lses = [jnp.broadcast_to(lse, (tq, LANES)) for (_, _, lse) in row]
            o_ref[pl.ds(q0, tq), :] = jnp.where(first, outs[0], outs[1]).astype(o_ref.dtype)
            lse_ref[pl.ds(q0, tq), :] = jnp.where(first, lses[0], lses[1])

    def body(it, carry):
        issue(2 * it + 1, sb_ref)
        consume(2 * it, sa_ref)
        issue(2 * it + 2, sa_ref)
        consume(2 * it + 1, sb_ref)
        return carry

    issue(0, sa_ref)
    lax.fori_loop(0, n_groups // 2, body, 0)


def _attn_b_call(q, k, v, half, tq):
    b, dil, n, w = q.shape
    pairs = w // LANES
    tq = min(tq, n)
    kw = min(tq + 2 * half, n)
    group = min(BLOCKS_PER_STEP, dil * n // tq)
    assert n % tq == 0 and (kw - tq) % half == 0 and (dil * n) % (2 * group * tq) == 0
    flat = lambda a: a.reshape(b, dil * n, w)
    spec = pl.BlockSpec((None, dil * n, LANES), lambda bi, p: (bi, 0, p))
    o, lse = pl.pallas_call(
        functools.partial(_attn_b_kernel, tq=tq, half=half, n=n),
        grid=(b, pairs),
        in_specs=[spec, spec, spec],
        out_specs=[spec, spec],
        out_shape=[jax.ShapeDtypeStruct((b, dil * n, w), BF16), jax.ShapeDtypeStruct((b, dil * n, w), F32)],
        scratch_shapes=[pltpu.VMEM(((kw - tq) // half + 1, tq, kw), F32),
                        pltpu.VMEM((group, 2 * tq, kw), F32), pltpu.VMEM((group, 2 * tq, kw), F32)],
        compiler_params=pltpu.CompilerParams(dimension_semantics=("parallel", "parallel"),
                                             vmem_limit_bytes=VMEM_LIMIT),
    )(flat(q), flat(k), flat(v))
    return o.reshape(q.shape), lse.reshape(q.shape)


def _na_bias_kernel(rpb_ref, o_ref):
    base = pl.program_id(0) * RPB_PER_HEAD
    shape = (GRID_W, LANES)
    lane = lax.broadcasted_iota(jnp.int32, shape, 1)
    p = lax.broadcasted_iota(jnp.int32, shape, 0)
    c = lane % GRID_W
    upper = lane >= GRID_W
    c_start = jnp.clip(p - NA_COLS // 2, 0, GRID_W - NA_COLS)
    col_ok = (c >= c_start) & (c < c_start + NA_COLS)
    dc = c - p + (NA_COLS - 1)
    n_dc = 2 * NA_COLS - 1
    for v in range(NA_ROWS):
        for m in range(NA_ROWS * GRID_W // LANES):
            a_lo = 2 * m - v + (NA_ROWS - 1)
            acc = jnp.full(shape, NEG_INF, F32)
            for b in range(n_dc):
                val = jnp.where(upper, rpb_ref[base + (a_lo + 1) * n_dc + b], rpb_ref[base + a_lo * n_dc + b])
                acc = jnp.where(dc == b, val, acc)
            o_ref[0, v, :, m * LANES:(m + 1) * LANES] = jnp.where(col_ok, acc, NEG_INF)


def _na_bias_call(rpb):
    nh = rpb.shape[0] * rpb.shape[1]
    return pl.pallas_call(
        _na_bias_kernel,
        grid=(nh,),
        in_specs=[pl.BlockSpec(memory_space=pltpu.SMEM)],
        out_specs=pl.BlockSpec((1, NA_ROWS, GRID_W, NA_ROWS * GRID_W), lambda g: (g, 0, 0, 0)),
        out_shape=jax.ShapeDtypeStruct((nh, NA_ROWS, GRID_W, NA_ROWS * GRID_W), F32),
    )(rpb.reshape(-1))


def _attn_c_kernel(q_ref, k_ref, v_ref, bias_ref, o_ref, sa_ref, sb_ref):
    rows = q_ref.shape[1] // GRID_W
    win = NA_ROWS * GRID_W
    first = _lane_is_first_head((GRID_W, LANES))
    group = sa_ref.shape[0]
    n_groups = rows // group

    def window(g, u):
        r = g * group + u
        r_start = jnp.clip(r - NA_ROWS // 2, 0, rows - NA_ROWS)
        return pl.multiple_of(r * GRID_W, GRID_W), pl.multiple_of(r_start * GRID_W, GRID_W), r - r_start

    def issue(g, st_ref):
        g = jnp.minimum(g, n_groups - 1)
        for u in range(group):
            q0, ks, _ = window(g, u)
            st_ref[u] = _dot_nt(_stack_heads(q_ref[0, pl.ds(q0, GRID_W), :], first), k_ref[0, pl.ds(ks, win), :])

    def consume(g, st_ref):
        probs = []
        for u in range(group):
            variant = window(g, u)[2]
            row = []
            for j in range(2):
                s = st_ref[u, j * GRID_W:(j + 1) * GRID_W, :] + bias_ref[j, variant]
                m = jnp.max(s, axis=-1, keepdims=True)
                p = jnp.exp(s - m)
                row.append((p.astype(BF16), jnp.sum(p, axis=-1, keepdims=True)))
            probs.append(row)
        for u, row in enumerate(probs):
            q0, ks, _ = window(g, u)
            o2 = _dot(jnp.concatenate([row[0][0], row[1][0]], axis=0), v_ref[0, pl.ds(ks, win), :])
            outs = [o2[j * GRID_W:(j + 1) * GRID_W] / row[j][1] for j in range(2)]
            o_ref[0, pl.ds(q0, GRID_W), :] = jnp.where(first, outs[0], outs[1]).astype(o_ref.dtype)

    def body(it, carry):
        issue(2 * it + 1, sb_ref)
        consume(2 * it, sa_ref)
        issue(2 * it + 2, sa_ref)
        consume(2 * it + 1, sb_ref)
        return carry

    issue(0, sa_ref)
    lax.fori_loop(0, n_groups // 2, body, 0)


def _attn_c_call(q, k, v, bias):
    b, s, w = q.shape
    pairs = w // LANES
    spec = pl.BlockSpec((1, s, LANES), lambda bi, p: (bi, 0, p))
    return pl.pallas_call(
        _attn_c_kernel,
        grid=(b, pairs),
        in_specs=[spec, spec, spec,
                  pl.BlockSpec((2,) + bias.shape[1:], lambda bi, p: (p, 0, 0, 0))],
        out_specs=spec,
        out_shape=jax.ShapeDtypeStruct((b, s, w), BF16),
        scratch_shapes=[pltpu.VMEM((BLOCKS_PER_STEP, 2 * GRID_W, NA_ROWS * GRID_W), F32) for _ in range(2)],
        compiler_params=pltpu.CompilerParams(dimension_semantics=("parallel", "parallel"),
                                             vmem_limit_bytes=VMEM_LIMIT),
    )(q, k, v, bias)


def _out_kernel(x_ref, oa_ref, ob1_ref, l1_ref, *rest):
    nd = len(_DILATIONS)
    dil_refs = rest[:2 * nd]
    oc_ref, ga_ref, gb_ref, gc_ref, w_ref, o_ref, stage_ref = rest[2 * nd:]
    tm = x_ref.shape[0]
    nblk = WIDTH_B // LANES
    oa = jnp.concatenate([oa_ref[0, blk * LANES:(blk + 1) * LANES, :].astype(F32).T
                          for blk in range(WIDTH_A // LANES)], axis=-1)
    na = _rms(oa, ga_ref[...]).astype(BF16)
    outs, lses = [ob1_ref[...].astype(F32)], [l1_ref[...]]
    for di, dil in enumerate(_DILATIONS):
        rows = tm // dil
        for which, acc in ((0, outs), (1, lses)):
            src = dil_refs[2 * di + which]
            slot = (2 * di + which) * nblk
            for r in range(dil):
                for blk in range(nblk):
                    stage_ref[slot + blk, pl.ds(r, rows, stride=dil), :] = (
                        src[0, r, :, blk * LANES:(blk + 1) * LANES].astype(F32))
            acc.append(jnp.concatenate([stage_ref[slot + blk] for blk in range(nblk)], axis=-1))
    lmax = functools.reduce(jnp.maximum, lses)
    es = [jnp.exp2(l - lmax) for l in lses]
    ob = sum(e * o for e, o in zip(es, outs)) / sum(es)
    nb = _rms(ob, gb_ref[...]).astype(BF16)
    nc = _rms(oc_ref[...].astype(F32), gc_ref[...]).astype(BF16)
    acc = _dot(na, w_ref[0:WIDTH_A, :])
    acc += _dot(nb, w_ref[WIDTH_A:WIDTH_A + WIDTH_B, :])
    acc += _dot(nc, w_ref[WIDTH_A + WIDTH_B:, :])
    o_ref[...] = x_ref[...] + acc


def _out_call(x2, oa, ob1, l1, dilated, oc, ga, gb, gc, w, seq, tm):
    t, d = x2.shape
    nseq = seq // tm
    row = lambda i: (i, 0)
    const = lambda i: (0, 0)
    rs = lambda width: pl.BlockSpec((tm, width), row)
    cs = lambda width: pl.BlockSpec((1, width), const)
    dil_specs, dil_args = [], []
    for dil, pair in zip(_DILATIONS, dilated):
        for a in pair:
            dil_specs.append(pl.BlockSpec((1, dil, tm // dil, WIDTH_B), lambda i: (i // nseq, 0, i % nseq, 0)))
            dil_args.append(a)
    return pl.pallas_call(
        _out_kernel,
        grid=(t // tm,),
        in_specs=[rs(d), pl.BlockSpec((1, WIDTH_A, tm), lambda i: (i // nseq, 0, i % nseq)),
                  rs(WIDTH_B), rs(WIDTH_B), *dil_specs,
                  rs(WIDTH_C), cs(WIDTH_A), cs(WIDTH_B), cs(WIDTH_C), pl.BlockSpec(w.shape, const)],
        out_specs=rs(d),
        out_shape=jax.ShapeDtypeStruct((t, d), F32),
        scratch_shapes=[pltpu.VMEM((2 * len(_DILATIONS) * WIDTH_B // LANES, tm, LANES), F32)],
        compiler_params=pltpu.CompilerParams(dimension_semantics=("parallel",),
                                             vmem_limit_bytes=VMEM_LIMIT),
    )(x2, oa, ob1, l1, *dil_args, oc, ga, gb, gc, w)


def _mlp_kernel(x_ref, g_ref, w1_ref, w2_ref, gf_ref, o_ref, *, final_norm, tf):
    tm = x_ref.shape[0]
    hm = tm // MLP_SUBTILES
    rows = [slice(sub * hm, (sub + 1) * hm) for sub in range(MLP_SUBTILES)]
    hs = [_rms(x_ref[rs, :], g_ref[...]).astype(BF16) for rs in rows]
    accs = [None] * MLP_SUBTILES
    for f in range(w1_ref.shape[1] // tf):
        cols = slice(f * tf, (f + 1) * tf)
        us = [jnp.maximum(_dot(h, w1_ref[:, cols]), 0.0) for h in hs]
        for sub, u in enumerate(us):
            y = _dot((u * u).astype(BF16), w2_ref[cols, :])
            accs[sub] = y if accs[sub] is None else accs[sub] + y
    for rs, acc in zip(rows, accs):
        y = x_ref[rs, :] + acc
        if final_norm:
            y = _rms(y, gf_ref[...])
        o_ref[rs, :] = y


def _mlp_call(x2, g, w1, w2, gf, final_norm, tm, tf):
    t, d = x2.shape
    dff = w1.shape[1]
    resident = pl.Buffered(1)
    return pl.pallas_call(
        functools.partial(_mlp_kernel, final_norm=final_norm, tf=tf),
        grid=(t // tm,),
        in_specs=[
            pl.BlockSpec((tm, d), lambda i: (i, 0)),
            pl.BlockSpec((1, d), lambda i: (0, 0)),
            pl.BlockSpec((d, dff), lambda i: (0, 0), pipeline_mode=resident),
            pl.BlockSpec((dff, d), lambda i: (0, 0), pipeline_mode=resident),
            pl.BlockSpec((1, d), lambda i: (0, 0)),
        ],
        out_specs=pl.BlockSpec((tm, d), lambda i: (i, 0)),
        out_shape=jax.ShapeDtypeStruct((t, d), F32),
        compiler_params=pltpu.CompilerParams(dimension_semantics=("parallel",),
                                             vmem_limit_bytes=VMEM_LIMIT),
    )(x2, g, w1, w2, gf)


def _rotate_half_cols(w, half):
    return jnp.concatenate([-w[..., half:], w[..., :half]], axis=-1)


def _rope_tables(seq):
    pos = jnp.arange(seq, dtype=F32)

    def cos_sin(half):
        inv_freq = ROPE_THETA ** (-jnp.arange(half, dtype=F32) / half)
        ang = pos[:, None] * inv_freq[None, :]
        return jnp.cos(ang), jnp.sin(ang)

    ca, sa = cos_sin(QK_ROPE // 2)
    ca2, sa2 = jnp.concatenate([ca, ca], -1), jnp.concatenate([sa, sa], -1)
    ones = jnp.ones((seq, QK_NOPE), F32)
    zeros = jnp.zeros((seq, QK_NOPE), F32)
    tail = jnp.zeros((seq, LANES - QK_NOPE - QK_ROPE), F32)
    scale_a = (QK_NOPE + QK_ROPE) ** -0.5 * LOG2_E
    cosq = jnp.concatenate([ones, ca2, tail], -1) * scale_a
    sinq = jnp.concatenate([zeros, sa2, tail], -1) * scale_a
    cosk = jnp.concatenate([zeros, ca2, tail], -1)
    sink = jnp.concatenate([zeros, sa2, tail], -1)
    cb, sb = cos_sin(HEAD_DIM // 2)
    cosb = jnp.concatenate([cb, cb, cb, cb], -1)
    sinb = jnp.concatenate([-sb, sb, -sb, sb], -1)
    return jnp.concatenate([cosq, sinq, cosk, sink, cosb, sinb], -1)


def _layer_weights(w_in, w_uq, w_ukv):
    d = w_in.shape[0]
    scale = HEAD_DIM ** -0.5
    c_b = Q_LORA + KV_LORA + QK_ROPE
    c_c = c_b + 3 * WIDTH_B
    w_kpe = w_in[:, Q_LORA + KV_LORA:c_b]

    def place(w):
        return jnp.concatenate([jnp.zeros((d, QK_NOPE), F32), w, jnp.zeros((d, LANES - QK_NOPE - QK_ROPE), F32)], -1)

    wbig = jnp.concatenate([
        w_in[:, :Q_LORA + KV_LORA],
        place(w_kpe), place(_rotate_half_cols(w_kpe, QK_ROPE // 2)),
        w_in[:, c_b:c_b + WIDTH_B] * scale, w_in[:, c_b + WIDTH_B:c_c],
        w_in[:, c_c:c_c + WIDTH_C] * scale, w_in[:, c_c + WIDTH_C:],
    ], -1).astype(BF16)

    uq = w_uq.reshape(Q_LORA, HEADS_A, QK_NOPE + QK_ROPE)
    pad = jnp.zeros((Q_LORA, HEADS_A, LANES - QK_NOPE - QK_ROPE), F32)
    uq_pad = jnp.concatenate([uq, pad], -1)
    uq_rot = jnp.concatenate([jnp.zeros((Q_LORA, HEADS_A, QK_NOPE), F32),
                              _rotate_half_cols(uq[..., QK_NOPE:], QK_ROPE // 2), pad], -1)
    wuq = jnp.concatenate([uq_pad.reshape(Q_LORA, QK_A_PAD), uq_rot.reshape(Q_LORA, QK_A_PAD)], -1).astype(BF16)

    ukv = w_ukv.reshape(KV_LORA, HEADS_A, QK_NOPE + HEAD_DIM)
    uk_pad = jnp.concatenate([ukv[..., :QK_NOPE], jnp.zeros((KV_LORA, HEADS_A, LANES - QK_NOPE), F32)], -1)
    wukv = jnp.concatenate([uk_pad.reshape(KV_LORA, QK_A_PAD), ukv[..., QK_NOPE:].reshape(KV_LORA, WIDTH_A)],
                           -1).astype(BF16)
    return wbig, wuq, wukv


def kernel(x, g_mix, w_in, q_norm, w_uq, kv_norm, w_ukv, rpb, out_norm_a, out_norm_b, out_norm_c, w_out, g_mlp,
           w_mlp_in, w_mlp_out, g_final):
    b, s, d = x.shape
    depth = w_in.shape[0]
    t = b * s
    assert s % (2 * BLOCKS_PER_STEP * GRID_W) == 0 and s // GRID_W >= NA_ROWS
    tm_proj = min(512, s)
    tm_out = min(512, s)
    tm_mlp = min(1024, t)
    tq_a = min(256, s)
    tk_a = min(512, s)

    tabs = _rope_tables(s)
    na_bias = _na_bias_call(rpb).reshape(depth, HEADS_C, NA_ROWS, GRID_W, NA_ROWS * GRID_W)
    row = lambda a: a.reshape(1, -1)

    x2 = x.reshape(t, d)
    for l in range(depth):
        wbig, wuq, wukv = _layer_weights(w_in[l], w_uq[l], w_ukv[l])
        qa, ka, va, qb, kb, vb, qc, kc, vc, *dil_qkv = _proj_call(
            x2, row(g_mix[l]), wbig, row(q_norm[l]), wuq, row(kv_norm[l]), wukv, tabs, s, tm_proj)
        seq3 = lambda a: a.reshape(b, s, a.shape[-1])
        oa = _attn_a_call(qa, seq3(ka), va, tq_a, tk_a)
        ob1, l1, dilated = None, None, []
        for window, dil in DILATED_PAIRS:
            half = window // (2 * dil)
            if dil == 1:
                unit = lambda a: a.reshape(b, 1, s, WIDTH_B)
                o_i, lse_i = _attn_b_call(unit(qb), unit(kb), unit(vb), half, 128)
                ob1, l1 = o_i.reshape(t, WIDTH_B), lse_i.reshape(t, WIDTH_B)
            else:
                di = 3 * _DILATIONS.index(dil)
                dilated.append(_attn_b_call(*dil_qkv[di:di + 3], half, 128))
        oc = _attn_c_call(seq3(qc), seq3(kc), seq3(vc), na_bias[l]).reshape(t, WIDTH_C)
        x2 = _out_call(x2, oa, ob1, l1, dilated, oc, row(out_norm_a[l]), row(out_norm_b[l]), row(out_norm_c[l]),
                       w_out[l].astype(BF16), s, tm_out)
        x2 = _mlp_call(x2, row(g_mlp[l]), w_mlp_in[l].astype(BF16), w_mlp_out[l].astype(BF16), row(g_final),
                       l == depth - 1, tm_mlp, min(1024, w_mlp_in.shape[2]))
    return x2.reshape(b, s, d)
```

```python
import functools

import jax
import jax.numpy as jnp
from jax import lax
from jax.experimental import pallas as pl
from jax.experimental.pallas import tpu as pltpu

HEAD_DIM = 64
LANES = 128
BF16_ROWS = 16
HEADS_A = 6
HEADS_B = 6
HEADS_C = 4
Q_LORA = 256
KV_LORA = 128
QK_NOPE = 64
QK_ROPE = 32
DILATED_PAIRS = ((128, 1), (512, 4), (2048, 16))
_DILATIONS = tuple(dil for _, dil in DILATED_PAIRS if dil > 1)
GRID_W = 64
NA_ROWS = 8
NA_COLS = 16
ROPE_THETA = 10000.0
NORM_EPS = 1e-6
NEG_INF = -1e30
LOG2_E = 1.4426950408889634
PROJ_SUBTILES = 2
MLP_SUBTILES = 2
BLOCKS_PER_STEP = 4

WIDTH_A = HEADS_A * HEAD_DIM
WIDTH_B = HEADS_B * HEAD_DIM
WIDTH_C = HEADS_C * HEAD_DIM
QK_A_PAD = HEADS_A * LANES
RPB_PER_HEAD = (2 * NA_ROWS - 1) * (2 * NA_COLS - 1)

VMEM_LIMIT = 56 * 1024 * 1024

BF16 = jnp.bfloat16
F32 = jnp.float32


def _rms(x, g):
    return x * lax.rsqrt(jnp.mean(x * x, axis=-1, keepdims=True) + NORM_EPS) * g


def _dot(a, b):
    return jnp.dot(a, b, preferred_element_type=F32)


def _dot_nt(a, b):
    return lax.dot_general(a, b, (((1,), (1,)), ((), ())), preferred_element_type=F32)


def _lane_is_first_head(shape):
    return lax.broadcasted_iota(jnp.int32, shape, len(shape) - 1) < HEAD_DIM


def _stack_heads(q, first):
    return jnp.concatenate([_keep_head(q, first, 0), _keep_head(q, first, 1)], axis=0)


def _keep_head(q, first, j):
    zero = jnp.zeros_like(q)
    return jnp.where(first, q, zero) if j == 0 else jnp.where(first, zero, q)


_C_CQ = 0
_C_CKV = _C_CQ + Q_LORA
_C_KPE = _C_CKV + KV_LORA
_C_KPR = _C_KPE + LANES
_C_QB = _C_KPR + LANES
_C_KB = _C_QB + WIDTH_B
_C_VB = _C_KB + WIDTH_B
_C_QC = _C_VB + WIDTH_B
_C_KC = _C_QC + WIDTH_C
_C_VC = _C_KC + WIDTH_C
_C_END = _C_VC + WIDTH_C

_T_COSQ, _T_SINQ, _T_COSK, _T_SINK, _T_COSB, _T_SINB = range(6)


def _proj_kernel(x_ref, g_ref, wbig_ref, qn_ref, wuq_ref, kvn_ref, wukv_ref, tab_ref,
                 qa_ref, ka_ref, va_ref, qb_ref, kb_ref, vb_ref, qc_ref, kc_ref, vc_ref, *rest):
    dil_refs, stage_ref = rest[:-1], rest[-1]
    tm = x_ref.shape[0]
    hm = tm // PROJ_SUBTILES
    nblk = WIDTH_B // LANES
    first_half = (lax.broadcasted_iota(jnp.int32, (hm, LANES), 1) % HEAD_DIM) < HEAD_DIM // 2

    for sub in range(PROJ_SUBTILES):
        rs = slice(sub * hm, (sub + 1) * hm)

        def tab(i):
            return tab_ref[rs, i * LANES:(i + 1) * LANES]

        h = _rms(x_ref[rs, :], g_ref[...]).astype(BF16)
        proj = _dot(h, wbig_ref[...])

        cqn = _rms(proj[:, _C_CQ:_C_CKV], qn_ref[...]).astype(BF16)
        qa2 = _dot(cqn, wuq_ref[...])
        ckvn = _rms(proj[:, _C_CKV:_C_KPE], kvn_ref[...]).astype(BF16)
        kv2 = _dot(ckvn, wukv_ref[...])
        kpe = proj[:, _C_KPE:_C_KPR] * tab(_T_COSK) + proj[:, _C_KPR:_C_QB] * tab(_T_SINK)
        cosq, sinq = tab(_T_COSQ), tab(_T_SINQ)
        for hd in range(HEADS_A):
            sl = slice(hd * LANES, (hd + 1) * LANES)
            rot = slice(QK_A_PAD + hd * LANES, QK_A_PAD + (hd + 1) * LANES)
            qa_ref[0, sl, rs] = (qa2[:, sl] * cosq + qa2[:, rot] * sinq).T.astype(BF16)
            ka_ref[rs, sl] = (kv2[:, sl] + kpe).astype(BF16)
        for blk in range(WIDTH_A // LANES):
            sl = slice(blk * LANES, (blk + 1) * LANES)
            va_ref[0, sl, rs] = kv2[:, QK_A_PAD + blk * LANES:QK_A_PAD + (blk + 1) * LANES].T.astype(BF16)

        cosb, sinb = tab(_T_COSB), tab(_T_SINB)
        for ti, (src, dst) in enumerate(((_C_QB, qb_ref), (_C_KB, kb_ref), (_C_VB, vb_ref))):
            for blk in range(nblk):
                xb = proj[:, src + blk * LANES:src + (blk + 1) * LANES]
                if dst is not vb_ref:
                    swapped = jnp.where(first_half, pltpu.roll(xb, LANES - HEAD_DIM // 2, 1),
                                        pltpu.roll(xb, HEAD_DIM // 2, 1))
                    xb = xb * cosb + swapped * sinb
                    if dst is qb_ref:
                        xb = xb * LOG2_E
                dst[rs, blk * LANES:(blk + 1) * LANES] = xb.astype(BF16)
                stage_ref[ti * nblk + blk, rs, :] = xb
        for di, dil in enumerate(_DILATIONS):
            rows = hm // dil
            for ti in range(3):
                dref = dil_refs[di * 3 + ti]
                for r in range(dil):
                    for blk in range(nblk):
                        dref[0, r, sub * rows:(sub + 1) * rows, blk * LANES:(blk + 1) * LANES] = (
                            stage_ref[ti * nblk + blk, pl.ds(sub * hm + r, rows, stride=dil), :].astype(BF16))

        qc_ref[rs, :] = proj[:, _C_QC:_C_KC].astype(BF16)
        kc_ref[rs, :] = proj[:, _C_KC:_C_VC].astype(BF16)
        vc_ref[rs, :] = proj[:, _C_VC:_C_END].astype(BF16)


def _proj_call(x2, g, wbig, qn, wuq, kvn, wukv, tabs, seq, tm):
    t, d = x2.shape
    nseq = seq // tm
    row = lambda i: (i, 0)
    const = lambda i: (0, 0)
    widths = (QK_A_PAD, QK_A_PAD, WIDTH_A, WIDTH_B, WIDTH_B, WIDTH_B, WIDTH_C, WIDTH_C, WIDTH_C)
    out_specs = [pl.BlockSpec((tm, w), row) for w in widths]
    out_shape = [jax.ShapeDtypeStruct((t, w), BF16) for w in widths]
    for idx in (0, 2):
        out_specs[idx] = pl.BlockSpec((1, widths[idx], tm), lambda i: (i // nseq, 0, i % nseq))
        out_shape[idx] = jax.ShapeDtypeStruct((t // seq, widths[idx], seq), BF16)
    for dil in _DILATIONS:
        assert tm % (BF16_ROWS * dil * PROJ_SUBTILES) == 0
        for _ in range(3):
            out_specs.append(pl.BlockSpec((1, dil, tm // dil, WIDTH_B), lambda i: (i // nseq, 0, i % nseq, 0)))
            out_shape.append(jax.ShapeDtypeStruct((t // seq, dil, seq // dil, WIDTH_B), BF16))
    return pl.pallas_call(
        _proj_kernel,
        grid=(t // tm,),
        in_specs=[
            pl.BlockSpec((tm, d), row),
            pl.BlockSpec((1, d), const),
            pl.BlockSpec(wbig.shape, const),
            pl.BlockSpec((1, Q_LORA), const),
            pl.BlockSpec(wuq.shape, const),
            pl.BlockSpec((1, KV_LORA), const),
            pl.BlockSpec(wukv.shape, const),
            pl.BlockSpec((tm, tabs.shape[1]), lambda i: (i % nseq, 0)),
        ],
        out_specs=out_specs,
        out_shape=out_shape,
        scratch_shapes=[pltpu.VMEM((3 * WIDTH_B // LANES, tm, LANES), F32)],
        compiler_params=pltpu.CompilerParams(dimension_semantics=("parallel",),
                                             vmem_limit_bytes=VMEM_LIMIT),
    )(x2, g, wbig, qn, wuq, kvn, wukv, tabs)


def _attn_a_kernel(qt_ref, k_ref, vt_ref, o_ref, a0_ref, a1_ref, b0_ref, b1_ref, *, tq, tk):
    s = k_ref.shape[1]
    nk, nq = s // tk, s // tq
    st_a, st_b = (a0_ref, a1_ref), (b0_ref, b1_ref)
    ones = jnp.ones((BF16_ROWS, tk), BF16)

    def scores(qb, c, st_ref):
        q0 = pl.multiple_of(qb * tq, tq)
        ks = pl.multiple_of(c * tk, tk)
        for j in range(2):
            st_ref[j] = _dot(k_ref[0, pl.ds(ks, tk), j * LANES:(j + 1) * LANES],
                             qt_ref[0, j * LANES:(j + 1) * LANES, pl.ds(q0, tq)])

    def accumulate(c, st_ref, carry):
        ks = pl.multiple_of(c * tk, tk)
        stats = []
        for j in range(2):
            m = carry[j][0]
            st = st_ref[j]
            m_new = jnp.maximum(m, jnp.max(st, axis=0, keepdims=True))
            stats.append((m_new, jnp.exp2(m - m_new), jnp.exp2(st - m_new).astype(BF16)))
        new = []
        for j in range(2):
            m_new, alpha, pt = stats[j]
            vt = jnp.concatenate([vt_ref[0, j * HEAD_DIM:(j + 1) * HEAD_DIM, pl.ds(ks, tk)], ones], axis=0)
            new.append((m_new, alpha * carry[j][1] + _dot(vt, pt)))
        return tuple(new)

    def half_step(qb, c, src, dst, carry, wrap=False):
        for u in range(2):
            if wrap:
                scores(jnp.minimum(qb + 1, nq - 1), u, dst[u])
            else:
                scores(qb, c + 2 + u, dst[u])
            carry = accumulate(c + u, src[u], carry)
        return carry

    def q_block(qb, _):
        def body(i, carry):
            carry = half_step(qb, 4 * i, st_a, st_b, carry)
            return half_step(qb, 4 * i + 2, st_b, st_a, carry)

        init = tuple((jnp.full((1, tq), NEG_INF, F32), jnp.zeros((HEAD_DIM + BF16_ROWS, tq), F32))
                     for _ in range(2))
        carry = lax.fori_loop(0, nk // 4 - 1, body, init)
        carry = half_step(qb, nk - 4, st_a, st_b, carry)
        res = half_step(qb, nk - 2, st_b, st_a, carry, wrap=True)
        q0 = pl.multiple_of(qb * tq, tq)
        for j in range(2):
            acc = res[j][1]
            o_ref[0, j * HEAD_DIM:(j + 1) * HEAD_DIM, pl.ds(q0, tq)] = (
                acc[:HEAD_DIM] / acc[HEAD_DIM:HEAD_DIM + 1]).astype(o_ref.dtype)
        return 0

    scores(0, 0, st_a[0])
    scores(0, 1, st_a[1])
    lax.fori_loop(0, nq, q_block, 0)


def _attn_a_call(qt, ka, vt, tq, tk):
    b, s, _ = ka.shape
    pairs = HEADS_A // 2
    assert s % (4 * tk) == 0
    return pl.pallas_call(
        functools.partial(_attn_a_kernel, tq=tq, tk=tk),
        grid=(b, pairs),
        in_specs=[
            pl.BlockSpec((1, 2 * LANES, s), lambda bi, p: (bi, p, 0)),
            pl.BlockSpec((1, s, 2 * LANES), lambda bi, p: (bi, 0, p)),
            pl.BlockSpec((1, LANES, s), lambda bi, p: (bi, p, 0)),
        ],
        out_specs=pl.BlockSpec((1, LANES, s), lambda bi, p: (bi, p, 0)),
        out_shape=jax.ShapeDtypeStruct((b, WIDTH_A, s), BF16),
        scratch_shapes=[pltpu.VMEM((2, tk, tq), F32) for _ in range(4)],
        compiler_params=pltpu.CompilerParams(dimension_semantics=("parallel", "parallel"),
                                             vmem_limit_bytes=VMEM_LIMIT),
    )(qt, ka, vt)


def _attn_b_kernel(q_ref, k_ref, v_ref, o_ref, lse_ref, band_ref, sa_ref, sb_ref, *, tq, half, n):
    total = q_ref.shape[0]
    kw = min(tq + 2 * half, n)
    first = _lane_is_first_head((tq, LANES))

    group = min(BLOCKS_PER_STEP, total // tq)
    diff = (lax.broadcasted_iota(jnp.int32, (tq, kw), 0) - lax.broadcasted_iota(jnp.int32, (tq, kw), 1))
    for var in range(band_ref.shape[0]):
        band_ref[var] = jnp.where(jnp.abs(diff + var * half) <= half, 0.0, NEG_INF)

    n_groups = total // (tq * group)

    def window(g, u):
        q0 = pl.multiple_of((g * group + u) * tq, tq)
        seg0 = (q0 // n) * n
        return q0, pl.multiple_of(jnp.clip(q0 - half, seg0, seg0 + n - kw), half)

    def issue(g, st_ref):
        g = jnp.minimum(g, n_groups - 1)
        for u in range(group):
            q0, ks = window(g, u)
            st_ref[u] = _dot_nt(_stack_heads(q_ref[pl.ds(q0, tq), :], first), k_ref[pl.ds(ks, kw), :])

    def consume(g, st_ref):
        probs = []
        for u in range(group):
            q0, ks = window(g, u)
            band = band_ref[(q0 - ks) // half]
            row = []
            for j in range(2):
                s = st_ref[u, j * tq:(j + 1) * tq, :] + band
                m = jnp.max(s, axis=-1, keepdims=True)
                p = jnp.exp2(s - m)
                den = jnp.sum(p, axis=-1, keepdims=True)
                row.append((p.astype(BF16), den, m + jnp.log2(den)))
            probs.append(row)
        for u, row in enumerate(probs):
            q0, ks = window(g, u)
            o2 = _dot(jnp.concatenate([row[0][0], row[1][0]], axis=0), v_ref[pl.ds(ks, kw), :])
            outs = [o2[j * tq:(j + 1) * tq] / row[j][1] for j in range(2)]
            lses = [jnp.broadcast_to(lse, (tq, LANES)) for (_, _, lse) in row]
            o_ref[pl.ds(q0, tq), :] = jnp.where(first, outs[0], outs[1]).astype(o_ref.dtype)
            lse_ref[pl.ds(q0, tq), :] = jnp.where(first, lses[0], lses[1])

    def body(it, carry):
        issue(2 * it + 1, sb_ref)
        consume(2 * it, sa_ref)
        issue(2 * it + 2, sa_ref)
        consume(2 * it + 1, sb_ref)
        return carry

    issue(0, sa_ref)
    lax.fori_loop(0, n_groups // 2, body, 0)


def _attn_b_call(q, k, v, half, tq):
    b, dil, n, w = q.shape
    pairs = w // LANES
    tq = min(tq, n)
    kw = min(tq + 2 * half, n)
    group = min(BLOCKS_PER_STEP, dil * n // tq)
    assert n % tq == 0 and (kw - tq) % half == 0 and (dil * n) % (2 * group * tq) == 0
    flat = lambda a: a.reshape(b, dil * n, w)
    spec = pl.BlockSpec((None, dil * n, LANES), lambda bi, p: (bi, 0, p))
    o, lse = pl.pallas_call(
        functools.partial(_attn_b_kernel, tq=tq, half=half, n=n),
        grid=(b, pairs),
        in_specs=[spec, spec, spec],
        out_specs=[spec, spec],
        out_shape=[jax.ShapeDtypeStruct((b, dil * n, w), BF16), jax.ShapeDtypeStruct((b, dil * n, w), F32)],
        scratch_shapes=[pltpu.VMEM(((kw - tq) // half + 1, tq, kw), F32),
                        pltpu.VMEM((group, 2 * tq, kw), F32), pltpu.VMEM((group, 2 * tq, kw), F32)],
        compiler_params=pltpu.CompilerParams(dimension_semantics=("parallel", "parallel"),
                                             vmem_limit_bytes=VMEM_LIMIT),
    )(flat(q), flat(k), flat(v))
    return o.reshape(q.shape), lse.reshape(q.shape)


def _na_bias_kernel(rpb_ref, o_ref):
    base = pl.program_id(0) * RPB_PER_HEAD
    shape = (GRID_W, LANES)
    lane = lax.broadcasted_iota(jnp.int32, shape, 1)
    p = lax.broadcasted_iota(jnp.int32, shape, 0)
    c = lane % GRID_W
    upper = lane >= GRID_W
    c_start = jnp.clip(p - NA_COLS // 2, 0, GRID_W - NA_COLS)
    col_ok = (c >= c_start) & (c < c_start + NA_COLS)
    dc = c - p + (NA_COLS - 1)
    n_dc = 2 * NA_COLS - 1
    for v in range(NA_ROWS):
        for m in range(NA_ROWS * GRID_W // LANES):
            a_lo = 2 * m - v + (NA_ROWS - 1)
            acc = jnp.full(shape, NEG_INF, F32)
            for b in range(n_dc):
                val = jnp.where(upper, rpb_ref[base + (a_lo + 1) * n_dc + b], rpb_ref[base + a_lo * n_dc + b])
                acc = jnp.where(dc == b, val, acc)
            o_ref[0, v, :, m * LANES:(m + 1) * LANES] = jnp.where(col_ok, acc, NEG_INF)


def _na_bias_call(rpb):
    nh = rpb.shape[0] * rpb.shape[1]
    return pl.pallas_call(
        _na_bias_kernel,
        grid=(nh,),
        in_specs=[pl.BlockSpec(memory_space=pltpu.SMEM)],
        out_specs=pl.BlockSpec((1, NA_ROWS, GRID_W, NA_ROWS * GRID_W), lambda g: (g, 0, 0, 0)),
        out_shape=jax.ShapeDtypeStruct((nh, NA_ROWS, GRID_W, NA_ROWS * GRID_W), F32),
    )(rpb.reshape(-1))


def _attn_c_kernel(q_ref, k_ref, v_ref, bias_ref, o_ref, sa_ref, sb_ref):
    rows = q_ref.shape[1] // GRID_W
    win = NA_ROWS * GRID_W
    first = _lane_is_first_head((GRID_W, LANES))
    group = sa_ref.shape[0]
    n_groups = rows // group

    def window(g, u):
        r = g * group + u
        r_start = jnp.clip(r - NA_ROWS // 2, 0, rows - NA_ROWS)
        return pl.multiple_of(r * GRID_W, GRID_W), pl.multiple_of(r_start * GRID_W, GRID_W), r - r_start

    def issue(g, st_ref):
        g = jnp.minimum(g, n_groups - 1)
        for u in range(group):
            q0, ks, _ = window(g, u)
            st_ref[u] = _dot_nt(_stack_heads(q_ref[0, pl.ds(q0, GRID_W), :], first), k_ref[0, pl.ds(ks, win), :])

    def consume(g, st_ref):
        probs = []
        for u in range(group):
            variant = window(g, u)[2]
            row = []
            for j in range(2):
                s = st_ref[u, j * GRID_W:(j + 1) * GRID_W, :] + bias_ref[j, variant]
                m = jnp.max(s, axis=-1, keepdims=True)
                p = jnp.exp(s - m)
                row.append((p.astype(BF16), jnp.sum(p, axis=-1, keepdims=True)))
            probs.append(row)
        for u, row in enumerate(probs):
            q0, ks, _ = window(g, u)
            o2 = _dot(jnp.concatenate([row[0][0], row[1][0]], axis=0), v_ref[0, pl.ds(ks, win), :])
            outs = [o2[j * GRID_W:(j + 1) * GRID_W] / row[j][1] for j in range(2)]
            o_ref[0, pl.ds(q0, GRID_W), :] = jnp.where(first, outs[0], outs[1]).astype(o_ref.dtype)

    def body(it, carry):
        issue(2 * it + 1, sb_ref)
        consume(2 * it, sa_ref)
        issue(2 * it + 2, sa_ref)
        consume(2 * it + 1, sb_ref)
        return carry

    issue(0, sa_ref)
    lax.fori_loop(0, n_groups // 2, body, 0)


def _attn_c_call(q, k, v, bias):
    b, s, w = q.shape
    pairs = w // LANES
    spec = pl.BlockSpec((1, s, LANES), lambda bi, p: (bi, 0, p))
    return pl.pallas_call(
        _attn_c_kernel,
        grid=(b, pairs),
        in_specs=[spec, spec, spec,
                  pl.BlockSpec((2,) + bias.shape[1:], lambda bi, p: (p, 0, 0, 0))],
        out_specs=spec,
        out_shape=jax.ShapeDtypeStruct((b, s, w), BF16),
        scratch_shapes=[pltpu.VMEM((BLOCKS_PER_STEP, 2 * GRID_W, NA_ROWS * GRID_W), F32) for _ in range(2)],
        compiler_params=pltpu.CompilerParams(dimension_semantics=("parallel", "parallel"),
                                             vmem_limit_bytes=VMEM_LIMIT),
    )(q, k, v, bias)


def _restage_dilated(tm, dil_refs, stage_ref):
    nblk = WIDTH_B // LANES
    for di, dil in enumerate(_DILATIONS):
        rows = tm // dil
        for which in range(2):
            src = dil_refs[2 * di + which]
            slot = (2 * di + which) * nblk
            for r in range(dil):
                for blk in range(nblk):
                    stage_ref[slot + blk, pl.ds(r, rows, stride=dil), :] = (
                        src[0, r, :, blk * LANES:(blk + 1) * LANES].astype(F32))


def _mixed_rows(sub, hm, oa_ref, ob1_ref, l1_ref, oc_ref, ga_ref, gb_ref, gc_ref, stage_ref):
    rs = slice(sub * hm, (sub + 1) * hm)
    nblk = WIDTH_B // LANES
    oa = jnp.concatenate([oa_ref[0, blk * LANES:(blk + 1) * LANES, rs].astype(F32).T
                          for blk in range(WIDTH_A // LANES)], axis=-1)
    outs, lses = [ob1_ref[rs, :].astype(F32)], [l1_ref[rs, :]]
    for di in range(len(_DILATIONS)):
        for which, acc in ((0, outs), (1, lses)):
            slot = (2 * di + which) * nblk
            acc.append(jnp.concatenate([stage_ref[slot + blk, rs, :] for blk in range(nblk)], axis=-1))
    lmax = functools.reduce(jnp.maximum, lses)
    es = [jnp.exp2(l - lmax) for l in lses]
    ob = sum(e * o for e, o in zip(es, outs)) / sum(es)
    return (_rms(oa, ga_ref[...]).astype(BF16), _rms(ob, gb_ref[...]).astype(BF16),
            _rms(oc_ref[rs, :].astype(F32), gc_ref[...]).astype(BF16))


def _out_mlp_kernel(x_ref, oa_ref, ob1_ref, l1_ref, *rest, final_norm, tf):
    nd = len(_DILATIONS)
    dil_refs = rest[:2 * nd]
    oc_ref, ga_ref, gb_ref, gc_ref, wo_ref, g_ref, w1_ref, w2_ref, gf_ref, o_ref, stage_ref = rest[2 * nd:]
    hm = x_ref.shape[0] // MLP_SUBTILES
    rows = [slice(sub * hm, (sub + 1) * hm) for sub in range(MLP_SUBTILES)]
    bounds = (0, WIDTH_A, WIDTH_A + WIDTH_B, WIDTH_A + WIDTH_B + WIDTH_C)

    def up(h, f):
        return jnp.maximum(_dot(h, w1_ref[:, f * tf:(f + 1) * tf]), 0.0)

    _restage_dilated(x_ref.shape[0], dil_refs, stage_ref)
    hs, us = [], []
    for sub, rs in enumerate(rows):
        parts = _mixed_rows(sub, hm, oa_ref, ob1_ref, l1_ref, oc_ref, ga_ref, gb_ref, gc_ref, stage_ref)
        x1 = x_ref[rs, :]
        for part, lo, hi in zip(parts, bounds[:-1], bounds[1:]):
            x1 = x1 + _dot(part, wo_ref[lo:hi, :])
        o_ref[rs, :] = x1
        hs.append(_rms(x1, g_ref[...]).astype(BF16))
        us.append(up(hs[sub], 0))
    accs = [None] * MLP_SUBTILES
    for f in range(w1_ref.shape[1] // tf):
        if f > 0:
            us = [up(h, f) for h in hs]
        for sub, u in enumerate(us):
            y = _dot((u * u).astype(BF16), w2_ref[f * tf:(f + 1) * tf, :])
            accs[sub] = y if accs[sub] is None else accs[sub] + y
    for rs, acc in zip(rows, accs):
        y = o_ref[rs, :] + acc
        o_ref[rs, :] = _rms(y, gf_ref[...]) if final_norm else y


def _out_mlp_call(x2, oa, ob1, l1, dilated, oc, ga, gb, gc, wo, g, w1, w2, gf, final_norm, seq, tm, tf):
    t, d = x2.shape
    nseq = seq // tm
    row = lambda i: (i, 0)
    const = lambda i: (0, 0)
    rs = lambda width: pl.BlockSpec((tm, width), row)
    cs = lambda width: pl.BlockSpec((1, width), const)
    resident = lambda w: pl.BlockSpec(w.shape, const, pipeline_mode=pl.Buffered(1))
    dil_specs, dil_args = [], []
    for dil, pair in zip(_DILATIONS, dilated):
        assert tm % (BF16_ROWS * dil * MLP_SUBTILES) == 0
        for a in pair:
            dil_specs.append(pl.BlockSpec((1, dil, tm // dil, WIDTH_B), lambda i: (i // nseq, 0, i % nseq, 0)))
            dil_args.append(a)
    return pl.pallas_call(
        functools.partial(_out_mlp_kernel, final_norm=final_norm, tf=tf),
        grid=(t // tm,),
        in_specs=[rs(d), pl.BlockSpec((1, WIDTH_A, tm), lambda i: (i // nseq, 0, i % nseq)),
                  rs(WIDTH_B), rs(WIDTH_B), *dil_specs, rs(WIDTH_C), cs(WIDTH_A), cs(WIDTH_B), cs(WIDTH_C),
                  resident(wo), cs(d), resident(w1), resident(w2), cs(d)],
        out_specs=rs(d),
        out_shape=jax.ShapeDtypeStruct((t, d), F32),
        scratch_shapes=[pltpu.VMEM((2 * len(_DILATIONS) * WIDTH_B // LANES, tm, LANES), F32)],
        compiler_params=pltpu.CompilerParams(dimension_semantics=("parallel",),
                                             vmem_limit_bytes=VMEM_LIMIT),
    )(x2, oa, ob1, l1, *dil_args, oc, ga, gb, gc, wo, g, w1, w2, gf)


def _rotate_half_cols(w, half):
    return jnp.concatenate([-w[..., half:], w[..., :half]], axis=-1)


def _rope_tables(seq):
    pos = jnp.arange(seq, dtype=F32)

    def cos_sin(half):
        inv_freq = ROPE_THETA ** (-jnp.arange(half, dtype=F32) / half)
        ang = pos[:, None] * inv_freq[None, :]
        return jnp.cos(ang), jnp.sin(ang)

    ca, sa = cos_sin(QK_ROPE // 2)
    ca2, sa2 = jnp.concatenate([ca, ca], -1), jnp.concatenate([sa, sa], -1)
    ones = jnp.ones((seq, QK_NOPE), F32)
    zeros = jnp.zeros((seq, QK_NOPE), F32)
    tail = jnp.zeros((seq, LANES - QK_NOPE - QK_ROPE), F32)
    scale_a = (QK_NOPE + QK_ROPE) ** -0.5 * LOG2_E
    cosq = jnp.concatenate([ones, ca2, tail], -1) * scale_a
    sinq = jnp.concatenate([zeros, sa2, tail], -1) * scale_a
    cosk = jnp.concatenate([zeros, ca2, tail], -1)
    sink = jnp.concatenate([zeros, sa2, tail], -1)
    cb, sb = cos_sin(HEAD_DIM // 2)
    cosb = jnp.concatenate([cb, cb, cb, cb], -1)
    sinb = jnp.concatenate([-sb, sb, -sb, sb], -1)
    return jnp.concatenate([cosq, sinq, cosk, sink, cosb, sinb], -1)


def _layer_weights(w_in, w_uq, w_ukv):
    d = w_in.shape[0]
    scale = HEAD_DIM ** -0.5
    c_b = Q_LORA + KV_LORA + QK_ROPE
    c_c = c_b + 3 * WIDTH_B
    w_kpe = w_in[:, Q_LORA + KV_LORA:c_b]

    def place(w):
        return jnp.concatenate([jnp.zeros((d, QK_NOPE), F32), w, jnp.zeros((d, LANES - QK_NOPE - QK_ROPE), F32)], -1)

    wbig = jnp.concatenate([
        w_in[:, :Q_LORA + KV_LORA],
        place(w_kpe), place(_rotate_half_cols(w_kpe, QK_ROPE // 2)),
        w_in[:, c_b:c_b + WIDTH_B] * scale, w_in[:, c_b + WIDTH_B:c_c],
        w_in[:, c_c:c_c + WIDTH_C] * scale, w_in[:, c_c + WIDTH_C:],
    ], -1).astype(BF16)

    uq = w_uq.reshape(Q_LORA, HEADS_A, QK_NOPE + QK_ROPE)
    pad = jnp.zeros((Q_LORA, HEADS_A, LANES - QK_NOPE - QK_ROPE), F32)
    uq_pad = jnp.concatenate([uq, pad], -1)
    uq_rot = jnp.concatenate([jnp.zeros((Q_LORA, HEADS_A, QK_NOPE), F32),
                              _rotate_half_cols(uq[..., QK_NOPE:], QK_ROPE // 2), pad], -1)
    wuq = jnp.concatenate([uq_pad.reshape(Q_LORA, QK_A_PAD), uq_rot.reshape(Q_LORA, QK_A_PAD)], -1).astype(BF16)

    ukv = w_ukv.reshape(KV_LORA, HEADS_A, QK_NOPE + HEAD_DIM)
    uk_pad = jnp.concatenate([ukv[..., :QK_NOPE], jnp.zeros((KV_LORA, HEADS_A, LANES - QK_NOPE), F32)], -1)
    wukv = jnp.concatenate([uk_pad.reshape(KV_LORA, QK_A_PAD), ukv[..., QK_NOPE:].reshape(KV_LORA, WIDTH_A)],
                           -1).astype(BF16)
    return wbig, wuq, wukv


def kernel(x, g_mix, w_in, q_norm, w_uq, kv_norm, w_ukv, rpb, out_norm_a, out_norm_b, out_norm_c, w_out, g_mlp,
           w_mlp_in, w_mlp_out, g_final):
    b, s, d = x.shape
    depth = w_in.shape[0]
    t = b * s
    assert s % (2 * BLOCKS_PER_STEP * GRID_W) == 0 and s // GRID_W >= NA_ROWS
    tm_proj = min(512, s)
    tm_mlp = min(512, s)
    tq_a = min(256, s)
    tk_a = min(512, s)

    tabs = _rope_tables(s)
    na_bias = _na_bias_call(rpb).reshape(depth, HEADS_C, NA_ROWS, GRID_W, NA_ROWS * GRID_W)
    row = lambda a: a.reshape(1, -1)

    x2 = x.reshape(t, d)
    for l in range(depth):
        wbig, wuq, wukv = _layer_weights(w_in[l], w_uq[l], w_ukv[l])
        qa, ka, va, qb, kb, vb, qc, kc, vc, *dil_qkv = _proj_call(
            x2, row(g_mix[l]), wbig, row(q_norm[l]), wuq, row(kv_norm[l]), wukv, tabs, s, tm_proj)
        seq3 = lambda a: a.reshape(b, s, a.shape[-1])
        oa = _attn_a_call(qa, seq3(ka), va, tq_a, tk_a)
        ob1, l1, dilated = None, None, []
        for window, dil in DILATED_PAIRS:
            half = window // (2 * dil)
            if dil == 1:
                unit = lambda a: a.reshape(b, 1, s, WIDTH_B)
                o_i, lse_i = _attn_b_call(unit(qb), unit(kb), unit(vb), half, 128)
                ob1, l1 = o_i.reshape(t, WIDTH_B), lse_i.reshape(t, WIDTH_B)
            else:
                di = 3 * _DILATIONS.index(dil)
                dilated.append(_attn_b_call(*dil_qkv[di:di + 3], half, 128))
        oc = _attn_c_call(seq3(qc), seq3(kc), seq3(vc), na_bias[l]).reshape(t, WIDTH_C)
        x2 = _out_mlp_call(x2, oa, ob1, l1, dilated, oc, row(out_norm_a[l]), row(out_norm_b[l]), row(out_norm_c[l]),
                           w_out[l].astype(BF16), row(g_mlp[l]), w_mlp_in[l].astype(BF16),
                           w_mlp_out[l].astype(BF16), row(g_final), l == depth - 1, s, tm_mlp,
                           min(1024, w_mlp_in.shape[2]))
    return x2.reshape(b, s, d)
```

```python
import functools

import jax
import jax.numpy as jnp
from jax import lax
from jax.experimental import pallas as pl
from jax.experimental.pallas import tpu as pltpu

HEAD_DIM = 64
LANES = 128
BF16_ROWS = 16
HEADS_A = 6
HEADS_B = 6
HEADS_C = 4
Q_LORA = 256
KV_LORA = 128
QK_NOPE = 64
QK_ROPE = 32
DILATED_PAIRS = ((128, 1), (512, 4), (2048, 16))
_DILATIONS = tuple(dil for _, dil in DILATED_PAIRS if dil > 1)
GRID_W = 64
NA_ROWS = 8
NA_COLS = 16
ROPE_THETA = 10000.0
NORM_EPS = 1e-6
NEG_INF = -1e30
LOG2_E = 1.4426950408889634
PROJ_SUBTILES = 2
MLP_SUBTILES = 2
BLOCKS_PER_STEP = 4

WIDTH_A = HEADS_A * HEAD_DIM
WIDTH_B = HEADS_B * HEAD_DIM
WIDTH_C = HEADS_C * HEAD_DIM
QK_A_PAD = HEADS_A * LANES
RPB_PER_HEAD = (2 * NA_ROWS - 1) * (2 * NA_COLS - 1)

VMEM_LIMIT = 56 * 1024 * 1024

BF16 = jnp.bfloat16
F32 = jnp.float32


def _rms(x, g):
    return x * lax.rsqrt(jnp.mean(x * x, axis=-1, keepdims=True) + NORM_EPS) * g


def _dot(a, b):
    return jnp.dot(a, b, preferred_element_type=F32)


def _dot_nt(a, b):
    return lax.dot_general(a, b, (((1,), (1,)), ((), ())), preferred_element_type=F32)


def _lane_is_first_head(shape):
    return lax.broadcasted_iota(jnp.int32, shape, len(shape) - 1) < HEAD_DIM


def _stack_heads(q, first):
    return jnp.concatenate([_keep_head(q, first, 0), _keep_head(q, first, 1)], axis=0)


def _keep_head(q, first, j):
    zero = jnp.zeros_like(q)
    return jnp.where(first, q, zero) if j == 0 else jnp.where(first, zero, q)


_C_CQ = 0
_C_CKV = _C_CQ + Q_LORA
_C_KPE = _C_CKV + KV_LORA
_C_KPR = _C_KPE + LANES
_C_QB = _C_KPR + LANES
_C_KB = _C_QB + WIDTH_B
_C_VB = _C_KB + WIDTH_B
_C_QC = _C_VB + WIDTH_B
_C_KC = _C_QC + WIDTH_C
_C_VC = _C_KC + WIDTH_C
_C_END = _C_VC + WIDTH_C

_T_COSQ, _T_SINQ, _T_COSK, _T_SINK, _T_COSB, _T_SINB = range(6)


def _proj_kernel(x_ref, g_ref, wbig_ref, qn_ref, wuq_ref, kvn_ref, wukv_ref, tab_ref,
                 qa_ref, ka_ref, va_ref, qb_ref, kb_ref, vb_ref, qc_ref, kc_ref, vc_ref, *rest):
    dil_refs, stage_ref = rest[:-1], rest[-1]
    tm = x_ref.shape[0]
    hm = tm // PROJ_SUBTILES
    nblk = WIDTH_B // LANES
    first_half = (lax.broadcasted_iota(jnp.int32, (hm, LANES), 1) % HEAD_DIM) < HEAD_DIM // 2

    for sub in range(PROJ_SUBTILES):
        rs = slice(sub * hm, (sub + 1) * hm)

        def tab(i):
            return tab_ref[rs, i * LANES:(i + 1) * LANES]

        h = _rms(x_ref[rs, :], g_ref[...]).astype(BF16)
        proj = _dot(h, wbig_ref[...])

        cqn = _rms(proj[:, _C_CQ:_C_CKV], qn_ref[...]).astype(BF16)
        qa2 = _dot(cqn, wuq_ref[...])
        ckvn = _rms(proj[:, _C_CKV:_C_KPE], kvn_ref[...]).astype(BF16)
        kv2 = _dot(ckvn, wukv_ref[...])
        kpe = proj[:, _C_KPE:_C_KPR] * tab(_T_COSK) + proj[:, _C_KPR:_C_QB] * tab(_T_SINK)
        cosq, sinq = tab(_T_COSQ), tab(_T_SINQ)
        for hd in range(HEADS_A):
            sl = slice(hd * LANES, (hd + 1) * LANES)
            rot = slice(QK_A_PAD + hd * LANES, QK_A_PAD + (hd + 1) * LANES)
            qa_ref[0, sl, rs] = (qa2[:, sl] * cosq + qa2[:, rot] * sinq).T.astype(BF16)
            ka_ref[rs, sl] = (kv2[:, sl] + kpe).astype(BF16)
        for blk in range(WIDTH_A // LANES):
            sl = slice(blk * LANES, (blk + 1) * LANES)
            va_ref[0, sl, rs] = kv2[:, QK_A_PAD + blk * LANES:QK_A_PAD + (blk + 1) * LANES].T.astype(BF16)

        cosb, sinb = tab(_T_COSB), tab(_T_SINB)
        for ti, (src, dst) in enumerate(((_C_QB, qb_ref), (_C_KB, kb_ref), (_C_VB, vb_ref))):
            for blk in range(nblk):
                xb = proj[:, src + blk * LANES:src + (blk + 1) * LANES]
                if dst is not vb_ref:
                    swapped = jnp.where(first_half, pltpu.roll(xb, LANES - HEAD_DIM // 2, 1),
                                        pltpu.roll(xb, HEAD_DIM // 2, 1))
                    xb = xb * cosb + swapped * sinb
                    if dst is qb_ref:
                        xb = xb * LOG2_E
                dst[rs, blk * LANES:(blk + 1) * LANES] = xb.astype(BF16)
                stage_ref[ti * nblk + blk, rs, :] = xb
        for di, dil in enumerate(_DILATIONS):
            rows = hm // dil
            for ti in range(3):
                dref = dil_refs[di * 3 + ti]
                for r in range(dil):
                    for blk in range(nblk):
                        dref[0, r, sub * rows:(sub + 1) * rows, blk * LANES:(blk + 1) * LANES] = (
                            stage_ref[ti * nblk + blk, pl.ds(sub * hm + r, rows, stride=dil), :].astype(BF16))

        qc_ref[rs, :] = proj[:, _C_QC:_C_KC].astype(BF16)
        kc_ref[rs, :] = proj[:, _C_KC:_C_VC].astype(BF16)
        vc_ref[rs, :] = proj[:, _C_VC:_C_END].astype(BF16)


def _layer_spec(w, layer, **kwargs):
    return pl.BlockSpec((None,) + w.shape[1:], lambda i: (layer,) + (0,) * (w.ndim - 1), **kwargs)


def _proj_call(x2, g, wbig, qn, wuq, kvn, wukv, tabs, seq, tm, layer):
    t, d = x2.shape
    nseq = seq // tm
    row = lambda i: (i, 0)
    const = lambda i: (0, 0)
    widths = (QK_A_PAD, QK_A_PAD, WIDTH_A, WIDTH_B, WIDTH_B, WIDTH_B, WIDTH_C, WIDTH_C, WIDTH_C)
    out_specs = [pl.BlockSpec((tm, w), row) for w in widths]
    out_shape = [jax.ShapeDtypeStruct((t, w), BF16) for w in widths]
    for idx in (0, 2):
        out_specs[idx] = pl.BlockSpec((1, widths[idx], tm), lambda i: (i // nseq, 0, i % nseq))
        out_shape[idx] = jax.ShapeDtypeStruct((t // seq, widths[idx], seq), BF16)
    for dil in _DILATIONS:
        assert tm % (BF16_ROWS * dil * PROJ_SUBTILES) == 0
        for _ in range(3):
            out_specs.append(pl.BlockSpec((1, dil, tm // dil, WIDTH_B), lambda i: (i // nseq, 0, i % nseq, 0)))
            out_shape.append(jax.ShapeDtypeStruct((t // seq, dil, seq // dil, WIDTH_B), BF16))
    return pl.pallas_call(
        _proj_kernel,
        grid=(t // tm,),
        in_specs=[
            pl.BlockSpec((tm, d), row),
            pl.BlockSpec((1, d), const),
            _layer_spec(wbig, layer),
            pl.BlockSpec((1, Q_LORA), const),
            _layer_spec(wuq, layer),
            pl.BlockSpec((1, KV_LORA), const),
            _layer_spec(wukv, layer),
            pl.BlockSpec((tm, tabs.shape[1]), lambda i: (i % nseq, 0)),
        ],
        out_specs=out_specs,
        out_shape=out_shape,
        scratch_shapes=[pltpu.VMEM((3 * WIDTH_B // LANES, tm, LANES), F32)],
        compiler_params=pltpu.CompilerParams(dimension_semantics=("parallel",),
                                             vmem_limit_bytes=VMEM_LIMIT),
    )(x2, g, wbig, qn, wuq, kvn, wukv, tabs)


def _attn_a_kernel(qt_ref, k_ref, vt_ref, o_ref, a0_ref, a1_ref, b0_ref, b1_ref, *, tq, tk):
    s = k_ref.shape[1]
    nk, nq = s // tk, s // tq
    st_a, st_b = (a0_ref, a1_ref), (b0_ref, b1_ref)
    ones = jnp.ones((BF16_ROWS, tk), BF16)

    def scores(qb, c, st_ref):
        q0 = pl.multiple_of(qb * tq, tq)
        ks = pl.multiple_of(c * tk, tk)
        for j in range(2):
            st_ref[j, :, :tq] = _dot(k_ref[0, pl.ds(ks, tk), j * LANES:(j + 1) * LANES],
                                     qt_ref[0, j * LANES:(j + 1) * LANES, pl.ds(q0, tq)])

    def accumulate(c, st_ref, carry):
        ks = pl.multiple_of(c * tk, tk)
        stats = []
        for j in range(2):
            m = carry[j][0]
            st = st_ref[j, :, :tq]
            m_new = jnp.maximum(m, jnp.max(st, axis=0, keepdims=True))
            stats.append((m_new, jnp.exp2(m - m_new), jnp.exp2(st - m_new).astype(BF16)))
        new = []
        for j in range(2):
            m_new, alpha, pt = stats[j]
            vt = jnp.concatenate([vt_ref[0, j * HEAD_DIM:(j + 1) * HEAD_DIM, pl.ds(ks, tk)], ones], axis=0)
            new.append((m_new, alpha * carry[j][1] + _dot(vt, pt)))
        return tuple(new)

    def half_step(qb, c, src, dst, carry, wrap=False):
        for u in range(2):
            if wrap:
                scores(jnp.minimum(qb + 1, nq - 1), u, dst[u])
            else:
                scores(qb, c + 2 + u, dst[u])
            carry = accumulate(c + u, src[u], carry)
        return carry

    def q_block(qb, _):
        def body(i, carry):
            carry = half_step(qb, 4 * i, st_a, st_b, carry)
            return half_step(qb, 4 * i + 2, st_b, st_a, carry)

        init = tuple((jnp.full((1, tq), NEG_INF, F32), jnp.zeros((HEAD_DIM + BF16_ROWS, tq), F32))
                     for _ in range(2))
        carry = lax.fori_loop(0, nk // 4 - 1, body, init)
        carry = half_step(qb, nk - 4, st_a, st_b, carry)
        res = half_step(qb, nk - 2, st_b, st_a, carry, wrap=True)
        q0 = pl.multiple_of(qb * tq, tq)
        for j in range(2):
            acc = res[j][1]
            o_ref[0, j * HEAD_DIM:(j + 1) * HEAD_DIM, pl.ds(q0, tq)] = (
                acc[:HEAD_DIM] / acc[HEAD_DIM:HEAD_DIM + 1]).astype(o_ref.dtype)
        return 0

    scores(0, 0, st_a[0])
    scores(0, 1, st_a[1])
    lax.fori_loop(0, nq, q_block, 0)


def _attn_a_call(qt, ka, vt, tq, tk):
    b, s, _ = ka.shape
    pairs = HEADS_A // 2
    assert s % (4 * tk) == 0
    return pl.pallas_call(
        functools.partial(_attn_a_kernel, tq=tq, tk=tk),
        grid=(b, pairs),
        in_specs=[
            pl.BlockSpec((1, 2 * LANES, s), lambda bi, p: (bi, p, 0)),
            pl.BlockSpec((1, s, 2 * LANES), lambda bi, p: (bi, 0, p)),
            pl.BlockSpec((1, LANES, s), lambda bi, p: (bi, p, 0)),
        ],
        out_specs=pl.BlockSpec((1, LANES, s), lambda bi, p: (bi, p, 0)),
        out_shape=jax.ShapeDtypeStruct((b, WIDTH_A, s), BF16),
        scratch_shapes=[pltpu.VMEM((2, tk, tq + LANES), F32) for _ in range(4)],
        compiler_params=pltpu.CompilerParams(dimension_semantics=("parallel", "parallel"),
                                             vmem_limit_bytes=VMEM_LIMIT),
    )(qt, ka, vt)


def _attn_b_kernel(q_ref, k_ref, v_ref, o_ref, lse_ref, band_ref, sa_ref, sb_ref, *, tq, half, n):
    total = q_ref.shape[0]
    kw = min(tq + 2 * half, n)
    first = _lane_is_first_head((tq, LANES))

    group = min(BLOCKS_PER_STEP, total // tq)
    diff = (lax.broadcasted_iota(jnp.int32, (tq, kw), 0) - lax.broadcasted_iota(jnp.int32, (tq, kw), 1))
    for var in range(band_ref.shape[0]):
        band_ref[var] = jnp.where(jnp.abs(diff + var * half) <= half, 0.0, NEG_INF)

    n_groups = total // (tq * group)

    def window(g, u):
        q0 = pl.multiple_of((g * group + u) * tq, tq)
        seg0 = (q0 // n) * n
        return q0, pl.multiple_of(jnp.clip(q0 - half, seg0, seg0 + n - kw), half)

    def issue(g, st_ref):
        g = jnp.minimum(g, n_groups - 1)
        for u in range(group):
            q0, ks = window(g, u)
            st_ref[u] = _dot_nt(_stack_heads(q_ref[pl.ds(q0, tq), :], first), k_ref[pl.ds(ks, kw), :])

    def consume(g, st_ref):
        probs = []
        for u in range(group):
            q0, ks = window(g, u)
            band = band_ref[(q0 - ks) // half]
            row = []
            for j in range(2):
                s = st_ref[u, j * tq:(j + 1) * tq, :] + band
                m = jnp.max(s, axis=-1, keepdims=True)
                p = jnp.exp2(s - m)
                den = jnp.sum(p, axis=-1, keepdims=True)
                row.append((p.astype(BF16), den, m + jnp.log2(den)))
            probs.append(row)
        for u, row in enumerate(probs):
            q0, ks = window(g, u)
            o2 = _dot(jnp.concatenate([row[0][0], row[1][0]], axis=0), v_ref[pl.ds(ks, kw), :])
            outs = [o2[j * tq:(j + 1) * tq] / row[j][1] for j in range(2)]
            lses = [jnp.broadcast_to(lse, (tq, LANES)) for (_, _, lse) in row]
            o_ref[pl.ds(q0, tq), :] = jnp.where(first, outs[0], outs[1]).astype(o_ref.dtype)
            lse_ref[pl.ds(q0, tq), :] = jnp.where(first, lses[0], lses[1])

    def body(it, carry):
        issue(2 * it + 1, sb_ref)
        consume(2 * it, sa_ref)
        issue(2 * it + 2, sa_ref)
        consume(2 * it + 1, sb_ref)
        return carry

    issue(0, sa_ref)
    lax.fori_loop(0, n_groups // 2, body, 0)


def _attn_b_call(q, k, v, half, tq):
    b, dil, n, w = q.shape
    pairs = w // LANES
    tq = min(tq, n)
    kw = min(tq + 2 * half, n)
    group = min(BLOCKS_PER_STEP, dil * n // tq)
    assert n % tq == 0 and (kw - tq) % half == 0 and (dil * n) % (2 * group * tq) == 0
    flat = lambda a: a.reshape(b, dil * n, w)
    spec = pl.BlockSpec((None, dil * n, LANES), lambda bi, p: (bi, 0, p))
    o, lse = pl.pallas_call(
        functools.partial(_attn_b_kernel, tq=tq, half=half, n=n),
        grid=(b, pairs),
        in_specs=[spec, spec, spec],
        out_specs=[spec, spec],
        out_shape=[jax.ShapeDtypeStruct((b, dil * n, w), BF16), jax.ShapeDtypeStruct((b, dil * n, w), F32)],
        scratch_shapes=[pltpu.VMEM(((kw - tq) // half + 1, tq, kw), F32),
                        pltpu.VMEM((group, 2 * tq, kw), F32), pltpu.VMEM((group, 2 * tq, kw), F32)],
        compiler_params=pltpu.CompilerParams(dimension_semantics=("parallel", "parallel"),
                                             vmem_limit_bytes=VMEM_LIMIT),
    )(flat(q), flat(k), flat(v))
    return o.reshape(q.shape), lse.reshape(q.shape)


def _na_bias_kernel(rpb_ref, o_ref):
    base = pl.program_id(0) * RPB_PER_HEAD
    shape = (GRID_W, LANES)
    lane = lax.broadcasted_iota(jnp.int32, shape, 1)
    p = lax.broadcasted_iota(jnp.int32, shape, 0)
    c = lane % GRID_W
    upper = lane >= GRID_W
    c_start = jnp.clip(p - NA_COLS // 2, 0, GRID_W - NA_COLS)
    col_ok = (c >= c_start) & (c < c_start + NA_COLS)
    dc = c - p + (NA_COLS - 1)
    n_dc = 2 * NA_COLS - 1
    for v in range(NA_ROWS):
        for m in range(NA_ROWS * GRID_W // LANES):
            a_lo = 2 * m - v + (NA_ROWS - 1)
            acc = jnp.full(shape, NEG_INF, F32)
            for b in range(n_dc):
                val = jnp.where(upper, rpb_ref[base + (a_lo + 1) * n_dc + b], rpb_ref[base + a_lo * n_dc + b])
                acc = jnp.where(dc == b, val, acc)
            o_ref[0, v, :, m * LANES:(m + 1) * LANES] = jnp.where(col_ok, acc, NEG_INF)


def _na_bias_call(rpb):
    nh = rpb.shape[0] * rpb.shape[1]
    return pl.pallas_call(
        _na_bias_kernel,
        grid=(nh,),
        in_specs=[pl.BlockSpec(memory_space=pltpu.SMEM)],
        out_specs=pl.BlockSpec((1, NA_ROWS, GRID_W, NA_ROWS * GRID_W), lambda g: (g, 0, 0, 0)),
        out_shape=jax.ShapeDtypeStruct((nh, NA_ROWS, GRID_W, NA_ROWS * GRID_W), F32),
    )(rpb.reshape(-1))


def _attn_c_kernel(q_ref, k_ref, v_ref, bias_ref, o_ref, sa_ref, sb_ref):
    rows = q_ref.shape[1] // GRID_W
    win = NA_ROWS * GRID_W
    first = _lane_is_first_head((GRID_W, LANES))
    group = sa_ref.shape[0]
    n_groups = rows // group

    def window(g, u):
        r = g * group + u
        r_start = jnp.clip(r - NA_ROWS // 2, 0, rows - NA_ROWS)
        return pl.multiple_of(r * GRID_W, GRID_W), pl.multiple_of(r_start * GRID_W, GRID_W), r - r_start

    def issue(g, st_ref):
        g = jnp.minimum(g, n_groups - 1)
        for u in range(group):
            q0, ks, _ = window(g, u)
            st_ref[u] = _dot_nt(_stack_heads(q_ref[0, pl.ds(q0, GRID_W), :], first), k_ref[0, pl.ds(ks, win), :])

    def consume(g, st_ref):
        probs = []
        for u in range(group):
            variant = window(g, u)[2]
            row = []
            for j in range(2):
                s = st_ref[u, j * GRID_W:(j + 1) * GRID_W, :] + bias_ref[j, variant]
                m = jnp.max(s, axis=-1, keepdims=True)
                p = jnp.exp(s - m)
                row.append((p.astype(BF16), jnp.sum(p, axis=-1, keepdims=True)))
            probs.append(row)
        for u, row in enumerate(probs):
            q0, ks, _ = window(g, u)
            o2 = _dot(jnp.concatenate([row[0][0], row[1][0]], axis=0), v_ref[0, pl.ds(ks, win), :])
            outs = [o2[j * GRID_W:(j + 1) * GRID_W] / row[j][1] for j in range(2)]
            o_ref[0, pl.ds(q0, GRID_W), :] = jnp.where(first, outs[0], outs[1]).astype(o_ref.dtype)

    def body(it, carry):
        issue(2 * it + 1, sb_ref)
        consume(2 * it, sa_ref)
        issue(2 * it + 2, sa_ref)
        consume(2 * it + 1, sb_ref)
        return carry

    issue(0, sa_ref)
    lax.fori_loop(0, n_groups // 2, body, 0)


def _attn_c_call(q, k, v, bias, layer):
    b, s, w = q.shape
    pairs = w // LANES
    spec = pl.BlockSpec((1, s, LANES), lambda bi, p: (bi, 0, p))
    return pl.pallas_call(
        _attn_c_kernel,
        grid=(b, pairs),
        in_specs=[spec, spec, spec,
                  pl.BlockSpec((2,) + bias.shape[1:], lambda bi, p: (layer * pairs + p, 0, 0, 0))],
        out_specs=spec,
        out_shape=jax.ShapeDtypeStruct((b, s, w), BF16),
        scratch_shapes=[pltpu.VMEM((BLOCKS_PER_STEP, 2 * GRID_W, NA_ROWS * GRID_W), F32) for _ in range(2)],
        compiler_params=pltpu.CompilerParams(dimension_semantics=("parallel", "parallel"),
                                             vmem_limit_bytes=VMEM_LIMIT),
    )(q, k, v, bias)


def _out_kernel(x_ref, oa_ref, ob1_ref, l1_ref, *rest):
    nd = len(_DILATIONS)
    dil_refs = rest[:2 * nd]
    oc_ref, ga_ref, gb_ref, gc_ref, w_ref, o_ref, stage_ref = rest[2 * nd:]
    tm = x_ref.shape[0]
    nblk = WIDTH_B // LANES
    oa = jnp.concatenate([oa_ref[0, blk * LANES:(blk + 1) * LANES, :].astype(F32).T
                          for blk in range(WIDTH_A // LANES)], axis=-1)
    na = _rms(oa, ga_ref[...]).astype(BF16)
    outs, lses = [ob1_ref[...].astype(F32)], [l1_ref[...]]
    for di, dil in enumerate(_DILATIONS):
        rows = tm // dil
        for which, acc in ((0, outs), (1, lses)):
            src = dil_refs[2 * di + which]
            slot = (2 * di + which) * nblk
            for r in range(dil):
                for blk in range(nblk):
                    stage_ref[slot + blk, pl.ds(r, rows, stride=dil), :] = (
                        src[0, r, :, blk * LANES:(blk + 1) * LANES].astype(F32))
            acc.append(jnp.concatenate([stage_ref[slot + blk] for blk in range(nblk)], axis=-1))
    lmax = functools.reduce(jnp.maximum, lses)
    es = [jnp.exp2(l - lmax) for l in lses]
    ob = sum(e * o for e, o in zip(es, outs)) / sum(es)
    nb = _rms(ob, gb_ref[...]).astype(BF16)
    nc = _rms(oc_ref[...].astype(F32), gc_ref[...]).astype(BF16)
    acc = _dot(na, w_ref[0:WIDTH_A, :])
    acc += _dot(nb, w_ref[WIDTH_A:WIDTH_A + WIDTH_B, :])
    acc += _dot(nc, w_ref[WIDTH_A + WIDTH_B:, :])
    o_ref[...] = x_ref[...] + acc


def _out_call(x2, oa, ob1, l1, dilated, oc, ga, gb, gc, w, seq, tm, layer):
    t, d = x2.shape
    nseq = seq // tm
    row = lambda i: (i, 0)
    const = lambda i: (0, 0)
    rs = lambda width: pl.BlockSpec((tm, width), row)
    cs = lambda width: pl.BlockSpec((1, width), const)
    dil_specs, dil_args = [], []
    for dil, pair in zip(_DILATIONS, dilated):
        for a in pair:
            dil_specs.append(pl.BlockSpec((1, dil, tm // dil, WIDTH_B), lambda i: (i // nseq, 0, i % nseq, 0)))
            dil_args.append(a)
    return pl.pallas_call(
        _out_kernel,
        grid=(t // tm,),
        in_specs=[rs(d), pl.BlockSpec((1, WIDTH_A, tm), lambda i: (i // nseq, 0, i % nseq)),
                  rs(WIDTH_B), rs(WIDTH_B), *dil_specs,
                  rs(WIDTH_C), cs(WIDTH_A), cs(WIDTH_B), cs(WIDTH_C), _layer_spec(w, layer)],
        out_specs=rs(d),
        out_shape=jax.ShapeDtypeStruct((t, d), F32),
        scratch_shapes=[pltpu.VMEM((2 * len(_DILATIONS) * WIDTH_B // LANES, tm, LANES), F32)],
        compiler_params=pltpu.CompilerParams(dimension_semantics=("parallel",),
                                             vmem_limit_bytes=VMEM_LIMIT),
    )(x2, oa, ob1, l1, *dil_args, oc, ga, gb, gc, w)


def _mlp_kernel(x_ref, g_ref, w1_ref, w2_ref, gf_ref, o_ref, *, final_norm, tf):
    tm = x_ref.shape[0]
    hm = tm // MLP_SUBTILES
    rows = [slice(sub * hm, (sub + 1) * hm) for sub in range(MLP_SUBTILES)]
    hs = [_rms(x_ref[rs, :], g_ref[...]).astype(BF16) for rs in rows]
    accs = [None] * MLP_SUBTILES
    for f in range(w1_ref.shape[1] // tf):
        cols = slice(f * tf, (f + 1) * tf)
        us = [jnp.maximum(_dot(h, w1_ref[:, cols]), 0.0) for h in hs]
        for sub, u in enumerate(us):
            y = _dot((u * u).astype(BF16), w2_ref[cols, :])
            accs[sub] = y if accs[sub] is None else accs[sub] + y
    for rs, acc in zip(rows, accs):
        y = x_ref[rs, :] + acc
        if final_norm:
            y = _rms(y, gf_ref[...])
        o_ref[rs, :] = y


def _mlp_call(x2, g, w1, w2, gf, final_norm, tm, tf, layer):
    t, d = x2.shape
    return pl.pallas_call(
        functools.partial(_mlp_kernel, final_norm=final_norm, tf=tf),
        grid=(t // tm,),
        in_specs=[
            pl.BlockSpec((tm, d), lambda i: (i, 0)),
            pl.BlockSpec((1, d), lambda i: (0, 0)),
            _layer_spec(w1, layer, pipeline_mode=pl.Buffered(1)),
            _layer_spec(w2, layer, pipeline_mode=pl.Buffered(1)),
            pl.BlockSpec((1, d), lambda i: (0, 0)),
        ],
        out_specs=pl.BlockSpec((tm, d), lambda i: (i, 0)),
        out_shape=jax.ShapeDtypeStruct((t, d), F32),
        compiler_params=pltpu.CompilerParams(dimension_semantics=("parallel",),
                                             vmem_limit_bytes=VMEM_LIMIT),
    )(x2, g, w1, w2, gf)


def _rotate_half_cols(w, half):
    return jnp.concatenate([-w[..., half:], w[..., :half]], axis=-1)


def _rope_tables(seq):
    pos = jnp.arange(seq, dtype=F32)

    def cos_sin(half):
        inv_freq = ROPE_THETA ** (-jnp.arange(half, dtype=F32) / half)
        ang = pos[:, None] * inv_freq[None, :]
        return jnp.cos(ang), jnp.sin(ang)

    ca, sa = cos_sin(QK_ROPE // 2)
    ca2, sa2 = jnp.concatenate([ca, ca], -1), jnp.concatenate([sa, sa], -1)
    ones = jnp.ones((seq, QK_NOPE), F32)
    zeros = jnp.zeros((seq, QK_NOPE), F32)
    tail = jnp.zeros((seq, LANES - QK_NOPE - QK_ROPE), F32)
    scale_a = (QK_NOPE + QK_ROPE) ** -0.5 * LOG2_E
    cosq = jnp.concatenate([ones, ca2, tail], -1) * scale_a
    sinq = jnp.concatenate([zeros, sa2, tail], -1) * scale_a
    cosk = jnp.concatenate([zeros, ca2, tail], -1)
    sink = jnp.concatenate([zeros, sa2, tail], -1)
    cb, sb = cos_sin(HEAD_DIM // 2)
    cosb = jnp.concatenate([cb, cb, cb, cb], -1)
    sinb = jnp.concatenate([-sb, sb, -sb, sb], -1)
    return jnp.concatenate([cosq, sinq, cosk, sink, cosb, sinb], -1)


def _layer_weights(w_in, w_uq, w_ukv):
    d = w_in.shape[0]
    scale = HEAD_DIM ** -0.5
    c_b = Q_LORA + KV_LORA + QK_ROPE
    c_c = c_b + 3 * WIDTH_B
    w_kpe = w_in[:, Q_LORA + KV_LORA:c_b]

    def place(w):
        return jnp.concatenate([jnp.zeros((d, QK_NOPE), F32), w, jnp.zeros((d, LANES - QK_NOPE - QK_ROPE), F32)], -1)

    wbig = jnp.concatenate([
        w_in[:, :Q_LORA + KV_LORA],
        place(w_kpe), place(_rotate_half_cols(w_kpe, QK_ROPE // 2)),
        w_in[:, c_b:c_b + WIDTH_B] * scale, w_in[:, c_b + WIDTH_B:c_c],
        w_in[:, c_c:c_c + WIDTH_C] * scale, w_in[:, c_c + WIDTH_C:],
    ], -1).astype(BF16)

    uq = w_uq.reshape(Q_LORA, HEADS_A, QK_NOPE + QK_ROPE)
    pad = jnp.zeros((Q_LORA, HEADS_A, LANES - QK_NOPE - QK_ROPE), F32)
    uq_pad = jnp.concatenate([uq, pad], -1)
    uq_rot = jnp.concatenate([jnp.zeros((Q_LORA, HEADS_A, QK_NOPE), F32),
                              _rotate_half_cols(uq[..., QK_NOPE:], QK_ROPE // 2), pad], -1)
    wuq = jnp.concatenate([uq_pad.reshape(Q_LORA, QK_A_PAD), uq_rot.reshape(Q_LORA, QK_A_PAD)], -1).astype(BF16)

    ukv = w_ukv.reshape(KV_LORA, HEADS_A, QK_NOPE + HEAD_DIM)
    uk_pad = jnp.concatenate([ukv[..., :QK_NOPE], jnp.zeros((KV_LORA, HEADS_A, LANES - QK_NOPE), F32)], -1)
    wukv = jnp.concatenate([uk_pad.reshape(KV_LORA, QK_A_PAD), ukv[..., QK_NOPE:].reshape(KV_LORA, WIDTH_A)],
                           -1).astype(BF16)
    return wbig, wuq, wukv


def kernel(x, g_mix, w_in, q_norm, w_uq, kv_norm, w_ukv, rpb, out_norm_a, out_norm_b, out_norm_c, w_out, g_mlp,
           w_mlp_in, w_mlp_out, g_final):
    b, s, d = x.shape
    depth = w_in.shape[0]
    t = b * s
    assert s % (2 * BLOCKS_PER_STEP * GRID_W) == 0 and s // GRID_W >= NA_ROWS
    tm_proj = min(512, s)
    tm_out = min(512, s)
    tm_mlp = min(1024, t)
    tq_a = min(256, s)
    tk_a = min(512, s)

    tabs = _rope_tables(s)
    na_bias = _na_bias_call(rpb)
    row = lambda a: a.reshape(1, -1)

    wbig, wuq, wukv = jax.vmap(_layer_weights)(w_in, w_uq, w_ukv)
    w_out_b, w1_b, w2_b = w_out.astype(BF16), w_mlp_in.astype(BF16), w_mlp_out.astype(BF16)

    x2 = x.reshape(t, d)
    for l in range(depth):
        qa, ka, va, qb, kb, vb, qc, kc, vc, *dil_qkv = _proj_call(
            x2, row(g_mix[l]), wbig, row(q_norm[l]), wuq, row(kv_norm[l]), wukv, tabs, s, tm_proj, l)
        seq3 = lambda a: a.reshape(b, s, a.shape[-1])
        oa = _attn_a_call(qa, seq3(ka), va, tq_a, tk_a)
        ob1, l1, dilated = None, None, []
        for window, dil in DILATED_PAIRS:
            half = window // (2 * dil)
            if dil == 1:
                unit = lambda a: a.reshape(b, 1, s, WIDTH_B)
                o_i, lse_i = _attn_b_call(unit(qb), unit(kb), unit(vb), half, 128)
                ob1, l1 = o_i.reshape(t, WIDTH_B), lse_i.reshape(t, WIDTH_B)
            else:
                di = 3 * _DILATIONS.index(dil)
                dilated.append(_attn_b_call(*dil_qkv[di:di + 3], half, 128))
        oc = _attn_c_call(seq3(qc), seq3(kc), seq3(vc), na_bias, l).reshape(t, WIDTH_C)
        x2 = _out_call(x2, oa, ob1, l1, dilated, oc, row(out_norm_a[l]), row(out_norm_b[l]), row(out_norm_c[l]),
                       w_out_b, s, tm_out, l)
        x2 = _mlp_call(x2, row(g_mlp[l]), w1_b, w2_b, row(g_final), l == depth - 1, tm_mlp,
                       min(1024, w_mlp_in.shape[2]), l)
    return x2.reshape(b, s, d)
```

```python
import functools

import jax
import jax.numpy as jnp
from jax import lax
from jax.experimental import pallas as pl
from jax.experimental.pallas import tpu as pltpu

HEAD_DIM = 64
LANES = 128
BF16_ROWS = 16
HEADS_A = 6
HEADS_B = 6
HEADS_C = 4
Q_LORA = 256
KV_LORA = 128
QK_NOPE = 64
QK_ROPE = 32
DILATED_PAIRS = ((128, 1), (512, 4), (2048, 16))
_DILATIONS = tuple(dil for _, dil in DILATED_PAIRS if dil > 1)
GRID_W = 64
NA_ROWS = 8
NA_COLS = 16
ROPE_THETA = 10000.0
NORM_EPS = 1e-6
NEG_INF = -1e30
LOG2_E = 1.4426950408889634
PROJ_SUBTILES = 2
MLP_SUBTILES = 2
BLOCKS_PER_STEP = 4

WIDTH_A = HEADS_A * HEAD_DIM
WIDTH_B = HEADS_B * HEAD_DIM
WIDTH_C = HEADS_C * HEAD_DIM
QK_A_PAD = HEADS_A * LANES
RPB_PER_HEAD = (2 * NA_ROWS - 1) * (2 * NA_COLS - 1)

VMEM_LIMIT = 56 * 1024 * 1024

BF16 = jnp.bfloat16
F32 = jnp.float32


def _rms(x, g):
    return x * lax.rsqrt(jnp.mean(x * x, axis=-1, keepdims=True) + NORM_EPS) * g


def _dot(a, b):
    return jnp.dot(a, b, preferred_element_type=F32)


def _dot_nt(a, b):
    return lax.dot_general(a, b, (((1,), (1,)), ((), ())), preferred_element_type=F32)


def _lane_is_first_head(shape):
    return lax.broadcasted_iota(jnp.int32, shape, len(shape) - 1) < HEAD_DIM


def _stack_heads(q, first):
    return jnp.concatenate([_keep_head(q, first, 0), _keep_head(q, first, 1)], axis=0)


def _keep_head(q, first, j):
    zero = jnp.zeros_like(q)
    return jnp.where(first, q, zero) if j == 0 else jnp.where(first, zero, q)


_C_CQ = 0
_C_CKV = _C_CQ + Q_LORA
_C_KPE = _C_CKV + KV_LORA
_C_KPR = _C_KPE + LANES
_C_QB = _C_KPR + LANES
_C_KB = _C_QB + WIDTH_B
_C_VB = _C_KB + WIDTH_B
_C_QC = _C_VB + WIDTH_B
_C_KC = _C_QC + WIDTH_C
_C_VC = _C_KC + WIDTH_C
_C_END = _C_VC + WIDTH_C

_T_COSQ, _T_SINQ, _T_COSK, _T_SINK, _T_COSB, _T_SINB = range(6)


def _proj_kernel(x_ref, g_ref, wbig_ref, qn_ref, wuq_ref, kvn_ref, wukv_ref, tab_ref,
                 qa_ref, ka_ref, va_ref, qb_ref, kb_ref, vb_ref, qc_ref, kc_ref, vc_ref, *rest):
    dil_refs, stage_ref = rest[:-1], rest[-1]
    tm = x_ref.shape[0]
    hm = tm // PROJ_SUBTILES
    nblk = WIDTH_B // LANES
    first_half = (lax.broadcasted_iota(jnp.int32, (hm, LANES), 1) % HEAD_DIM) < HEAD_DIM // 2

    for sub in range(PROJ_SUBTILES):
        rs = slice(sub * hm, (sub + 1) * hm)

        def tab(i):
            return tab_ref[rs, i * LANES:(i + 1) * LANES]

        h = _rms(x_ref[rs, :], g_ref[...]).astype(BF16)
        proj = _dot(h, wbig_ref[...])

        cqn = _rms(proj[:, _C_CQ:_C_CKV], qn_ref[...]).astype(BF16)
        qa2 = _dot(cqn, wuq_ref[...])
        ckvn = _rms(proj[:, _C_CKV:_C_KPE], kvn_ref[...]).astype(BF16)
        kv2 = _dot(ckvn, wukv_ref[...])
        kpe = proj[:, _C_KPE:_C_KPR] * tab(_T_COSK) + proj[:, _C_KPR:_C_QB] * tab(_T_SINK)
        cosq, sinq = tab(_T_COSQ), tab(_T_SINQ)
        for hd in range(HEADS_A):
            sl = slice(hd * LANES, (hd + 1) * LANES)
            rot = slice(QK_A_PAD + hd * LANES, QK_A_PAD + (hd + 1) * LANES)
            qa_ref[0, sl, rs] = (qa2[:, sl] * cosq + qa2[:, rot] * sinq).T.astype(BF16)
            ka_ref[rs, sl] = (kv2[:, sl] + kpe).astype(BF16)
        for blk in range(WIDTH_A // LANES):
            sl = slice(blk * LANES, (blk + 1) * LANES)
            va_ref[0, sl, rs] = kv2[:, QK_A_PAD + blk * LANES:QK_A_PAD + (blk + 1) * LANES].T.astype(BF16)

        cosb, sinb = tab(_T_COSB), tab(_T_SINB)
        for ti, (src, dst) in enumerate(((_C_QB, qb_ref), (_C_KB, kb_ref), (_C_VB, vb_ref))):
            for blk in range(nblk):
                xb = proj[:, src + blk * LANES:src + (blk + 1) * LANES]
                if dst is not vb_ref:
                    swapped = jnp.where(first_half, pltpu.roll(xb, LANES - HEAD_DIM // 2, 1),
                                        pltpu.roll(xb, HEAD_DIM // 2, 1))
                    xb = xb * cosb + swapped * sinb
                    if dst is qb_ref:
                        xb = xb * LOG2_E
                dst[rs, blk * LANES:(blk + 1) * LANES] = xb.astype(BF16)
                stage_ref[ti * nblk + blk, rs, :] = xb
        for di, dil in enumerate(_DILATIONS):
            rows = hm // dil
            for ti in range(3):
                dref = dil_refs[di * 3 + ti]
                for r in range(dil):
                    for blk in range(nblk):
                        dref[0, r, sub * rows:(sub + 1) * rows, blk * LANES:(blk + 1) * LANES] = (
                            stage_ref[ti * nblk + blk, pl.ds(sub * hm + r, rows, stride=dil), :].astype(BF16))

        qc_ref[rs, :] = proj[:, _C_QC:_C_KC].astype(BF16)
        kc_ref[rs, :] = proj[:, _C_KC:_C_VC].astype(BF16)
        vc_ref[rs, :] = proj[:, _C_VC:_C_END].astype(BF16)


def _layer_spec(w, layer, **kwargs):
    return pl.BlockSpec((None,) + w.shape[1:], lambda i: (layer,) + (0,) * (w.ndim - 1), **kwargs)


def _proj_call(x2, g, wbig, qn, wuq, kvn, wukv, tabs, seq, tm, layer):
    t, d = x2.shape
    nseq = seq // tm
    row = lambda i: (i, 0)
    const = lambda i: (0, 0)
    widths = (QK_A_PAD, QK_A_PAD, WIDTH_A, WIDTH_B, WIDTH_B, WIDTH_B, WIDTH_C, WIDTH_C, WIDTH_C)
    out_specs = [pl.BlockSpec((tm, w), row) for w in widths]
    out_shape = [jax.ShapeDtypeStruct((t, w), BF16) for w in widths]
    for idx in (0, 2):
        out_specs[idx] = pl.BlockSpec((1, widths[idx], tm), lambda i: (i // nseq, 0, i % nseq))
        out_shape[idx] = jax.ShapeDtypeStruct((t // seq, widths[idx], seq), BF16)
    for dil in _DILATIONS:
        assert tm % (BF16_ROWS * dil * PROJ_SUBTILES) == 0
        for _ in range(3):
            out_specs.append(pl.BlockSpec((1, dil, tm // dil, WIDTH_B), lambda i: (i // nseq, 0, i % nseq, 0)))
            out_shape.append(jax.ShapeDtypeStruct((t // seq, dil, seq // dil, WIDTH_B), BF16))
    return pl.pallas_call(
        _proj_kernel,
        grid=(t // tm,),
        in_specs=[
            pl.BlockSpec((tm, d), row),
            pl.BlockSpec((1, d), const),
            _layer_spec(wbig, layer),
            pl.BlockSpec((1, Q_LORA), const),
            _layer_spec(wuq, layer),
            pl.BlockSpec((1, KV_LORA), const),
            _layer_spec(wukv, layer),
            pl.BlockSpec((tm, tabs.shape[1]), lambda i: (i % nseq, 0)),
        ],
        out_specs=out_specs,
        out_shape=out_shape,
        scratch_shapes=[pltpu.VMEM((3 * WIDTH_B // LANES, tm, LANES), F32)],
        compiler_params=pltpu.CompilerParams(dimension_semantics=("parallel",),
                                             vmem_limit_bytes=VMEM_LIMIT),
    )(x2, g, wbig, qn, wuq, kvn, wukv, tabs)


def _attn_a_kernel(qt_ref, k_ref, vt_ref, o_ref, a0_ref, a1_ref, b0_ref, b1_ref, *, tq, tk):
    s = k_ref.shape[1]
    nk, nq = s // tk, s // tq
    st_a, st_b = (a0_ref, a1_ref), (b0_ref, b1_ref)
    ones = jnp.ones((BF16_ROWS, tk), BF16)

    def scores(qb, c, st_ref):
        q0 = pl.multiple_of(qb * tq, tq)
        ks = pl.multiple_of(c * tk, tk)
        for j in range(2):
            st_ref[j] = _dot(k_ref[0, pl.ds(ks, tk), j * LANES:(j + 1) * LANES],
                             qt_ref[0, j * LANES:(j + 1) * LANES, pl.ds(q0, tq)])

    def accumulate(c, st_ref, carry):
        ks = pl.multiple_of(c * tk, tk)
        stats = []
        for j in range(2):
            m = carry[j][0]
            st = st_ref[j]
            m_new = jnp.maximum(m, jnp.max(st, axis=0, keepdims=True))
            stats.append((m_new, jnp.exp2(m - m_new), jnp.exp2(st - m_new).astype(BF16)))
        new = []
        for j in range(2):
            m_new, alpha, pt = stats[j]
            vt = jnp.concatenate([vt_ref[0, j * HEAD_DIM:(j + 1) * HEAD_DIM, pl.ds(ks, tk)], ones], axis=0)
            new.append((m_new, alpha * carry[j][1] + _dot(vt, pt)))
        return tuple(new)

    def half_step(qb, c, src, dst, carry, wrap=False):
        for u in range(2):
            if wrap:
                scores(jnp.minimum(qb + 1, nq - 1), u, dst[u])
            else:
                scores(qb, c + 2 + u, dst[u])
            carry = accumulate(c + u, src[u], carry)
        return carry

    def q_block(qb, _):
        def body(i, carry):
            carry = half_step(qb, 4 * i, st_a, st_b, carry)
            return half_step(qb, 4 * i + 2, st_b, st_a, carry)

        init = tuple((jnp.full((1, tq), NEG_INF, F32), jnp.zeros((HEAD_DIM + BF16_ROWS, tq), F32))
                     for _ in range(2))
        carry = lax.fori_loop(0, nk // 4 - 1, body, init)
        carry = half_step(qb, nk - 4, st_a, st_b, carry)
        res = half_step(qb, nk - 2, st_b, st_a, carry, wrap=True)
        q0 = pl.multiple_of(qb * tq, tq)
        for j in range(2):
            acc = res[j][1]
            o_ref[0, j * HEAD_DIM:(j + 1) * HEAD_DIM, pl.ds(q0, tq)] = (
                acc[:HEAD_DIM] / acc[HEAD_DIM:HEAD_DIM + 1]).astype(o_ref.dtype)
        return 0

    scores(0, 0, st_a[0])
    scores(0, 1, st_a[1])
    lax.fori_loop(0, nq, q_block, 0)


def _attn_a_call(qt, ka, vt, tq, tk):
    b, s, _ = ka.shape
    pairs = HEADS_A // 2
    assert s % (4 * tk) == 0
    return pl.pallas_call(
        functools.partial(_attn_a_kernel, tq=tq, tk=tk),
        grid=(b, pairs),
        in_specs=[
            pl.BlockSpec((1, 2 * LANES, s), lambda bi, p: (bi, p, 0)),
            pl.BlockSpec((1, s, 2 * LANES), lambda bi, p: (bi, 0, p)),
            pl.BlockSpec((1, LANES, s), lambda bi, p: (bi, p, 0)),
        ],
        out_specs=pl.BlockSpec((1, LANES, s), lambda bi, p: (bi, p, 0)),
        out_shape=jax.ShapeDtypeStruct((b, WIDTH_A, s), BF16),
        scratch_shapes=[pltpu.VMEM((2, tk, tq), F32) for _ in range(4)],
        compiler_params=pltpu.CompilerParams(dimension_semantics=("parallel", "parallel"),
                                             vmem_limit_bytes=VMEM_LIMIT),
    )(qt, ka, vt)


def _attn_b_kernel(q_ref, k_ref, v_ref, o_ref, lse_ref, band_ref, sa_ref, sb_ref, *, tq, half, n):
    total = q_ref.shape[0]
    kw = min(tq + 2 * half, n)
    first = _lane_is_first_head((tq, LANES))

    group = min(BLOCKS_PER_STEP, total // tq)
    diff = (lax.broadcasted_iota(jnp.int32, (tq, kw), 0) - lax.broadcasted_iota(jnp.int32, (tq, kw), 1))
    for var in range(band_ref.shape[0]):
        band_ref[var] = jnp.where(jnp.abs(diff + var * half) <= half, 0.0, NEG_INF)

    n_groups = total // (tq * group)

    def window(g, u):
        q0 = pl.multiple_of((g * group + u) * tq, tq)
        seg0 = (q0 // n) * n
        return q0, pl.multiple_of(jnp.clip(q0 - half, seg0, seg0 + n - kw), half)

    def issue(g, st_ref):
        g = jnp.minimum(g, n_groups - 1)
        for u in range(group):
            q0, ks = window(g, u)
            st_ref[u] = _dot_nt(_stack_heads(q_ref[pl.ds(q0, tq), :], first), k_ref[pl.ds(ks, kw), :])

    def consume(g, st_ref):
        probs = []
        for u in range(group):
            q0, ks = window(g, u)
            band = band_ref[(q0 - ks) // half]
            row = []
            for j in range(2):
                s = st_ref[u, j * tq:(j + 1) * tq, :] + band
                m = jnp.max(s, axis=-1, keepdims=True)
                p = jnp.exp2(s - m)
                den = jnp.sum(p, axis=-1, keepdims=True)
                row.append((p.astype(BF16), den, m + jnp.log2(den)))
            probs.append(row)
        for u, row in enumerate(probs):
            q0, ks = window(g, u)
            o2 = _dot(jnp.concatenate([row[0][0], row[1][0]], axis=0), v_ref[pl.ds(ks, kw), :])
            outs = [o2[j * tq:(j + 1) * tq] / row[j][1] for j in range(2)]
            lses = [jnp.broadcast_to(lse, (tq, LANES)) for (_, _, lse) in row]
            o_ref[pl.ds(q0, tq), :] = jnp.where(first, outs[0], outs[1]).astype(o_ref.dtype)
            lse_ref[pl.ds(q0, tq), :] = jnp.where(first, lses[0], lses[1])

    def body(it, carry):
        issue(2 * it + 1, sb_ref)
        consume(2 * it, sa_ref)
        issue(2 * it + 2, sa_ref)
        consume(2 * it + 1, sb_ref)
        return carry

    issue(0, sa_ref)
    lax.fori_loop(0, n_groups // 2, body, 0)


def _attn_b_call(q, k, v, half, tq):
    b, dil, n, w = q.shape
    pairs = w // LANES
    tq = min(tq, n)
    kw = min(tq + 2 * half, n)
    group = min(BLOCKS_PER_STEP, dil * n // tq)
    assert n % tq == 0 and (kw - tq) % half == 0 and (dil * n) % (2 * group * tq) == 0
    flat = lambda a: a.reshape(b, dil * n, w)
    spec = pl.BlockSpec((None, dil * n, LANES), lambda bi, p: (bi, 0, p))
    o, lse = pl.pallas_call(
        functools.partial(_attn_b_kernel, tq=tq, half=half, n=n),
        grid=(b, pairs),
        in_specs=[spec, spec, spec],
        out_specs=[spec, spec],
        out_shape=[jax.ShapeDtypeStruct((b, dil * n, w), BF16), jax.ShapeDtypeStruct((b, dil * n, w), F32)],
        scratch_shapes=[pltpu.VMEM(((kw - tq) // half + 1, tq, kw), F32),
                        pltpu.VMEM((group, 2 * tq, kw), F32), pltpu.VMEM((group, 2 * tq, kw), F32)],
        compiler_params=pltpu.CompilerParams(dimension_semantics=("parallel", "parallel"),
                                             vmem_limit_bytes=VMEM_LIMIT),
    )(flat(q), flat(k), flat(v))
    return o.reshape(q.shape), lse.reshape(q.shape)


def _na_bias_kernel(rpb_ref, o_ref):
    base = pl.program_id(0) * RPB_PER_HEAD
    shape = (GRID_W, LANES)
    lane = lax.broadcasted_iota(jnp.int32, shape, 1)
    p = lax.broadcasted_iota(jnp.int32, shape, 0)
    c = lane % GRID_W
    upper = lane >= GRID_W
    c_start = jnp.clip(p - NA_COLS // 2, 0, GRID_W - NA_COLS)
    col_ok = (c >= c_start) & (c < c_start + NA_COLS)
    dc = c - p + (NA_COLS - 1)
    n_dc = 2 * NA_COLS - 1
    for v in range(NA_ROWS):
        for m in range(NA_ROWS * GRID_W // LANES):
            a_lo = 2 * m - v + (NA_ROWS - 1)
            acc = jnp.full(shape, NEG_INF, F32)
            for b in range(n_dc):
                val = jnp.where(upper, rpb_ref[base + (a_lo + 1) * n_dc + b], rpb_ref[base + a_lo * n_dc + b])
                acc = jnp.where(dc == b, val, acc)
            o_ref[0, v, :, m * LANES:(m + 1) * LANES] = jnp.where(col_ok, acc, NEG_INF)


def _na_bias_call(rpb):
    nh = rpb.shape[0] * rpb.shape[1]
    return pl.pallas_call(
        _na_bias_kernel,
        grid=(nh,),
        in_specs=[pl.BlockSpec(memory_space=pltpu.SMEM)],
        out_specs=pl.BlockSpec((1, NA_ROWS, GRID_W, NA_ROWS * GRID_W), lambda g: (g, 0, 0, 0)),
        out_shape=jax.ShapeDtypeStruct((nh, NA_ROWS, GRID_W, NA_ROWS * GRID_W), F32),
    )(rpb.reshape(-1))


def _attn_c_kernel(q_ref, k_ref, v_ref, bias_ref, o_ref, sa_ref, sb_ref):
    rows = q_ref.shape[1] // GRID_W
    win = NA_ROWS * GRID_W
    first = _lane_is_first_head((GRID_W, LANES))
    group = sa_ref.shape[0]
    n_groups = rows // group

    def window(g, u):
        r = g * group + u
        r_start = jnp.clip(r - NA_ROWS // 2, 0, rows - NA_ROWS)
        return pl.multiple_of(r * GRID_W, GRID_W), pl.multiple_of(r_start * GRID_W, GRID_W), r - r_start

    def issue(g, st_ref):
        g = jnp.minimum(g, n_groups - 1)
        for u in range(group):
            q0, ks, _ = window(g, u)
            st_ref[u] = _dot_nt(_stack_heads(q_ref[0, pl.ds(q0, GRID_W), :], first), k_ref[0, pl.ds(ks, win), :])

    def consume(g, st_ref):
        probs = []
        for u in range(group):
            variant = window(g, u)[2]
            row = []
            for j in range(2):
                s = st_ref[u, j * GRID_W:(j + 1) * GRID_W, :] + bias_ref[j, variant]
                m = jnp.max(s, axis=-1, keepdims=True)
                p = jnp.exp(s - m)
                row.append((p.astype(BF16), jnp.sum(p, axis=-1, keepdims=True)))
            probs.append(row)
        for u, row in enumerate(probs):
            q0, ks, _ = window(g, u)
            o2 = _dot(jnp.concatenate([row[0][0], row[1][0]], axis=0), v_ref[0, pl.ds(ks, win), :])
            outs = [o2[j * GRID_W:(j + 1) * GRID_W] / row[j][1] for j in range(2)]
            o_ref[0, pl.ds(q0, GRID_W), :] = jnp.where(first, outs[0], outs[1]).astype(o_ref.dtype)

    def body(it, carry):
        issue(2 * it + 1, sb_ref)
        consume(2 * it, sa_ref)
        issue(2 * it + 2, sa_ref)
        consume(2 * it + 1, sb_ref)
        return carry

    issue(0, sa_ref)
    lax.fori_loop(0, n_groups // 2, body, 0)


def _attn_c_call(q, k, v, bias, layer):
    b, s, w = q.shape
    pairs = w // LANES
    spec = pl.BlockSpec((1, s, LANES), lambda bi, p: (bi, 0, p))
    return pl.pallas_call(
        _attn_c_kernel,
        grid=(b, pairs),
        in_specs=[spec, spec, spec,
                  pl.BlockSpec((2,) + bias.shape[1:], lambda bi, p: (layer * pairs + p, 0, 0, 0))],
        out_specs=spec,
        out_shape=jax.ShapeDtypeStruct((b, s, w), BF16),
        scratch_shapes=[pltpu.VMEM((BLOCKS_PER_STEP, 2 * GRID_W, NA_ROWS * GRID_W), F32) for _ in range(2)],
        compiler_params=pltpu.CompilerParams(dimension_semantics=("parallel", "parallel"),
                                             vmem_limit_bytes=VMEM_LIMIT),
    )(q, k, v, bias)


def _out_kernel(x_ref, oa_ref, ob1_ref, l1_ref, *rest):
    nd = len(_DILATIONS)
    dil_refs = rest[:2 * nd]
    oc_ref, ga_ref, gb_ref, gc_ref, w_ref, o_ref, stage_ref = rest[2 * nd:]
    tm = x_ref.shape[0]
    nblk = WIDTH_B // LANES
    oa = jnp.concatenate([oa_ref[0, blk * LANES:(blk + 1) * LANES, :].astype(F32).T
                          for blk in range(WIDTH_A // LANES)], axis=-1)
    na = _rms(oa, ga_ref[...]).astype(BF16)
    outs, lses = [ob1_ref[...].astype(F32)], [l1_ref[...]]
    for di, dil in enumerate(_DILATIONS):
        rows = tm // dil
        for which, acc in ((0, outs), (1, lses)):
            src = dil_refs[2 * di + which]
            slot = (2 * di + which) * nblk
            for r in range(dil):
                for blk in range(nblk):
                    stage_ref[slot + blk, pl.ds(r, rows, stride=dil), :] = (
                        src[0, r, :, blk * LANES:(blk + 1) * LANES].astype(F32))
            acc.append(jnp.concatenate([stage_ref[slot + blk] for blk in range(nblk)], axis=-1))
    lmax = functools.reduce(jnp.maximum, lses)
    es = [jnp.exp2(l - lmax) for l in lses]
    ob = sum(e * o for e, o in zip(es, outs)) / sum(es)
    nb = _rms(ob, gb_ref[...]).astype(BF16)
    nc = _rms(oc_ref[...].astype(F32), gc_ref[...]).astype(BF16)
    acc = _dot(na, w_ref[0:WIDTH_A, :])
    acc += _dot(nb, w_ref[WIDTH_A:WIDTH_A + WIDTH_B, :])
    acc += _dot(nc, w_ref[WIDTH_A + WIDTH_B:, :])
    o_ref[...] = x_ref[...] + acc


def _out_call(x2, oa, ob1, l1, dilated, oc, ga, gb, gc, w, seq, tm, layer):
    t, d = x2.shape
    nseq = seq // tm
    row = lambda i: (i, 0)
    const = lambda i: (0, 0)
    rs = lambda width: pl.BlockSpec((tm, width), row)
    cs = lambda width: pl.BlockSpec((1, width), const)
    dil_specs, dil_args = [], []
    for dil, pair in zip(_DILATIONS, dilated):
        for a in pair:
            dil_specs.append(pl.BlockSpec((1, dil, tm // dil, WIDTH_B), lambda i: (i // nseq, 0, i % nseq, 0)))
            dil_args.append(a)
    return pl.pallas_call(
        _out_kernel,
        grid=(t // tm,),
        in_specs=[rs(d), pl.BlockSpec((1, WIDTH_A, tm), lambda i: (i // nseq, 0, i % nseq)),
                  rs(WIDTH_B), rs(WIDTH_B), *dil_specs,
                  rs(WIDTH_C), cs(WIDTH_A), cs(WIDTH_B), cs(WIDTH_C), _layer_spec(w, layer)],
        out_specs=rs(d),
        out_shape=jax.ShapeDtypeStruct((t, d), F32),
        scratch_shapes=[pltpu.VMEM((2 * len(_DILATIONS) * WIDTH_B // LANES, tm, LANES), F32)],
        compiler_params=pltpu.CompilerParams(dimension_semantics=("parallel",),
                                             vmem_limit_bytes=VMEM_LIMIT),
    )(x2, oa, ob1, l1, *dil_args, oc, ga, gb, gc, w)


def _mlp_kernel(x_ref, g_ref, w1_ref, w2_ref, gf_ref, o_ref, *, final_norm, tf):
    tm = x_ref.shape[0]
    hm = tm // MLP_SUBTILES
    rows = [slice(sub * hm, (sub + 1) * hm) for sub in range(MLP_SUBTILES)]
    hs = [_rms(x_ref[rs, :], g_ref[...]).astype(BF16) for rs in rows]
    accs = [None] * MLP_SUBTILES
    for f in range(w1_ref.shape[1] // tf):
        cols = slice(f * tf, (f + 1) * tf)
        us = [jnp.maximum(_dot(h, w1_ref[:, cols]), 0.0) for h in hs]
        for sub, u in enumerate(us):
            y = _dot((u * u).astype(BF16), w2_ref[cols, :])
            accs[sub] = y if accs[sub] is None else accs[sub] + y
    for rs, acc in zip(rows, accs):
        y = x_ref[rs, :] + acc
        if final_norm:
            y = _rms(y, gf_ref[...])
        o_ref[rs, :] = y


def _mlp_call(x2, g, w1, w2, gf, final_norm, tm, tf, layer):
    t, d = x2.shape
    return pl.pallas_call(
        functools.partial(_mlp_kernel, final_norm=final_norm, tf=tf),
        grid=(t // tm,),
        in_specs=[
            pl.BlockSpec((tm, d), lambda i: (i, 0)),
            pl.BlockSpec((1, d), lambda i: (0, 0)),
            _layer_spec(w1, layer, pipeline_mode=pl.Buffered(1)),
            _layer_spec(w2, layer, pipeline_mode=pl.Buffered(1)),
            pl.BlockSpec((1, d), lambda i: (0, 0)),
        ],
        out_specs=pl.BlockSpec((tm, d), lambda i: (i, 0)),
        out_shape=jax.ShapeDtypeStruct((t, d), F32),
        compiler_params=pltpu.CompilerParams(dimension_semantics=("parallel",),
                                             vmem_limit_bytes=VMEM_LIMIT),
    )(x2, g, w1, w2, gf)


def _rotate_half_cols(w, half):
    return jnp.concatenate([-w[..., half:], w[..., :half]], axis=-1)


def _rope_tables(seq):
    pos = jnp.arange(seq, dtype=F32)

    def cos_sin(half):
        inv_freq = ROPE_THETA ** (-jnp.arange(half, dtype=F32) / half)
        ang = pos[:, None] * inv_freq[None, :]
        return jnp.cos(ang), jnp.sin(ang)

    ca, sa = cos_sin(QK_ROPE // 2)
    ca2, sa2 = jnp.concatenate([ca, ca], -1), jnp.concatenate([sa, sa], -1)
    ones = jnp.ones((seq, QK_NOPE), F32)
    zeros = jnp.zeros((seq, QK_NOPE), F32)
    tail = jnp.zeros((seq, LANES - QK_NOPE - QK_ROPE), F32)
    scale_a = (QK_NOPE + QK_ROPE) ** -0.5 * LOG2_E
    cosq = jnp.concatenate([ones, ca2, tail], -1) * scale_a
    sinq = jnp.concatenate([zeros, sa2, tail], -1) * scale_a
    cosk = jnp.concatenate([zeros, ca2, tail], -1)
    sink = jnp.concatenate([zeros, sa2, tail], -1)
    cb, sb = cos_sin(HEAD_DIM // 2)
    cosb = jnp.concatenate([cb, cb, cb, cb], -1)
    sinb = jnp.concatenate([-sb, sb, -sb, sb], -1)
    return jnp.concatenate([cosq, sinq, cosk, sink, cosb, sinb], -1)


def _layer_weights(w_in, w_uq, w_ukv):
    d = w_in.shape[0]
    scale = HEAD_DIM ** -0.5
    c_b = Q_LORA + KV_LORA + QK_ROPE
    c_c = c_b + 3 * WIDTH_B
    w_kpe = w_in[:, Q_LORA + KV_LORA:c_b]

    def place(w):
        return jnp.concatenate([jnp.zeros((d, QK_NOPE), F32), w, jnp.zeros((d, LANES - QK_NOPE - QK_ROPE), F32)], -1)

    wbig = jnp.concatenate([
        w_in[:, :Q_LORA + KV_LORA],
        place(w_kpe), place(_rotate_half_cols(w_kpe, QK_ROPE // 2)),
        w_in[:, c_b:c_b + WIDTH_B] * scale, w_in[:, c_b + WIDTH_B:c_c],
        w_in[:, c_c:c_c + WIDTH_C] * scale, w_in[:, c_c + WIDTH_C:],
    ], -1).astype(BF16)

    uq = w_uq.reshape(Q_LORA, HEADS_A, QK_NOPE + QK_ROPE)
    pad = jnp.zeros((Q_LORA, HEADS_A, LANES - QK_NOPE - QK_ROPE), F32)
    uq_pad = jnp.concatenate([uq, pad], -1)
    uq_rot = jnp.concatenate([jnp.zeros((Q_LORA, HEADS_A, QK_NOPE), F32),
                              _rotate_half_cols(uq[..., QK_NOPE:], QK_ROPE // 2), pad], -1)
    wuq = jnp.concatenate([uq_pad.reshape(Q_LORA, QK_A_PAD), uq_rot.reshape(Q_LORA, QK_A_PAD)], -1).astype(BF16)

    ukv = w_ukv.reshape(KV_LORA, HEADS_A, QK_NOPE + HEAD_DIM)
    uk_pad = jnp.concatenate([ukv[..., :QK_NOPE], jnp.zeros((KV_LORA, HEADS_A, LANES - QK_NOPE), F32)], -1)
    wukv = jnp.concatenate([uk_pad.reshape(KV_LORA, QK_A_PAD), ukv[..., QK_NOPE:].reshape(KV_LORA, WIDTH_A)],
                           -1).astype(BF16)
    return wbig, wuq, wukv


def kernel(x, g_mix, w_in, q_norm, w_uq, kv_norm, w_ukv, rpb, out_norm_a, out_norm_b, out_norm_c, w_out, g_mlp,
           w_mlp_in, w_mlp_out, g_final):
    b, s, d = x.shape
    depth = w_in.shape[0]
    t = b * s
    assert s % (2 * BLOCKS_PER_STEP * GRID_W) == 0 and s // GRID_W >= NA_ROWS
    tm_proj = min(512, s)
    tm_out = min(1024, s)
    tm_mlp = min(1024, t)
    tq_a = min(256, s)
    tk_a = min(512, s)

    tabs = _rope_tables(s)
    na_bias = _na_bias_call(rpb)
    row = lambda a: a.reshape(1, -1)

    wbig, wuq, wukv = jax.vmap(_layer_weights)(w_in, w_uq, w_ukv)
    w_out_b, w1_b, w2_b = w_out.astype(BF16), w_mlp_in.astype(BF16), w_mlp_out.astype(BF16)

    x2 = x.reshape(t, d)
    for l in range(depth):
        qa, ka, va, qb, kb, vb, qc, kc, vc, *dil_qkv = _proj_call(
            x2, row(g_mix[l]), wbig, row(q_norm[l]), wuq, row(kv_norm[l]), wukv, tabs, s, tm_proj, l)
        seq3 = lambda a: a.reshape(b, s, a.shape[-1])
        oa = _attn_a_call(qa, seq3(ka), va, tq_a, tk_a)
        ob1, l1, dilated = None, None, []
        for window, dil in DILATED_PAIRS:
            half = window // (2 * dil)
            if dil == 1:
                unit = lambda a: a.reshape(b, 1, s, WIDTH_B)
                o_i, lse_i = _attn_b_call(unit(qb), unit(kb), unit(vb), half, 128)
                ob1, l1 = o_i.reshape(t, WIDTH_B), lse_i.reshape(t, WIDTH_B)
            else:
                di = 3 * _DILATIONS.index(dil)
                dilated.append(_attn_b_call(*dil_qkv[di:di + 3], half, 128))
        oc = _attn_c_call(seq3(qc), seq3(kc), seq3(vc), na_bias, l).reshape(t, WIDTH_C)
        x2 = _out_call(x2, oa, ob1, l1, dilated, oc, row(out_norm_a[l]), row(out_norm_b[l]), row(out_norm_c[l]),
                       w_out_b, s, tm_out, l)
        x2 = _mlp_call(x2, row(g_mlp[l]), w1_b, w2_b, row(g_final), l == depth - 1, tm_mlp,
                       min(1024, w_mlp_in.shape[2]), l)
    return x2.reshape(b, s, d)
```

```python
import functools

import jax
import jax.numpy as jnp
from jax import lax
from jax.experimental import pallas as pl
from jax.experimental.pallas import tpu as pltpu

HEAD_DIM = 64
LANES = 128
BF16_ROWS = 16
HEADS_A = 6
HEADS_B = 6
HEADS_C = 4
Q_LORA = 256
KV_LORA = 128
QK_NOPE = 64
QK_ROPE = 32
DILATED_PAIRS = ((128, 1), (512, 4), (2048, 16))
_DILATIONS = tuple(dil for _, dil in DILATED_PAIRS if dil > 1)
GRID_W = 64
NA_ROWS = 8
NA_COLS = 16
ROPE_THETA = 10000.0
NORM_EPS = 1e-6
NEG_INF = -1e30
LOG2_E = 1.4426950408889634
PROJ_SUBTILES = 2
MLP_SUBTILES = 2
BLOCKS_PER_STEP = 4

WIDTH_A = HEADS_A * HEAD_DIM
WIDTH_B = HEADS_B * HEAD_DIM
WIDTH_C = HEADS_C * HEAD_DIM
QK_A_PAD = HEADS_A * LANES
RPB_PER_HEAD = (2 * NA_ROWS - 1) * (2 * NA_COLS - 1)

VMEM_LIMIT = 56 * 1024 * 1024

TILE_ROWS_PROJ = 512
TILE_ROWS_OUT = 1024
TILE_ROWS_MLP = 1024
TILE_FF_MLP = 1024
TILE_Q_DENSE = 256
TILE_K_DENSE = 512
TILE_Q_BANDED = 128

BF16 = jnp.bfloat16
F32 = jnp.float32


def _rms(x, g):
    return x * lax.rsqrt(jnp.mean(x * x, axis=-1, keepdims=True) + NORM_EPS) * g


def _dot(a, b):
    return jnp.dot(a, b, preferred_element_type=F32)


def _dot_nt(a, b):
    return lax.dot_general(a, b, (((1,), (1,)), ((), ())), preferred_element_type=F32)


def _lane_is_first_head(shape):
    return lax.broadcasted_iota(jnp.int32, shape, len(shape) - 1) < HEAD_DIM


def _stack_heads(q, first):
    return jnp.concatenate([_keep_head(q, first, 0), _keep_head(q, first, 1)], axis=0)


def _keep_head(q, first, j):
    zero = jnp.zeros_like(q)
    return jnp.where(first, q, zero) if j == 0 else jnp.where(first, zero, q)


_C_CQ = 0
_C_CKV = _C_CQ + Q_LORA
_C_KPE = _C_CKV + KV_LORA
_C_KPR = _C_KPE + LANES
_C_QB = _C_KPR + LANES
_C_KB = _C_QB + WIDTH_B
_C_VB = _C_KB + WIDTH_B
_C_QC = _C_VB + WIDTH_B
_C_KC = _C_QC + WIDTH_C
_C_VC = _C_KC + WIDTH_C
_C_END = _C_VC + WIDTH_C

_T_COSQ, _T_SINQ, _T_COSK, _T_SINK, _T_COSB, _T_SINB = range(6)


def _proj_kernel(x_ref, g_ref, wbig_ref, qn_ref, wuq_ref, kvn_ref, wukv_ref, tab_ref,
                 qa_ref, ka_ref, va_ref, qb_ref, kb_ref, vb_ref, qc_ref, kc_ref, vc_ref, *rest):
    dil_refs, stage_ref = rest[:-1], rest[-1]
    tm = x_ref.shape[0]
    hm = tm // PROJ_SUBTILES
    nblk = WIDTH_B // LANES
    first_half = (lax.broadcasted_iota(jnp.int32, (hm, LANES), 1) % HEAD_DIM) < HEAD_DIM // 2

    for sub in range(PROJ_SUBTILES):
        rs = slice(sub * hm, (sub + 1) * hm)

        def tab(i):
            return tab_ref[rs, i * LANES:(i + 1) * LANES]

        h = _rms(x_ref[rs, :], g_ref[...]).astype(BF16)
        proj = _dot(h, wbig_ref[...])

        cqn = _rms(proj[:, _C_CQ:_C_CKV], qn_ref[...]).astype(BF16)
        qa2 = _dot(cqn, wuq_ref[...])
        ckvn = _rms(proj[:, _C_CKV:_C_KPE], kvn_ref[...]).astype(BF16)
        kv2 = _dot(ckvn, wukv_ref[...])
        kpe = proj[:, _C_KPE:_C_KPR] * tab(_T_COSK) + proj[:, _C_KPR:_C_QB] * tab(_T_SINK)
        cosq, sinq = tab(_T_COSQ), tab(_T_SINQ)
        for hd in range(HEADS_A):
            sl = slice(hd * LANES, (hd + 1) * LANES)
            rot = slice(QK_A_PAD + hd * LANES, QK_A_PAD + (hd + 1) * LANES)
            qa_ref[0, sl, rs] = (qa2[:, sl] * cosq + qa2[:, rot] * sinq).T.astype(BF16)
            ka_ref[rs, sl] = (kv2[:, sl] + kpe).astype(BF16)
        for blk in range(WIDTH_A // LANES):
            sl = slice(blk * LANES, (blk + 1) * LANES)
            va_ref[0, sl, rs] = kv2[:, QK_A_PAD + blk * LANES:QK_A_PAD + (blk + 1) * LANES].T.astype(BF16)

        cosb, sinb = tab(_T_COSB), tab(_T_SINB)
        for ti, (src, dst) in enumerate(((_C_QB, qb_ref), (_C_KB, kb_ref), (_C_VB, vb_ref))):
            for blk in range(nblk):
                xb = proj[:, src + blk * LANES:src + (blk + 1) * LANES]
                if dst is not vb_ref:
                    swapped = jnp.where(first_half, pltpu.roll(xb, LANES - HEAD_DIM // 2, 1),
                                        pltpu.roll(xb, HEAD_DIM // 2, 1))
                    xb = xb * cosb + swapped * sinb
                    if dst is qb_ref:
                        xb = xb * LOG2_E
                dst[rs, blk * LANES:(blk + 1) * LANES] = xb.astype(BF16)
                stage_ref[ti * nblk + blk, rs, :] = xb
        for di, dil in enumerate(_DILATIONS):
            rows = hm // dil
            for ti in range(3):
                dref = dil_refs[di * 3 + ti]
                for r in range(dil):
                    for blk in range(nblk):
                        dref[0, r, sub * rows:(sub + 1) * rows, blk * LANES:(blk + 1) * LANES] = (
                            stage_ref[ti * nblk + blk, pl.ds(sub * hm + r, rows, stride=dil), :].astype(BF16))

        qc_ref[rs, :] = proj[:, _C_QC:_C_KC].astype(BF16)
        kc_ref[rs, :] = proj[:, _C_KC:_C_VC].astype(BF16)
        vc_ref[rs, :] = proj[:, _C_VC:_C_END].astype(BF16)


def _layer_spec(w, layer, **kwargs):
    return pl.BlockSpec((None,) + w.shape[1:], lambda i: (layer,) + (0,) * (w.ndim - 1), **kwargs)


def _proj_call(x2, g, wbig, qn, wuq, kvn, wukv, tabs, seq, tm, layer):
    t, d = x2.shape
    nseq = seq // tm
    row = lambda i: (i, 0)
    const = lambda i: (0, 0)
    widths = (QK_A_PAD, QK_A_PAD, WIDTH_A, WIDTH_B, WIDTH_B, WIDTH_B, WIDTH_C, WIDTH_C, WIDTH_C)
    out_specs = [pl.BlockSpec((tm, w), row) for w in widths]
    out_shape = [jax.ShapeDtypeStruct((t, w), BF16) for w in widths]
    for idx in (0, 2):
        out_specs[idx] = pl.BlockSpec((1, widths[idx], tm), lambda i: (i // nseq, 0, i % nseq))
        out_shape[idx] = jax.ShapeDtypeStruct((t // seq, widths[idx], seq), BF16)
    for dil in _DILATIONS:
        assert tm % (BF16_ROWS * dil * PROJ_SUBTILES) == 0
        for _ in range(3):
            out_specs.append(pl.BlockSpec((1, dil, tm // dil, WIDTH_B), lambda i: (i // nseq, 0, i % nseq, 0)))
            out_shape.append(jax.ShapeDtypeStruct((t // seq, dil, seq // dil, WIDTH_B), BF16))
    return pl.pallas_call(
        _proj_kernel,
        grid=(t // tm,),
        in_specs=[
            pl.BlockSpec((tm, d), row),
            pl.BlockSpec((1, d), const),
            _layer_spec(wbig, layer),
            pl.BlockSpec((1, Q_LORA), const),
            _layer_spec(wuq, layer),
            pl.BlockSpec((1, KV_LORA), const),
            _layer_spec(wukv, layer),
            pl.BlockSpec((tm, tabs.shape[1]), lambda i: (i % nseq, 0)),
        ],
        out_specs=out_specs,
        out_shape=out_shape,
        scratch_shapes=[pltpu.VMEM((3 * WIDTH_B // LANES, tm, LANES), F32)],
        compiler_params=pltpu.CompilerParams(dimension_semantics=("parallel",),
                                             vmem_limit_bytes=VMEM_LIMIT),
    )(x2, g, wbig, qn, wuq, kvn, wukv, tabs)


def _attn_a_kernel(qt_ref, k_ref, vt_ref, o_ref, a0_ref, a1_ref, b0_ref, b1_ref, *, tq, tk):
    s = k_ref.shape[1]
    nk, nq = s // tk, s // tq
    st_a, st_b = (a0_ref, a1_ref), (b0_ref, b1_ref)
    ones = jnp.ones((BF16_ROWS, tk), BF16)

    def scores(qb, c, st_ref):
        q0 = pl.multiple_of(qb * tq, tq)
        ks = pl.multiple_of(c * tk, tk)
        for j in range(2):
            st_ref[j] = _dot(k_ref[0, pl.ds(ks, tk), j * LANES:(j + 1) * LANES],
                             qt_ref[0, j * LANES:(j + 1) * LANES, pl.ds(q0, tq)])

    def accumulate(c, st_ref, carry):
        ks = pl.multiple_of(c * tk, tk)
        stats = []
        for j in range(2):
            m = carry[j][0]
            st = st_ref[j]
            m_new = jnp.maximum(m, jnp.max(st, axis=0, keepdims=True))
            stats.append((m_new, jnp.exp2(m - m_new), jnp.exp2(st - m_new).astype(BF16)))
        new = []
        for j in range(2):
            m_new, alpha, pt = stats[j]
            vt = jnp.concatenate([vt_ref[0, j * HEAD_DIM:(j + 1) * HEAD_DIM, pl.ds(ks, tk)], ones], axis=0)
            new.append((m_new, alpha * carry[j][1] + _dot(vt, pt)))
        return tuple(new)

    def half_step(qb, c, src, dst, carry, wrap=False):
        for u in range(2):
            if wrap:
                scores(jnp.minimum(qb + 1, nq - 1), u, dst[u])
            else:
                scores(qb, c + 2 + u, dst[u])
            carry = accumulate(c + u, src[u], carry)
        return carry

    def q_block(qb, _):
        def body(i, carry):
            carry = half_step(qb, 4 * i, st_a, st_b, carry)
            return half_step(qb, 4 * i + 2, st_b, st_a, carry)

        init = tuple((jnp.full((1, tq), NEG_INF, F32), jnp.zeros((HEAD_DIM + BF16_ROWS, tq), F32))
                     for _ in range(2))
        carry = lax.fori_loop(0, nk // 4 - 1, body, init)
        carry = half_step(qb, nk - 4, st_a, st_b, carry)
        res = half_step(qb, nk - 2, st_b, st_a, carry, wrap=True)
        q0 = pl.multiple_of(qb * tq, tq)
        for j in range(2):
            acc = res[j][1]
            o_ref[0, j * HEAD_DIM:(j + 1) * HEAD_DIM, pl.ds(q0, tq)] = (
                acc[:HEAD_DIM] / acc[HEAD_DIM:HEAD_DIM + 1]).astype(o_ref.dtype)
        return 0

    scores(0, 0, st_a[0])
    scores(0, 1, st_a[1])
    lax.fori_loop(0, nq, q_block, 0)


def _attn_a_call(qt, ka, vt, tq, tk):
    b, s, _ = ka.shape
    pairs = HEADS_A // 2
    assert s % (4 * tk) == 0
    return pl.pallas_call(
        functools.partial(_attn_a_kernel, tq=tq, tk=tk),
        grid=(b, pairs),
        in_specs=[
            pl.BlockSpec((1, 2 * LANES, s), lambda bi, p: (bi, p, 0)),
            pl.BlockSpec((1, s, 2 * LANES), lambda bi, p: (bi, 0, p)),
            pl.BlockSpec((1, LANES, s), lambda bi, p: (bi, p, 0)),
        ],
        out_specs=pl.BlockSpec((1, LANES, s), lambda bi, p: (bi, p, 0)),
        out_shape=jax.ShapeDtypeStruct((b, WIDTH_A, s), BF16),
        scratch_shapes=[pltpu.VMEM((2, tk, tq), F32) for _ in range(4)],
        compiler_params=pltpu.CompilerParams(dimension_semantics=("parallel", "parallel"),
                                             vmem_limit_bytes=VMEM_LIMIT),
    )(qt, ka, vt)


def _attn_b_kernel(q_ref, k_ref, v_ref, o_ref, lse_ref, band_ref, sa_ref, sb_ref, *, tq, half, n):
    total = q_ref.shape[0]
    kw = min(tq + 2 * half, n)
    first = _lane_is_first_head((tq, LANES))

    group = min(BLOCKS_PER_STEP, total // tq)
    diff = (lax.broadcasted_iota(jnp.int32, (tq, kw), 0) - lax.broadcasted_iota(jnp.int32, (tq, kw), 1))
    for var in range(band_ref.shape[0]):
        band_ref[var] = jnp.where(jnp.abs(diff + var * half) <= half, 0.0, NEG_INF)

    n_groups = total // (tq * group)

    def window(g, u):
        q0 = pl.multiple_of((g * group + u) * tq, tq)
        seg0 = (q0 // n) * n
        return q0, pl.multiple_of(jnp.clip(q0 - half, seg0, seg0 + n - kw), half)

    def issue(g, st_ref):
        g = jnp.minimum(g, n_groups - 1)
        for u in range(group):
            q0, ks = window(g, u)
            st_ref[u] = _dot_nt(_stack_heads(q_ref[pl.ds(q0, tq), :], first), k_ref[pl.ds(ks, kw), :])

    def consume(g, st_ref):
        probs = []
        for u in range(group):
            q0, ks = window(g, u)
            band = band_ref[(q0 - ks) // half]
            row = []
            for j in range(2):
                s = st_ref[u, j * tq:(j + 1) * tq, :] + band
                m = jnp.max(s, axis=-1, keepdims=True)
                p = jnp.exp2(s - m)
                den = jnp.sum(p, axis=-1, keepdims=True)
                row.append((p.astype(BF16), den, m + jnp.log2(den)))
            probs.append(row)
        for u, row in enumerate(probs):
            q0, ks = window(g, u)
            o2 = _dot(jnp.concatenate([row[0][0], row[1][0]], axis=0), v_ref[pl.ds(ks, kw), :])
            outs = [o2[j * tq:(j + 1) * tq] / row[j][1] for j in range(2)]
            lses = [jnp.broadcast_to(lse, (tq, LANES)) for (_, _, lse) in row]
            o_ref[pl.ds(q0, tq), :] = jnp.where(first, outs[0], outs[1]).astype(o_ref.dtype)
            lse_ref[pl.ds(q0, tq), :] = jnp.where(first, lses[0], lses[1])

    def body(it, carry):
        issue(2 * it + 1, sb_ref)
        consume(2 * it, sa_ref)
        issue(2 * it + 2, sa_ref)
        consume(2 * it + 1, sb_ref)
        return carry

    issue(0, sa_ref)
    lax.fori_loop(0, n_groups // 2, body, 0)


def _attn_b_call(q, k, v, half, tq):
    b, dil, n, w = q.shape
    pairs = w // LANES
    tq = min(tq, n)
    kw = min(tq + 2 * half, n)
    group = min(BLOCKS_PER_STEP, dil * n // tq)
    assert n % tq == 0 and (kw - tq) % half == 0 and (dil * n) % (2 * group * tq) == 0
    flat = lambda a: a.reshape(b, dil * n, w)
    spec = pl.BlockSpec((None, dil * n, LANES), lambda bi, p: (bi, 0, p))
    o, lse = pl.pallas_call(
        functools.partial(_attn_b_kernel, tq=tq, half=half, n=n),
        grid=(b, pairs),
        in_specs=[spec, spec, spec],
        out_specs=[spec, spec],
        out_shape=[jax.ShapeDtypeStruct((b, dil * n, w), BF16), jax.ShapeDtypeStruct((b, dil * n, w), F32)],
        scratch_shapes=[pltpu.VMEM(((kw - tq) // half + 1, tq, kw), F32),
                        pltpu.VMEM((group, 2 * tq, kw), F32), pltpu.VMEM((group, 2 * tq, kw), F32)],
        compiler_params=pltpu.CompilerParams(dimension_semantics=("parallel", "parallel"),
                                             vmem_limit_bytes=VMEM_LIMIT),
    )(flat(q), flat(k), flat(v))
    return o.reshape(q.shape), lse.reshape(q.shape)


def _na_bias_kernel(rpb_ref, o_ref):
    base = pl.program_id(0) * RPB_PER_HEAD
    shape = (GRID_W, LANES)
    lane = lax.broadcasted_iota(jnp.int32, shape, 1)
    p = lax.broadcasted_iota(jnp.int32, shape, 0)
    c = lane % GRID_W
    upper = lane >= GRID_W
    c_start = jnp.clip(p - NA_COLS // 2, 0, GRID_W - NA_COLS)
    col_ok = (c >= c_start) & (c < c_start + NA_COLS)
    dc = c - p + (NA_COLS - 1)
    n_dc = 2 * NA_COLS - 1
    for v in range(NA_ROWS):
        for m in range(NA_ROWS * GRID_W // LANES):
            a_lo = 2 * m - v + (NA_ROWS - 1)
            acc = jnp.full(shape, NEG_INF, F32)
            for b in range(n_dc):
                val = jnp.where(upper, rpb_ref[base + (a_lo + 1) * n_dc + b], rpb_ref[base + a_lo * n_dc + b])
                acc = jnp.where(dc == b, val, acc)
            o_ref[0, v, :, m * LANES:(m + 1) * LANES] = jnp.where(col_ok, acc, NEG_INF)


def _na_bias_call(rpb):
    nh = rpb.shape[0] * rpb.shape[1]
    return pl.pallas_call(
        _na_bias_kernel,
        grid=(nh,),
        in_specs=[pl.BlockSpec(memory_space=pltpu.SMEM)],
        out_specs=pl.BlockSpec((1, NA_ROWS, GRID_W, NA_ROWS * GRID_W), lambda g: (g, 0, 0, 0)),
        out_shape=jax.ShapeDtypeStruct((nh, NA_ROWS, GRID_W, NA_ROWS * GRID_W), F32),
    )(rpb.reshape(-1))


def _attn_c_kernel(q_ref, k_ref, v_ref, bias_ref, o_ref, sa_ref, sb_ref):
    rows = q_ref.shape[1] // GRID_W
    win = NA_ROWS * GRID_W
    first = _lane_is_first_head((GRID_W, LANES))
    group = sa_ref.shape[0]
    n_groups = rows // group

    def window(g, u):
        r = g * group + u
        r_start = jnp.clip(r - NA_ROWS // 2, 0, rows - NA_ROWS)
        return pl.multiple_of(r * GRID_W, GRID_W), pl.multiple_of(r_start * GRID_W, GRID_W), r - r_start

    def issue(g, st_ref):
        g = jnp.minimum(g, n_groups - 1)
        for u in range(group):
            q0, ks, _ = window(g, u)
            st_ref[u] = _dot_nt(_stack_heads(q_ref[0, pl.ds(q0, GRID_W), :], first), k_ref[0, pl.ds(ks, win), :])

    def consume(g, st_ref):
        probs = []
        for u in range(group):
            variant = window(g, u)[2]
            row = []
            for j in range(2):
                s = st_ref[u, j * GRID_W:(j + 1) * GRID_W, :] + bias_ref[j, variant]
                m = jnp.max(s, axis=-1, keepdims=True)
                p = jnp.exp(s - m)
                row.append((p.astype(BF16), jnp.sum(p, axis=-1, keepdims=True)))
            probs.append(row)
        for u, row in enumerate(probs):
            q0, ks, _ = window(g, u)
            o2 = _dot(jnp.concatenate([row[0][0], row[1][0]], axis=0), v_ref[0, pl.ds(ks, win), :])
            outs = [o2[j * GRID_W:(j + 1) * GRID_W] / row[j][1] for j in range(2)]
            o_ref[0, pl.ds(q0, GRID_W), :] = jnp.where(first, outs[0], outs[1]).astype(o_ref.dtype)

    def body(it, carry):
        issue(2 * it + 1, sb_ref)
        consume(2 * it, sa_ref)
        issue(2 * it + 2, sa_ref)
        consume(2 * it + 1, sb_ref)
        return carry

    issue(0, sa_ref)
    lax.fori_loop(0, n_groups // 2, body, 0)


def _attn_c_call(q, k, v, bias, layer):
    b, s, w = q.shape
    pairs = w // LANES
    spec = pl.BlockSpec((1, s, LANES), lambda bi, p: (bi, 0, p))
    return pl.pallas_call(
        _attn_c_kernel,
        grid=(b, pairs),
        in_specs=[spec, spec, spec,
                  pl.BlockSpec((2,) + bias.shape[1:], lambda bi, p: (layer * pairs + p, 0, 0, 0))],
        out_specs=spec,
        out_shape=jax.ShapeDtypeStruct((b, s, w), BF16),
        scratch_shapes=[pltpu.VMEM((BLOCKS_PER_STEP, 2 * GRID_W, NA_ROWS * GRID_W), F32) for _ in range(2)],
        compiler_params=pltpu.CompilerParams(dimension_semantics=("parallel", "parallel"),
                                             vmem_limit_bytes=VMEM_LIMIT),
    )(q, k, v, bias)


def _out_kernel(x_ref, oa_ref, ob1_ref, l1_ref, *rest):
    nd = len(_DILATIONS)
    dil_refs = rest[:2 * nd]
    oc_ref, ga_ref, gb_ref, gc_ref, w_ref, o_ref, stage_ref = rest[2 * nd:]
    tm = x_ref.shape[0]
    nblk = WIDTH_B // LANES
    oa = jnp.concatenate([oa_ref[0, blk * LANES:(blk + 1) * LANES, :].astype(F32).T
                          for blk in range(WIDTH_A // LANES)], axis=-1)
    na = _rms(oa, ga_ref[...]).astype(BF16)
    outs, lses = [ob1_ref[...].astype(F32)], [l1_ref[...]]
    for di, dil in enumerate(_DILATIONS):
        rows = tm // dil
        for which, acc in ((0, outs), (1, lses)):
            src = dil_refs[2 * di + which]
            slot = (2 * di + which) * nblk
            for r in range(dil):
                for blk in range(nblk):
                    stage_ref[slot + blk, pl.ds(r, rows, stride=dil), :] = (
                        src[0, r, :, blk * LANES:(blk + 1) * LANES].astype(F32))
            acc.append(jnp.concatenate([stage_ref[slot + blk] for blk in range(nblk)], axis=-1))
    lmax = functools.reduce(jnp.maximum, lses)
    es = [jnp.exp2(l - lmax) for l in lses]
    ob = sum(e * o for e, o in zip(es, outs)) / sum(es)
    nb = _rms(ob, gb_ref[...]).astype(BF16)
    nc = _rms(oc_ref[...].astype(F32), gc_ref[...]).astype(BF16)
    acc = _dot(na, w_ref[0:WIDTH_A, :])
    acc += _dot(nb, w_ref[WIDTH_A:WIDTH_A + WIDTH_B, :])
    acc += _dot(nc, w_ref[WIDTH_A + WIDTH_B:, :])
    o_ref[...] = x_ref[...] + acc


def _out_call(x2, oa, ob1, l1, dilated, oc, ga, gb, gc, w, seq, tm, layer):
    t, d = x2.shape
    nseq = seq // tm
    row = lambda i: (i, 0)
    const = lambda i: (0, 0)
    rs = lambda width: pl.BlockSpec((tm, width), row)
    cs = lambda width: pl.BlockSpec((1, width), const)
    dil_specs, dil_args = [], []
    for dil, pair in zip(_DILATIONS, dilated):
        for a in pair:
            dil_specs.append(pl.BlockSpec((1, dil, tm // dil, WIDTH_B), lambda i: (i // nseq, 0, i % nseq, 0)))
            dil_args.append(a)
    return pl.pallas_call(
        _out_kernel,
        grid=(t // tm,),
        in_specs=[rs(d), pl.BlockSpec((1, WIDTH_A, tm), lambda i: (i // nseq, 0, i % nseq)),
                  rs(WIDTH_B), rs(WIDTH_B), *dil_specs,
                  rs(WIDTH_C), cs(WIDTH_A), cs(WIDTH_B), cs(WIDTH_C), _layer_spec(w, layer)],
        out_specs=rs(d),
        out_shape=jax.ShapeDtypeStruct((t, d), F32),
        scratch_shapes=[pltpu.VMEM((2 * len(_DILATIONS) * WIDTH_B // LANES, tm, LANES), F32)],
        compiler_params=pltpu.CompilerParams(dimension_semantics=("parallel",),
                                             vmem_limit_bytes=VMEM_LIMIT),
    )(x2, oa, ob1, l1, *dil_args, oc, ga, gb, gc, w)


def _mlp_kernel(x_ref, g_ref, w1_ref, w2_ref, gf_ref, o_ref, *, final_norm, tf):
    tm = x_ref.shape[0]
    hm = tm // MLP_SUBTILES
    rows = [slice(sub * hm, (sub + 1) * hm) for sub in range(MLP_SUBTILES)]
    hs = [_rms(x_ref[rs, :], g_ref[...]).astype(BF16) for rs in rows]
    accs = [None] * MLP_SUBTILES
    for f in range(w1_ref.shape[1] // tf):
        cols = slice(f * tf, (f + 1) * tf)
        us = [jnp.maximum(_dot(h, w1_ref[:, cols]), 0.0) for h in hs]
        for sub, u in enumerate(us):
            y = _dot((u * u).astype(BF16), w2_ref[cols, :])
            accs[sub] = y if accs[sub] is None else accs[sub] + y
    for rs, acc in zip(rows, accs):
        y = x_ref[rs, :] + acc
        if final_norm:
            y = _rms(y, gf_ref[...])
        o_ref[rs, :] = y


def _mlp_call(x2, g, w1, w2, gf, final_norm, tm, tf, layer):
    t, d = x2.shape
    return pl.pallas_call(
        functools.partial(_mlp_kernel, final_norm=final_norm, tf=tf),
        grid=(t // tm,),
        in_specs=[
            pl.BlockSpec((tm, d), lambda i: (i, 0)),
            pl.BlockSpec((1, d), lambda i: (0, 0)),
            _layer_spec(w1, layer, pipeline_mode=pl.Buffered(1)),
            _layer_spec(w2, layer, pipeline_mode=pl.Buffered(1)),
            pl.BlockSpec((1, d), lambda i: (0, 0)),
        ],
        out_specs=pl.BlockSpec((tm, d), lambda i: (i, 0)),
        out_shape=jax.ShapeDtypeStruct((t, d), F32),
        compiler_params=pltpu.CompilerParams(dimension_semantics=("parallel",),
                                             vmem_limit_bytes=VMEM_LIMIT),
    )(x2, g, w1, w2, gf)


def _rotate_half_cols(w, half):
    return jnp.concatenate([-w[..., half:], w[..., :half]], axis=-1)


def _rope_tables(seq):
    pos = jnp.arange(seq, dtype=F32)

    def cos_sin(half):
        inv_freq = ROPE_THETA ** (-jnp.arange(half, dtype=F32) / half)
        ang = pos[:, None] * inv_freq[None, :]
        return jnp.cos(ang), jnp.sin(ang)

    ca, sa = cos_sin(QK_ROPE // 2)
    ca2, sa2 = jnp.concatenate([ca, ca], -1), jnp.concatenate([sa, sa], -1)
    ones = jnp.ones((seq, QK_NOPE), F32)
    zeros = jnp.zeros((seq, QK_NOPE), F32)
    tail = jnp.zeros((seq, LANES - QK_NOPE - QK_ROPE), F32)
    scale_a = (QK_NOPE + QK_ROPE) ** -0.5 * LOG2_E
    cosq = jnp.concatenate([ones, ca2, tail], -1) * scale_a
    sinq = jnp.concatenate([zeros, sa2, tail], -1) * scale_a
    cosk = jnp.concatenate([zeros, ca2, tail], -1)
    sink = jnp.concatenate([zeros, sa2, tail], -1)
    cb, sb = cos_sin(HEAD_DIM // 2)
    cosb = jnp.concatenate([cb, cb, cb, cb], -1)
    sinb = jnp.concatenate([-sb, sb, -sb, sb], -1)
    return jnp.concatenate([cosq, sinq, cosk, sink, cosb, sinb], -1)


def _layer_weights(w_in, w_uq, w_ukv):
    w_in, w_uq, w_ukv = w_in.astype(BF16), w_uq.astype(BF16), w_ukv.astype(BF16)
    d = w_in.shape[0]
    scale = HEAD_DIM ** -0.5
    c_b = Q_LORA + KV_LORA + QK_ROPE
    c_c = c_b + 3 * WIDTH_B
    w_kpe = w_in[:, Q_LORA + KV_LORA:c_b]

    def place(w):
        return jnp.concatenate([jnp.zeros((d, QK_NOPE), BF16), w, jnp.zeros((d, LANES - QK_NOPE - QK_ROPE), BF16)], -1)

    wbig = jnp.concatenate([
        w_in[:, :Q_LORA + KV_LORA],
        place(w_kpe), place(_rotate_half_cols(w_kpe, QK_ROPE // 2)),
        w_in[:, c_b:c_b + WIDTH_B] * scale, w_in[:, c_b + WIDTH_B:c_c],
        w_in[:, c_c:c_c + WIDTH_C] * scale, w_in[:, c_c + WIDTH_C:],
    ], -1)

    uq = w_uq.reshape(Q_LORA, HEADS_A, QK_NOPE + QK_ROPE)
    pad = jnp.zeros((Q_LORA, HEADS_A, LANES - QK_NOPE - QK_ROPE), BF16)
    uq_pad = jnp.concatenate([uq, pad], -1)
    uq_rot = jnp.concatenate([jnp.zeros((Q_LORA, HEADS_A, QK_NOPE), BF16),
                              _rotate_half_cols(uq[..., QK_NOPE:], QK_ROPE // 2), pad], -1)
    wuq = jnp.concatenate([uq_pad.reshape(Q_LORA, QK_A_PAD), uq_rot.reshape(Q_LORA, QK_A_PAD)], -1)

    ukv = w_ukv.reshape(KV_LORA, HEADS_A, QK_NOPE + HEAD_DIM)
    uk_pad = jnp.concatenate([ukv[..., :QK_NOPE], jnp.zeros((KV_LORA, HEADS_A, LANES - QK_NOPE), BF16)], -1)
    wukv = jnp.concatenate([uk_pad.reshape(KV_LORA, QK_A_PAD), ukv[..., QK_NOPE:].reshape(KV_LORA, WIDTH_A)],
                           -1)
    return wbig, wuq, wukv


def kernel(x, g_mix, w_in, q_norm, w_uq, kv_norm, w_ukv, rpb, out_norm_a, out_norm_b, out_norm_c, w_out, g_mlp,
           w_mlp_in, w_mlp_out, g_final):
    b, s, d = x.shape
    depth = w_in.shape[0]
    t = b * s
    assert s % (2 * BLOCKS_PER_STEP * GRID_W) == 0 and s // GRID_W >= NA_ROWS
    tm_proj = min(TILE_ROWS_PROJ, s)
    tm_out = min(TILE_ROWS_OUT, s)
    tm_mlp = min(TILE_ROWS_MLP, t)
    tq_a = min(TILE_Q_DENSE, s)
    tk_a = min(TILE_K_DENSE, s)

    tabs = _rope_tables(s)
    na_bias = _na_bias_call(rpb)
    row = lambda a: a.reshape(1, -1)

    wbig, wuq, wukv = jax.vmap(_layer_weights)(w_in, w_uq, w_ukv)
    w_out_b, w1_b, w2_b = w_out.astype(BF16), w_mlp_in.astype(BF16), w_mlp_out.astype(BF16)

    x2 = x.reshape(t, d)
    for l in range(depth):
        qa, ka, va, qb, kb, vb, qc, kc, vc, *dil_qkv = _proj_call(
            x2, row(g_mix[l]), wbig, row(q_norm[l]), wuq, row(kv_norm[l]), wukv, tabs, s, tm_proj, l)
        seq3 = lambda a: a.reshape(b, s, a.shape[-1])
        oa = _attn_a_call(qa, seq3(ka), va, tq_a, tk_a)
        ob1, l1, dilated = None, None, []
        for window, dil in DILATED_PAIRS:
            half = window // (2 * dil)
            if dil == 1:
                unit = lambda a: a.reshape(b, 1, s, WIDTH_B)
                o_i, lse_i = _attn_b_call(unit(qb), unit(kb), unit(vb), half, TILE_Q_BANDED)
                ob1, l1 = o_i.reshape(t, WIDTH_B), lse_i.reshape(t, WIDTH_B)
            else:
                di = 3 * _DILATIONS.index(dil)
                dilated.append(_attn_b_call(*dil_qkv[di:di + 3], half, TILE_Q_BANDED))
        oc = _attn_c_call(seq3(qc), seq3(kc), seq3(vc), na_bias, l).reshape(t, WIDTH_C)
        x2 = _out_call(x2, oa, ob1, l1, dilated, oc, row(out_norm_a[l]), row(out_norm_b[l]), row(out_norm_c[l]),
                       w_out_b, s, tm_out, l)
        x2 = _mlp_call(x2, row(g_mlp[l]), w1_b, w2_b, row(g_final), l == depth - 1, tm_mlp,
                       min(TILE_FF_MLP, w_mlp_in.shape[2]), l)
    return x2.reshape(b, s, d)
```

```python
import functools

import jax
import jax.numpy as jnp
from jax import lax
from jax.experimental import pallas as pl
from jax.experimental.pallas import tpu as pltpu

HEAD_DIM = 64
LANES = 128
BF16_ROWS = 16
HEADS_A = 6
HEADS_B = 6
HEADS_C = 4
Q_LORA = 256
KV_LORA = 128
QK_NOPE = 64
QK_ROPE = 32
DILATED_PAIRS = ((128, 1), (512, 4), (2048, 16))
_DILATIONS = tuple(dil for _, dil in DILATED_PAIRS if dil > 1)
GRID_W = 64
NA_ROWS = 8
NA_COLS = 16
ROPE_THETA = 10000.0
NORM_EPS = 1e-6
NEG_INF = -1e30
LOG2_E = 1.4426950408889634
PROJ_SUBTILES = 2
MLP_SUBTILES = 2
BLOCKS_PER_STEP = 4

WIDTH_A = HEADS_A * HEAD_DIM
WIDTH_B = HEADS_B * HEAD_DIM
WIDTH_C = HEADS_C * HEAD_DIM
QK_A_PAD = HEADS_A * LANES
RPB_PER_HEAD = (2 * NA_ROWS - 1) * (2 * NA_COLS - 1)

VMEM_LIMIT = 56 * 1024 * 1024

TILE_ROWS_PROJ = 512
TILE_ROWS_OUT = 1024
TILE_ROWS_MLP = 1024
TILE_FF_MLP = 1024
TILE_Q_DENSE = 256
TILE_K_DENSE = 512
TILE_Q_BANDED = 128

BF16 = jnp.bfloat16
F32 = jnp.float32


def _rms(x, g):
    return x * lax.rsqrt(jnp.mean(x * x, axis=-1, keepdims=True) + NORM_EPS) * g


def _dot(a, b):
    return jnp.dot(a, b, preferred_element_type=F32)


def _dot_nt(a, b):
    return lax.dot_general(a, b, (((1,), (1,)), ((), ())), preferred_element_type=F32)


def _lane_is_first_head(shape):
    return lax.broadcasted_iota(jnp.int32, shape, len(shape) - 1) < HEAD_DIM


def _stack_heads(q, first):
    return jnp.concatenate([_keep_head(q, first, 0), _keep_head(q, first, 1)], axis=0)


def _keep_head(q, first, j):
    zero = jnp.zeros_like(q)
    return jnp.where(first, q, zero) if j == 0 else jnp.where(first, zero, q)


_C_CQ = 0
_C_CKV = _C_CQ + Q_LORA
_C_KPE = _C_CKV + KV_LORA
_C_KPR = _C_KPE + LANES
_C_QB = _C_KPR + LANES
_C_KB = _C_QB + WIDTH_B
_C_VB = _C_KB + WIDTH_B
_C_QC = _C_VB + WIDTH_B
_C_KC = _C_QC + WIDTH_C
_C_VC = _C_KC + WIDTH_C
_C_END = _C_VC + WIDTH_C

_T_COSQ, _T_SINQ, _T_COSK, _T_SINK, _T_COSB, _T_SINB = range(6)


def _proj_kernel(x_ref, g_ref, wbig_ref, qn_ref, wuq_ref, kvn_ref, wukv_ref, tab_ref,
                 qa_ref, ka_ref, va_ref, qb_ref, kb_ref, vb_ref, qc_ref, kc_ref, vc_ref, *rest):
    dil_refs, stage_ref = rest[:-1], rest[-1]
    tm = x_ref.shape[0]
    hm = tm // PROJ_SUBTILES
    nblk = WIDTH_B // LANES
    first_half = (lax.broadcasted_iota(jnp.int32, (hm, LANES), 1) % HEAD_DIM) < HEAD_DIM // 2

    for sub in range(PROJ_SUBTILES):
        rs = slice(sub * hm, (sub + 1) * hm)

        def tab(i):
            return tab_ref[rs, i * LANES:(i + 1) * LANES]

        h = _rms(x_ref[rs, :], g_ref[...]).astype(BF16)
        proj = _dot(h, wbig_ref[...])

        cqn = _rms(proj[:, _C_CQ:_C_CKV], qn_ref[...]).astype(BF16)
        qa2 = _dot(cqn, wuq_ref[...])
        ckvn = _rms(proj[:, _C_CKV:_C_KPE], kvn_ref[...]).astype(BF16)
        kv2 = _dot(ckvn, wukv_ref[...])
        kpe = proj[:, _C_KPE:_C_KPR] * tab(_T_COSK) + proj[:, _C_KPR:_C_QB] * tab(_T_SINK)
        cosq, sinq = tab(_T_COSQ), tab(_T_SINQ)
        for hd in range(HEADS_A):
            sl = slice(hd * LANES, (hd + 1) * LANES)
            rot = slice(QK_A_PAD + hd * LANES, QK_A_PAD + (hd + 1) * LANES)
            qa_ref[0, sl, rs] = (qa2[:, sl] * cosq + qa2[:, rot] * sinq).T.astype(BF16)
            ka_ref[rs, sl] = (kv2[:, sl] + kpe).astype(BF16)
        for blk in range(WIDTH_A // LANES):
            sl = slice(blk * LANES, (blk + 1) * LANES)
            va_ref[0, sl, rs] = kv2[:, QK_A_PAD + blk * LANES:QK_A_PAD + (blk + 1) * LANES].T.astype(BF16)

        cosb, sinb = tab(_T_COSB), tab(_T_SINB)
        for ti, (src, dst) in enumerate(((_C_QB, qb_ref), (_C_KB, kb_ref), (_C_VB, vb_ref))):
            for blk in range(nblk):
                xb = proj[:, src + blk * LANES:src + (blk + 1) * LANES]
                if dst is not vb_ref:
                    swapped = jnp.where(first_half, pltpu.roll(xb, LANES - HEAD_DIM // 2, 1),
                                        pltpu.roll(xb, HEAD_DIM // 2, 1))
                    xb = xb * cosb + swapped * sinb
                    if dst is qb_ref:
                        xb = xb * LOG2_E
                dst[rs, blk * LANES:(blk + 1) * LANES] = xb.astype(BF16)
                stage_ref[ti * nblk + blk, rs, :] = xb
        for di, dil in enumerate(_DILATIONS):
            rows = hm // dil
            for ti in range(3):
                dref = dil_refs[di * 3 + ti]
                for r in range(dil):
                    for blk in range(nblk):
                        dref[0, r, sub * rows:(sub + 1) * rows, blk * LANES:(blk + 1) * LANES] = (
                            stage_ref[ti * nblk + blk, pl.ds(sub * hm + r, rows, stride=dil), :].astype(BF16))

        qc_ref[rs, :] = proj[:, _C_QC:_C_KC].astype(BF16)
        kc_ref[rs, :] = proj[:, _C_KC:_C_VC].astype(BF16)
        vc_ref[rs, :] = proj[:, _C_VC:_C_END].astype(BF16)


def _layer_spec(w, layer, **kwargs):
    return pl.BlockSpec((None,) + w.shape[1:], lambda i: (layer,) + (0,) * (w.ndim - 1), **kwargs)


def _proj_call(x2, g, wbig, qn, wuq, kvn, wukv, tabs, seq, tm, layer):
    t, d = x2.shape
    nseq = seq // tm
    row = lambda i: (i, 0)
    const = lambda i: (0, 0)
    widths = (QK_A_PAD, QK_A_PAD, WIDTH_A, WIDTH_B, WIDTH_B, WIDTH_B, WIDTH_C, WIDTH_C, WIDTH_C)
    out_specs = [pl.BlockSpec((tm, w), row) for w in widths]
    out_shape = [jax.ShapeDtypeStruct((t, w), BF16) for w in widths]
    for idx in (0, 2):
        out_specs[idx] = pl.BlockSpec((1, widths[idx], tm), lambda i: (i // nseq, 0, i % nseq))
        out_shape[idx] = jax.ShapeDtypeStruct((t // seq, widths[idx], seq), BF16)
    for dil in _DILATIONS:
        assert tm % (BF16_ROWS * dil * PROJ_SUBTILES) == 0
        for _ in range(3):
            out_specs.append(pl.BlockSpec((1, dil, tm // dil, WIDTH_B), lambda i: (i // nseq, 0, i % nseq, 0)))
            out_shape.append(jax.ShapeDtypeStruct((t // seq, dil, seq // dil, WIDTH_B), BF16))
    return pl.pallas_call(
        _proj_kernel,
        grid=(t // tm,),
        in_specs=[
            pl.BlockSpec((tm, d), row),
            pl.BlockSpec((1, d), const),
            _layer_spec(wbig, layer),
            pl.BlockSpec((1, Q_LORA), const),
            _layer_spec(wuq, layer),
            pl.BlockSpec((1, KV_LORA), const),
            _layer_spec(wukv, layer),
            pl.BlockSpec((tm, tabs.shape[1]), lambda i: (i % nseq, 0)),
        ],
        out_specs=out_specs,
        out_shape=out_shape,
        scratch_shapes=[pltpu.VMEM((3 * WIDTH_B // LANES, tm, LANES), F32)],
        compiler_params=pltpu.CompilerParams(dimension_semantics=("parallel",),
                                             vmem_limit_bytes=VMEM_LIMIT),
    )(x2, g, wbig, qn, wuq, kvn, wukv, tabs)


def _attn_a_kernel(qt_ref, k_ref, vt_ref, o_ref, a0_ref, a1_ref, b0_ref, b1_ref, *, tq, tk):
    s = k_ref.shape[1]
    nk, nq = s // tk, s // tq
    st_a, st_b = (a0_ref, a1_ref), (b0_ref, b1_ref)
    ones = jnp.ones((BF16_ROWS, tk), BF16)

    def scores(qb, c, st_ref):
        q0 = pl.multiple_of(qb * tq, tq)
        ks = pl.multiple_of(c * tk, tk)
        for j in range(2):
            st_ref[j] = _dot(k_ref[0, pl.ds(ks, tk), j * LANES:(j + 1) * LANES],
                             qt_ref[0, j * LANES:(j + 1) * LANES, pl.ds(q0, tq)])

    def accumulate(c, st_ref, carry):
        ks = pl.multiple_of(c * tk, tk)
        stats = []
        for j in range(2):
            m = carry[j][0]
            st = st_ref[j]
            m_new = jnp.maximum(m, jnp.max(st, axis=0, keepdims=True))
            stats.append((m_new, jnp.exp2(m - m_new), jnp.exp2(st - m_new).astype(BF16)))
        new = []
        for j in range(2):
            m_new, alpha, pt = stats[j]
            vt = jnp.concatenate([vt_ref[0, j * HEAD_DIM:(j + 1) * HEAD_DIM, pl.ds(ks, tk)], ones], axis=0)
            new.append((m_new, alpha * carry[j][1] + _dot(vt, pt)))
        return tuple(new)

    def half_step(qb, c, src, dst, carry, wrap=False):
        for u in range(2):
            if wrap:
                scores(jnp.minimum(qb + 1, nq - 1), u, dst[u])
            else:
                scores(qb, c + 2 + u, dst[u])
            carry = accumulate(c + u, src[u], carry)
        return carry

    def q_block(qb, _):
        def body(i, carry):
            carry = half_step(qb, 4 * i, st_a, st_b, carry)
            return half_step(qb, 4 * i + 2, st_b, st_a, carry)

        init = tuple((jnp.full((1, tq), NEG_INF, F32), jnp.zeros((HEAD_DIM + BF16_ROWS, tq), F32))
                     for _ in range(2))
        carry = lax.fori_loop(0, nk // 4 - 1, body, init)
        carry = half_step(qb, nk - 4, st_a, st_b, carry)
        res = half_step(qb, nk - 2, st_b, st_a, carry, wrap=True)
        q0 = pl.multiple_of(qb * tq, tq)
        for j in range(2):
            acc = res[j][1]
            o_ref[0, j * HEAD_DIM:(j + 1) * HEAD_DIM, pl.ds(q0, tq)] = (
                acc[:HEAD_DIM] / acc[HEAD_DIM:HEAD_DIM + 1]).astype(o_ref.dtype)
        return 0

    scores(0, 0, st_a[0])
    scores(0, 1, st_a[1])
    lax.fori_loop(0, nq, q_block, 0)


def _attn_a_call(qt, ka, vt, tq, tk):
    b, s, _ = ka.shape
    pairs = HEADS_A // 2
    assert s % (4 * tk) == 0
    return pl.pallas_call(
        functools.partial(_attn_a_kernel, tq=tq, tk=tk),
        grid=(b, pairs),
        in_specs=[
            pl.BlockSpec((1, 2 * LANES, s), lambda bi, p: (bi, p, 0)),
            pl.BlockSpec((1, s, 2 * LANES), lambda bi, p: (bi, 0, p)),
            pl.BlockSpec((1, LANES, s), lambda bi, p: (bi, p, 0)),
        ],
        out_specs=pl.BlockSpec((1, LANES, s), lambda bi, p: (bi, p, 0)),
        out_shape=jax.ShapeDtypeStruct((b, WIDTH_A, s), BF16),
        scratch_shapes=[pltpu.VMEM((2, tk, tq), F32) for _ in range(4)],
        compiler_params=pltpu.CompilerParams(dimension_semantics=("parallel", "parallel"),
                                             vmem_limit_bytes=VMEM_LIMIT),
    )(qt, ka, vt)


def _attn_b_kernel(q_ref, k_ref, v_ref, o_ref, lse_ref, band_ref, sa_ref, sb_ref, *, tq, half, n):
    total = q_ref.shape[0]
    kw = min(tq + 2 * half, n)
    first = _lane_is_first_head((tq, LANES))

    group = min(BLOCKS_PER_STEP, total // tq)
    diff = (lax.broadcasted_iota(jnp.int32, (tq, kw), 0) - lax.broadcasted_iota(jnp.int32, (tq, kw), 1))
    for var in range(band_ref.shape[0]):
        band_ref[var] = jnp.where(jnp.abs(diff + var * half) <= half, 0.0, NEG_INF)

    n_groups = total // (tq * group)
    ones = jnp.ones((kw, LANES), BF16)

    def window(g, u):
        q0 = pl.multiple_of((g * group + u) * tq, tq)
        seg0 = (q0 // n) * n
        return q0, pl.multiple_of(jnp.clip(q0 - half, seg0, seg0 + n - kw), half)

    def issue(g, st_ref):
        g = jnp.minimum(g, n_groups - 1)
        for u in range(group):
            q0, ks = window(g, u)
            st_ref[u] = _dot_nt(_stack_heads(q_ref[pl.ds(q0, tq), :], first), k_ref[pl.ds(ks, kw), :])

    def consume(g, st_ref):
        probs = []
        for u in range(group):
            q0, ks = window(g, u)
            band = band_ref[(q0 - ks) // half]
            row = []
            for j in range(2):
                s = st_ref[u, j * tq:(j + 1) * tq, :] + band
                m = jnp.max(s, axis=-1, keepdims=True)
                row.append((jnp.exp2(s - m).astype(BF16), m))
            probs.append(row)
        for u, row in enumerate(probs):
            q0, ks = window(g, u)
            v_aug = jnp.concatenate([v_ref[pl.ds(ks, kw), :], ones], axis=1)
            o2 = _dot(jnp.concatenate([row[0][0], row[1][0]], axis=0), v_aug)
            dens = [o2[j * tq:(j + 1) * tq, LANES:] for j in range(2)]
            outs = [o2[j * tq:(j + 1) * tq, :LANES] / dens[j] for j in range(2)]
            lses = [row[j][1] + jnp.log2(dens[j]) for j in range(2)]
            o_ref[pl.ds(q0, tq), :] = jnp.where(first, outs[0], outs[1]).astype(o_ref.dtype)
            lse_ref[pl.ds(q0, tq), :] = jnp.where(first, lses[0], lses[1])

    def body(it, carry):
        issue(2 * it + 1, sb_ref)
        consume(2 * it, sa_ref)
        issue(2 * it + 2, sa_ref)
        consume(2 * it + 1, sb_ref)
        return carry

    issue(0, sa_ref)
    lax.fori_loop(0, n_groups // 2, body, 0)


def _attn_b_call(q, k, v, half, tq):
    b, dil, n, w = q.shape
    pairs = w // LANES
    tq = min(tq, n)
    kw = min(tq + 2 * half, n)
    group = min(BLOCKS_PER_STEP, dil * n // tq)
    assert n % tq == 0 and (kw - tq) % half == 0 and (dil * n) % (2 * group * tq) == 0
    flat = lambda a: a.reshape(b, dil * n, w)
    spec = pl.BlockSpec((None, dil * n, LANES), lambda bi, p: (bi, 0, p))
    o, lse = pl.pallas_call(
        functools.partial(_attn_b_kernel, tq=tq, half=half, n=n),
        grid=(b, pairs),
        in_specs=[spec, spec, spec],
        out_specs=[spec, spec],
        out_shape=[jax.ShapeDtypeStruct((b, dil * n, w), BF16), jax.ShapeDtypeStruct((b, dil * n, w), F32)],
        scratch_shapes=[pltpu.VMEM(((kw - tq) // half + 1, tq, kw), F32),
                        pltpu.VMEM((group, 2 * tq, kw), F32), pltpu.VMEM((group, 2 * tq, kw), F32)],
        compiler_params=pltpu.CompilerParams(dimension_semantics=("parallel", "parallel"),
                                             vmem_limit_bytes=VMEM_LIMIT),
    )(flat(q), flat(k), flat(v))
    return o.reshape(q.shape), lse.reshape(q.shape)


def _na_bias_kernel(rpb_ref, o_ref):
    base = pl.program_id(0) * RPB_PER_HEAD
    shape = (GRID_W, LANES)
    lane = lax.broadcasted_iota(jnp.int32, shape, 1)
    p = lax.broadcasted_iota(jnp.int32, shape, 0)
    c = lane % GRID_W
    upper = lane >= GRID_W
    c_start = jnp.clip(p - NA_COLS // 2, 0, GRID_W - NA_COLS)
    col_ok = (c >= c_start) & (c < c_start + NA_COLS)
    dc = c - p + (NA_COLS - 1)
    n_dc = 2 * NA_COLS - 1
    for v in range(NA_ROWS):
        for m in range(NA_ROWS * GRID_W // LANES):
            a_lo = 2 * m - v + (NA_ROWS - 1)
            acc = jnp.full(shape, NEG_INF, F32)
            for b in range(n_dc):
                val = jnp.where(upper, rpb_ref[base + (a_lo + 1) * n_dc + b], rpb_ref[base + a_lo * n_dc + b])
                acc = jnp.where(dc == b, val, acc)
            o_ref[0, v, :, m * LANES:(m + 1) * LANES] = jnp.where(col_ok, acc, NEG_INF)


def _na_bias_call(rpb):
    nh = rpb.shape[0] * rpb.shape[1]
    return pl.pallas_call(
        _na_bias_kernel,
        grid=(nh,),
        in_specs=[pl.BlockSpec(memory_space=pltpu.SMEM)],
        out_specs=pl.BlockSpec((1, NA_ROWS, GRID_W, NA_ROWS * GRID_W), lambda g: (g, 0, 0, 0)),
        out_shape=jax.ShapeDtypeStruct((nh, NA_ROWS, GRID_W, NA_ROWS * GRID_W), F32),
    )(rpb.reshape(-1))


def _attn_c_kernel(q_ref, k_ref, v_ref, bias_ref, o_ref, sa_ref, sb_ref):
    rows = q_ref.shape[1] // GRID_W
    win = NA_ROWS * GRID_W
    first = _lane_is_first_head((GRID_W, LANES))
    group = sa_ref.shape[0]
    n_groups = rows // group
    ones = jnp.ones((win, LANES), BF16)

    def window(g, u):
        r = g * group + u
        r_start = jnp.clip(r - NA_ROWS // 2, 0, rows - NA_ROWS)
        return pl.multiple_of(r * GRID_W, GRID_W), pl.multiple_of(r_start * GRID_W, GRID_W), r - r_start

    def issue(g, st_ref):
        g = jnp.minimum(g, n_groups - 1)
        for u in range(group):
            q0, ks, _ = window(g, u)
            st_ref[u] = _dot_nt(_stack_heads(q_ref[0, pl.ds(q0, GRID_W), :], first), k_ref[0, pl.ds(ks, win), :])

    def consume(g, st_ref):
        probs = []
        for u in range(group):
            variant = window(g, u)[2]
            row = []
            for j in range(2):
                s = st_ref[u, j * GRID_W:(j + 1) * GRID_W, :] + bias_ref[j, variant]
                row.append(jnp.exp(s - jnp.max(s, axis=-1, keepdims=True)).astype(BF16))
            probs.append(row)
        for u, row in enumerate(probs):
            q0, ks, _ = window(g, u)
            v_aug = jnp.concatenate([v_ref[0, pl.ds(ks, win), :], ones], axis=1)
            o2 = _dot(jnp.concatenate(row, axis=0), v_aug)
            outs = [o2[j * GRID_W:(j + 1) * GRID_W, :LANES] / o2[j * GRID_W:(j + 1) * GRID_W, LANES:]
                    for j in range(2)]
            o_ref[0, pl.ds(q0, GRID_W), :] = jnp.where(first, outs[0], outs[1]).astype(o_ref.dtype)

    def body(it, carry):
        issue(2 * it + 1, sb_ref)
        consume(2 * it, sa_ref)
        issue(2 * it + 2, sa_ref)
        consume(2 * it + 1, sb_ref)
        return carry

    issue(0, sa_ref)
    lax.fori_loop(0, n_groups // 2, body, 0)


def _attn_c_call(q, k, v, bias, layer):
    b, s, w = q.shape
    pairs = w // LANES
    spec = pl.BlockSpec((1, s, LANES), lambda bi, p: (bi, 0, p))
    return pl.pallas_call(
        _attn_c_kernel,
        grid=(b, pairs),
        in_specs=[spec, spec, spec,
                  pl.BlockSpec((2,) + bias.shape[1:], lambda bi, p: (layer * pairs + p, 0, 0, 0))],
        out_specs=spec,
        out_shape=jax.ShapeDtypeStruct((b, s, w), BF16),
        scratch_shapes=[pltpu.VMEM((BLOCKS_PER_STEP, 2 * GRID_W, NA_ROWS * GRID_W), F32) for _ in range(2)],
        compiler_params=pltpu.CompilerParams(dimension_semantics=("parallel", "parallel"),
                                             vmem_limit_bytes=VMEM_LIMIT),
    )(q, k, v, bias)


def _out_kernel(x_ref, oa_ref, ob1_ref, l1_ref, *rest):
    nd = len(_DILATIONS)
    dil_refs = rest[:2 * nd]
    oc_ref, ga_ref, gb_ref, gc_ref, w_ref, o_ref, stage_ref = rest[2 * nd:]
    tm = x_ref.shape[0]
    nblk = WIDTH_B // LANES
    oa = jnp.concatenate([oa_ref[0, blk * LANES:(blk + 1) * LANES, :].astype(F32).T
                          for blk in range(WIDTH_A // LANES)], axis=-1)
    na = _rms(oa, ga_ref[...]).astype(BF16)
    outs, lses = [ob1_ref[...].astype(F32)], [l1_ref[...]]
    for di, dil in enumerate(_DILATIONS):
        rows = tm // dil
        for which, acc in ((0, outs), (1, lses)):
            src = dil_refs[2 * di + which]
            slot = (2 * di + which) * nblk
            for r in range(dil):
                for blk in range(nblk):
                    stage_ref[slot + blk, pl.ds(r, rows, stride=dil), :] = (
                        src[0, r, :, blk * LANES:(blk + 1) * LANES].astype(F32))
            acc.append(jnp.concatenate([stage_ref[slot + blk] for blk in range(nblk)], axis=-1))
    lmax = functools.reduce(jnp.maximum, lses)
    es = [jnp.exp2(l - lmax) for l in lses]
    ob = sum(e * o for e, o in zip(es, outs)) / sum(es)
    nb = _rms(ob, gb_ref[...]).astype(BF16)
    nc = _rms(oc_ref[...].astype(F32), gc_ref[...]).astype(BF16)
    acc = _dot(na, w_ref[0:WIDTH_A, :])
    acc += _dot(nb, w_ref[WIDTH_A:WIDTH_A + WIDTH_B, :])
    acc += _dot(nc, w_ref[WIDTH_A + WIDTH_B:, :])
    o_ref[...] = x_ref[...] + acc


def _out_call(x2, oa, ob1, l1, dilated, oc, ga, gb, gc, w, seq, tm, layer):
    t, d = x2.shape
    nseq = seq // tm
    row = lambda i: (i, 0)
    const = lambda i: (0, 0)
    rs = lambda width: pl.BlockSpec((tm, width), row)
    cs = lambda width: pl.BlockSpec((1, width), const)
    dil_specs, dil_args = [], []
    for dil, pair in zip(_DILATIONS, dilated):
        for a in pair:
            dil_specs.append(pl.BlockSpec((1, dil, tm // dil, WIDTH_B), lambda i: (i // nseq, 0, i % nseq, 0)))
            dil_args.append(a)
    return pl.pallas_call(
        _out_kernel,
        grid=(t // tm,),
        in_specs=[rs(d), pl.BlockSpec((1, WIDTH_A, tm), lambda i: (i // nseq, 0, i % nseq)),
                  rs(WIDTH_B), rs(WIDTH_B), *dil_specs,
                  rs(WIDTH_C), cs(WIDTH_A), cs(WIDTH_B), cs(WIDTH_C), _layer_spec(w, layer)],
        out_specs=rs(d),
        out_shape=jax.ShapeDtypeStruct((t, d), F32),
        scratch_shapes=[pltpu.VMEM((2 * len(_DILATIONS) * WIDTH_B // LANES, tm, LANES), F32)],
        compiler_params=pltpu.CompilerParams(dimension_semantics=("parallel",),
                                             vmem_limit_bytes=VMEM_LIMIT),
    )(x2, oa, ob1, l1, *dil_args, oc, ga, gb, gc, w)


def _mlp_kernel(x_ref, g_ref, w1_ref, w2_ref, gf_ref, o_ref, *, final_norm, tf):
    tm = x_ref.shape[0]
    hm = tm // MLP_SUBTILES
    rows = [slice(sub * hm, (sub + 1) * hm) for sub in range(MLP_SUBTILES)]
    hs = [_rms(x_ref[rs, :], g_ref[...]).astype(BF16) for rs in rows]
    accs = [None] * MLP_SUBTILES
    for f in range(w1_ref.shape[1] // tf):
        cols = slice(f * tf, (f + 1) * tf)
        us = [jnp.maximum(_dot(h, w1_ref[:, cols]), 0.0) for h in hs]
        for sub, u in enumerate(us):
            y = _dot((u * u).astype(BF16), w2_ref[cols, :])
            accs[sub] = y if accs[sub] is None else accs[sub] + y
    for rs, acc in zip(rows, accs):
        y = x_ref[rs, :] + acc
        if final_norm:
            y = _rms(y, gf_ref[...])
        o_ref[rs, :] = y


def _mlp_call(x2, g, w1, w2, gf, final_norm, tm, tf, layer):
    t, d = x2.shape
    return pl.pallas_call(
        functools.partial(_mlp_kernel, final_norm=final_norm, tf=tf),
        grid=(t // tm,),
        in_specs=[
            pl.BlockSpec((tm, d), lambda i: (i, 0)),
            pl.BlockSpec((1, d), lambda i: (0, 0)),
            _layer_spec(w1, layer, pipeline_mode=pl.Buffered(1)),
            _layer_spec(w2, layer, pipeline_mode=pl.Buffered(1)),
            pl.BlockSpec((1, d), lambda i: (0, 0)),
        ],
        out_specs=pl.BlockSpec((tm, d), lambda i: (i, 0)),
        out_shape=jax.ShapeDtypeStruct((t, d), F32),
        compiler_params=pltpu.CompilerParams(dimension_semantics=("parallel",),
                                             vmem_limit_bytes=VMEM_LIMIT),
    )(x2, g, w1, w2, gf)


def _rotate_half_cols(w, half):
    return jnp.concatenate([-w[..., half:], w[..., :half]], axis=-1)


def _rope_tables(seq):
    pos = jnp.arange(seq, dtype=F32)

    def cos_sin(half):
        inv_freq = ROPE_THETA ** (-jnp.arange(half, dtype=F32) / half)
        ang = pos[:, None] * inv_freq[None, :]
        return jnp.cos(ang), jnp.sin(ang)

    ca, sa = cos_sin(QK_ROPE // 2)
    ca2, sa2 = jnp.concatenate([ca, ca], -1), jnp.concatenate([sa, sa], -1)
    ones = jnp.ones((seq, QK_NOPE), F32)
    zeros = jnp.zeros((seq, QK_NOPE), F32)
    tail = jnp.zeros((seq, LANES - QK_NOPE - QK_ROPE), F32)
    scale_a = (QK_NOPE + QK_ROPE) ** -0.5 * LOG2_E
    cosq = jnp.concatenate([ones, ca2, tail], -1) * scale_a
    sinq = jnp.concatenate([zeros, sa2, tail], -1) * scale_a
    cosk = jnp.concatenate([zeros, ca2, tail], -1)
    sink = jnp.concatenate([zeros, sa2, tail], -1)
    cb, sb = cos_sin(HEAD_DIM // 2)
    cosb = jnp.concatenate([cb, cb, cb, cb], -1)
    sinb = jnp.concatenate([-sb, sb, -sb, sb], -1)
    return jnp.concatenate([cosq, sinq, cosk, sink, cosb, sinb], -1)


def _layer_weights(w_in, w_uq, w_ukv):
    w_in, w_uq, w_ukv = w_in.astype(BF16), w_uq.astype(BF16), w_ukv.astype(BF16)
    d = w_in.shape[0]
    scale = HEAD_DIM ** -0.5
    c_b = Q_LORA + KV_LORA + QK_ROPE
    c_c = c_b + 3 * WIDTH_B
    w_kpe = w_in[:, Q_LORA + KV_LORA:c_b]

    def place(w):
        return jnp.concatenate([jnp.zeros((d, QK_NOPE), BF16), w, jnp.zeros((d, LANES - QK_NOPE - QK_ROPE), BF16)], -1)

    wbig = jnp.concatenate([
        w_in[:, :Q_LORA + KV_LORA],
        place(w_kpe), place(_rotate_half_cols(w_kpe, QK_ROPE // 2)),
        w_in[:, c_b:c_b + WIDTH_B] * scale, w_in[:, c_b + WIDTH_B:c_c],
        w_in[:, c_c:c_c + WIDTH_C] * scale, w_in[:, c_c + WIDTH_C:],
    ], -1)

    uq = w_uq.reshape(Q_LORA, HEADS_A, QK_NOPE + QK_ROPE)
    pad = jnp.zeros((Q_LORA, HEADS_A, LANES - QK_NOPE - QK_ROPE), BF16)
    uq_pad = jnp.concatenate([uq, pad], -1)
    uq_rot = jnp.concatenate([jnp.zeros((Q_LORA, HEADS_A, QK_NOPE), BF16),
                              _rotate_half_cols(uq[..., QK_NOPE:], QK_ROPE // 2), pad], -1)
    wuq = jnp.concatenate([uq_pad.reshape(Q_LORA, QK_A_PAD), uq_rot.reshape(Q_LORA, QK_A_PAD)], -1)

    ukv = w_ukv.reshape(KV_LORA, HEADS_A, QK_NOPE + HEAD_DIM)
    uk_pad = jnp.concatenate([ukv[..., :QK_NOPE], jnp.zeros((KV_LORA, HEADS_A, LANES - QK_NOPE), BF16)], -1)
    wukv = jnp.concatenate([uk_pad.reshape(KV_LORA, QK_A_PAD), ukv[..., QK_NOPE:].reshape(KV_LORA, WIDTH_A)],
                           -1)
    return wbig, wuq, wukv


def kernel(x, g_mix, w_in, q_norm, w_uq, kv_norm, w_ukv, rpb, out_norm_a, out_norm_b, out_norm_c, w_out, g_mlp,
           w_mlp_in, w_mlp_out, g_final):
    b, s, d = x.shape
    depth = w_in.shape[0]
    t = b * s
    assert s % (2 * BLOCKS_PER_STEP * GRID_W) == 0 and s // GRID_W >= NA_ROWS
    tm_proj = min(TILE_ROWS_PROJ, s)
    tm_out = min(TILE_ROWS_OUT, s)
    tm_mlp = min(TILE_ROWS_MLP, t)
    tq_a = min(TILE_Q_DENSE, s)
    tk_a = min(TILE_K_DENSE, s)

    tabs = _rope_tables(s)
    na_bias = _na_bias_call(rpb)
    row = lambda a: a.reshape(1, -1)

    wbig, wuq, wukv = jax.vmap(_layer_weights)(w_in, w_uq, w_ukv)
    w_out_b, w1_b, w2_b = w_out.astype(BF16), w_mlp_in.astype(BF16), w_mlp_out.astype(BF16)

    x2 = x.reshape(t, d)
    for l in range(depth):
        qa, ka, va, qb, kb, vb, qc, kc, vc, *dil_qkv = _proj_call(
            x2, row(g_mix[l]), wbig, row(q_norm[l]), wuq, row(kv_norm[l]), wukv, tabs, s, tm_proj, l)
        seq3 = lambda a: a.reshape(b, s, a.shape[-1])
        oa = _attn_a_call(qa, seq3(ka), va, tq_a, tk_a)
        ob1, l1, dilated = None, None, []
        for window, dil in DILATED_PAIRS:
            half = window // (2 * dil)
            if dil == 1:
                unit = lambda a: a.reshape(b, 1, s, WIDTH_B)
                o_i, lse_i = _attn_b_call(unit(qb), unit(kb), unit(vb), half, TILE_Q_BANDED)
                ob1, l1 = o_i.reshape(t, WIDTH_B), lse_i.reshape(t, WIDTH_B)
            else:
                di = 3 * _DILATIONS.index(dil)
                dilated.append(_attn_b_call(*dil_qkv[di:di + 3], half, TILE_Q_BANDED))
        oc = _attn_c_call(seq3(qc), seq3(kc), seq3(vc), na_bias, l).reshape(t, WIDTH_C)
        x2 = _out_call(x2, oa, ob1, l1, dilated, oc, row(out_norm_a[l]), row(out_norm_b[l]), row(out_norm_c[l]),
                       w_out_b, s, tm_out, l)
        x2 = _mlp_call(x2, row(g_mlp[l]), w1_b, w2_b, row(g_final), l == depth - 1, tm_mlp,
                       min(TILE_FF_MLP, w_mlp_in.shape[2]), l)
    return x2.reshape(b, s, d)
```

```python
import functools

import jax
import jax.numpy as jnp
from jax import lax
from jax.experimental import pallas as pl
from jax.experimental.pallas import tpu as pltpu

HEAD_DIM = 64
LANES = 128
BF16_ROWS = 16
HEADS_A = 6
HEADS_B = 6
HEADS_C = 4
Q_LORA = 256
KV_LORA = 128
QK_NOPE = 64
QK_ROPE = 32
DILATED_PAIRS = ((128, 1), (512, 4), (2048, 16))
_DILATIONS = tuple(dil for _, dil in DILATED_PAIRS if dil > 1)
GRID_W = 64
NA_ROWS = 8
NA_COLS = 16
ROPE_THETA = 10000.0
NORM_EPS = 1e-6
NEG_INF = -1e30
LOG2_E = 1.4426950408889634
PROJ_SUBTILES = 2
MLP_SUBTILES = 2
BLOCKS_PER_STEP = 4

WIDTH_A = HEADS_A * HEAD_DIM
WIDTH_B = HEADS_B * HEAD_DIM
WIDTH_C = HEADS_C * HEAD_DIM
QK_A_PAD = HEADS_A * LANES
RPB_PER_HEAD = (2 * NA_ROWS - 1) * (2 * NA_COLS - 1)

VMEM_LIMIT = 56 * 1024 * 1024

TILE_ROWS_PROJ = 512
TILE_ROWS_OUT = 1024
TILE_ROWS_MLP = 1024
TILE_FF_MLP = 1024
TILE_Q_DENSE = 256
TILE_K_DENSE = 512
TILE_Q_BANDED = 128

BF16 = jnp.bfloat16
F32 = jnp.float32


def _rms(x, g):
    return x * lax.rsqrt(jnp.mean(x * x, axis=-1, keepdims=True) + NORM_EPS) * g


def _dot(a, b):
    return jnp.dot(a, b, preferred_element_type=F32)


def _dot_nt(a, b):
    return lax.dot_general(a, b, (((1,), (1,)), ((), ())), preferred_element_type=F32)


def _lane_is_first_head(shape):
    return lax.broadcasted_iota(jnp.int32, shape, len(shape) - 1) < HEAD_DIM


def _stack_heads(q, first):
    return jnp.concatenate([_keep_head(q, first, 0), _keep_head(q, first, 1)], axis=0)


def _keep_head(q, first, j):
    zero = jnp.zeros_like(q)
    return jnp.where(first, q, zero) if j == 0 else jnp.where(first, zero, q)


_C_CQ = 0
_C_CKV = _C_CQ + Q_LORA
_C_KPE = _C_CKV + KV_LORA
_C_KPR = _C_KPE + LANES
_C_QB = _C_KPR + LANES
_C_KB = _C_QB + WIDTH_B
_C_VB = _C_KB + WIDTH_B
_C_QC = _C_VB + WIDTH_B
_C_KC = _C_QC + WIDTH_C
_C_VC = _C_KC + WIDTH_C
_C_END = _C_VC + WIDTH_C

_T_COSQ, _T_SINQ, _T_COSK, _T_SINK, _T_COSB, _T_SINB = range(6)


def _proj_kernel(x_ref, g_ref, wbig_ref, qn_ref, wuq_ref, kvn_ref, wukv_ref, tab_ref,
                 qa_ref, ka_ref, va_ref, qb_ref, kb_ref, vb_ref, qc_ref, kc_ref, vc_ref, *rest):
    dil_refs, stage_ref = rest[:-1], rest[-1]
    tm = x_ref.shape[0]
    hm = tm // PROJ_SUBTILES
    nblk = WIDTH_B // LANES
    first_half = (lax.broadcasted_iota(jnp.int32, (hm, LANES), 1) % HEAD_DIM) < HEAD_DIM // 2

    for sub in range(PROJ_SUBTILES):
        rs = slice(sub * hm, (sub + 1) * hm)

        def tab(i):
            return tab_ref[rs, i * LANES:(i + 1) * LANES]

        h = _rms(x_ref[rs, :], g_ref[...]).astype(BF16)
        proj = _dot(h, wbig_ref[...])

        cqn = _rms(proj[:, _C_CQ:_C_CKV], qn_ref[...]).astype(BF16)
        qa2 = _dot(cqn, wuq_ref[...])
        ckvn = _rms(proj[:, _C_CKV:_C_KPE], kvn_ref[...]).astype(BF16)
        kv2 = _dot(ckvn, wukv_ref[...])
        kpe = proj[:, _C_KPE:_C_KPR] * tab(_T_COSK) + proj[:, _C_KPR:_C_QB] * tab(_T_SINK)
        cosq, sinq = tab(_T_COSQ), tab(_T_SINQ)
        for hd in range(HEADS_A):
            sl = slice(hd * LANES, (hd + 1) * LANES)
            rot = slice(QK_A_PAD + hd * LANES, QK_A_PAD + (hd + 1) * LANES)
            qa_ref[0, sl, rs] = (qa2[:, sl] * cosq + qa2[:, rot] * sinq).T.astype(BF16)
            ka_ref[rs, sl] = (kv2[:, sl] + kpe).astype(BF16)
        for blk in range(WIDTH_A // LANES):
            sl = slice(blk * LANES, (blk + 1) * LANES)
            va_ref[0, sl, rs] = kv2[:, QK_A_PAD + blk * LANES:QK_A_PAD + (blk + 1) * LANES].T.astype(BF16)

        cosb, sinb = tab(_T_COSB), tab(_T_SINB)
        for ti, (src, dst) in enumerate(((_C_QB, qb_ref), (_C_KB, kb_ref), (_C_VB, vb_ref))):
            for blk in range(nblk):
                xb = proj[:, src + blk * LANES:src + (blk + 1) * LANES]
                if dst is not vb_ref:
                    swapped = jnp.where(first_half, pltpu.roll(xb, LANES - HEAD_DIM // 2, 1),
                                        pltpu.roll(xb, HEAD_DIM // 2, 1))
                    xb = xb * cosb + swapped * sinb
                    if dst is qb_ref:
                        xb = xb * LOG2_E
                dst[rs, blk * LANES:(blk + 1) * LANES] = xb.astype(BF16)
                stage_ref[ti * nblk + blk, rs, :] = xb
        for di, dil in enumerate(_DILATIONS):
            rows = hm // dil
            for ti in range(3):
                dref = dil_refs[di * 3 + ti]
                for r in range(dil):
                    for blk in range(nblk):
                        dref[0, r, sub * rows:(sub + 1) * rows, blk * LANES:(blk + 1) * LANES] = (
                            stage_ref[ti * nblk + blk, pl.ds(sub * hm + r, rows, stride=dil), :].astype(BF16))

        qc_ref[rs, :] = (proj[:, _C_QC:_C_KC] * LOG2_E).astype(BF16)
        kc_ref[rs, :] = proj[:, _C_KC:_C_VC].astype(BF16)
        vc_ref[rs, :] = proj[:, _C_VC:_C_END].astype(BF16)


def _layer_spec(w, layer, **kwargs):
    return pl.BlockSpec((None,) + w.shape[1:], lambda i: (layer,) + (0,) * (w.ndim - 1), **kwargs)


def _proj_call(x2, g, wbig, qn, wuq, kvn, wukv, tabs, seq, tm, layer):
    t, d = x2.shape
    nseq = seq // tm
    row = lambda i: (i, 0)
    const = lambda i: (0, 0)
    widths = (QK_A_PAD, QK_A_PAD, WIDTH_A, WIDTH_B, WIDTH_B, WIDTH_B, WIDTH_C, WIDTH_C, WIDTH_C)
    out_specs = [pl.BlockSpec((tm, w), row) for w in widths]
    out_shape = [jax.ShapeDtypeStruct((t, w), BF16) for w in widths]
    for idx in (0, 2):
        out_specs[idx] = pl.BlockSpec((1, widths[idx], tm), lambda i: (i // nseq, 0, i % nseq))
        out_shape[idx] = jax.ShapeDtypeStruct((t // seq, widths[idx], seq), BF16)
    for dil in _DILATIONS:
        assert tm % (BF16_ROWS * dil * PROJ_SUBTILES) == 0
        for _ in range(3):
            out_specs.append(pl.BlockSpec((1, dil, tm // dil, WIDTH_B), lambda i: (i // nseq, 0, i % nseq, 0)))
            out_shape.append(jax.ShapeDtypeStruct((t // seq, dil, seq // dil, WIDTH_B), BF16))
    return pl.pallas_call(
        _proj_kernel,
        grid=(t // tm,),
        in_specs=[
            pl.BlockSpec((tm, d), row),
            pl.BlockSpec((1, d), const),
            _layer_spec(wbig, layer),
            pl.BlockSpec((1, Q_LORA), const),
            _layer_spec(wuq, layer),
            pl.BlockSpec((1, KV_LORA), const),
            _layer_spec(wukv, layer),
            pl.BlockSpec((tm, tabs.shape[1]), lambda i: (i % nseq, 0)),
        ],
        out_specs=out_specs,
        out_shape=out_shape,
        scratch_shapes=[pltpu.VMEM((3 * WIDTH_B // LANES, tm, LANES), F32)],
        compiler_params=pltpu.CompilerParams(dimension_semantics=("parallel",),
                                             vmem_limit_bytes=VMEM_LIMIT),
    )(x2, g, wbig, qn, wuq, kvn, wukv, tabs)


def _attn_a_kernel(qt_ref, k_ref, vt_ref, o_ref, a0_ref, a1_ref, b0_ref, b1_ref, *, tq, tk):
    s = k_ref.shape[1]
    nk, nq = s // tk, s // tq
    st_a, st_b = (a0_ref, a1_ref), (b0_ref, b1_ref)
    ones = jnp.ones((BF16_ROWS, tk), BF16)

    def scores(qb, c, st_ref):
        q0 = pl.multiple_of(qb * tq, tq)
        ks = pl.multiple_of(c * tk, tk)
        for j in range(2):
            st_ref[j] = _dot(k_ref[0, pl.ds(ks, tk), j * LANES:(j + 1) * LANES],
                             qt_ref[0, j * LANES:(j + 1) * LANES, pl.ds(q0, tq)])

    def accumulate(c, st_ref, carry):
        ks = pl.multiple_of(c * tk, tk)
        stats = []
        for j in range(2):
            m = carry[j][0]
            st = st_ref[j]
            m_new = jnp.maximum(m, jnp.max(st, axis=0, keepdims=True))
            stats.append((m_new, jnp.exp2(m - m_new), jnp.exp2(st - m_new).astype(BF16)))
        new = []
        for j in range(2):
            m_new, alpha, pt = stats[j]
            vt = jnp.concatenate([vt_ref[0, j * HEAD_DIM:(j + 1) * HEAD_DIM, pl.ds(ks, tk)], ones], axis=0)
            new.append((m_new, alpha * carry[j][1] + _dot(vt, pt)))
        return tuple(new)

    def half_step(qb, c, src, dst, carry, wrap=False):
        for u in range(2):
            if wrap:
                scores(jnp.minimum(qb + 1, nq - 1), u, dst[u])
            else:
                scores(qb, c + 2 + u, dst[u])
            carry = accumulate(c + u, src[u], carry)
        return carry

    def q_block(qb, _):
        def body(i, carry):
            carry = half_step(qb, 4 * i, st_a, st_b, carry)
            return half_step(qb, 4 * i + 2, st_b, st_a, carry)

        init = tuple((jnp.full((1, tq), NEG_INF, F32), jnp.zeros((HEAD_DIM + BF16_ROWS, tq), F32))
                     for _ in range(2))
        carry = lax.fori_loop(0, nk // 4 - 1, body, init)
        carry = half_step(qb, nk - 4, st_a, st_b, carry)
        res = half_step(qb, nk - 2, st_b, st_a, carry, wrap=True)
        q0 = pl.multiple_of(qb * tq, tq)
        for j in range(2):
            acc = res[j][1]
            o_ref[0, j * HEAD_DIM:(j + 1) * HEAD_DIM, pl.ds(q0, tq)] = (
                acc[:HEAD_DIM] / acc[HEAD_DIM:HEAD_DIM + 1]).astype(o_ref.dtype)
        return 0

    scores(0, 0, st_a[0])
    scores(0, 1, st_a[1])
    lax.fori_loop(0, nq, q_block, 0)


def _attn_a_call(qt, ka, vt, tq, tk):
    b, s, _ = ka.shape
    pairs = HEADS_A // 2
    assert s % (4 * tk) == 0
    return pl.pallas_call(
        functools.partial(_attn_a_kernel, tq=tq, tk=tk),
        grid=(b, pairs),
        in_specs=[
            pl.BlockSpec((1, 2 * LANES, s), lambda bi, p: (bi, p, 0)),
            pl.BlockSpec((1, s, 2 * LANES), lambda bi, p: (bi, 0, p)),
            pl.BlockSpec((1, LANES, s), lambda bi, p: (bi, p, 0)),
        ],
        out_specs=pl.BlockSpec((1, LANES, s), lambda bi, p: (bi, p, 0)),
        out_shape=jax.ShapeDtypeStruct((b, WIDTH_A, s), BF16),
        scratch_shapes=[pltpu.VMEM((2, tk, tq), F32) for _ in range(4)],
        compiler_params=pltpu.CompilerParams(dimension_semantics=("parallel", "parallel"),
                                             vmem_limit_bytes=VMEM_LIMIT),
    )(qt, ka, vt)


def _attn_b_kernel(q_ref, k_ref, v_ref, o_ref, lse_ref, band_ref, sa_ref, sb_ref, *, tq, half, n):
    total = q_ref.shape[0]
    kw = min(tq + 2 * half, n)
    first = _lane_is_first_head((tq, LANES))

    group = min(BLOCKS_PER_STEP, total // tq)
    diff = (lax.broadcasted_iota(jnp.int32, (tq, kw), 0) - lax.broadcasted_iota(jnp.int32, (tq, kw), 1))
    for var in range(band_ref.shape[0]):
        band_ref[var] = jnp.where(jnp.abs(diff + var * half) <= half, 0.0, NEG_INF)

    n_groups = total // (tq * group)
    ones = jnp.ones((kw, LANES), BF16)

    def window(g, u):
        q0 = pl.multiple_of((g * group + u) * tq, tq)
        seg0 = (q0 // n) * n
        return q0, pl.multiple_of(jnp.clip(q0 - half, seg0, seg0 + n - kw), half)

    def issue(g, st_ref):
        g = jnp.minimum(g, n_groups - 1)
        for u in range(group):
            q0, ks = window(g, u)
            st_ref[u] = _dot_nt(_stack_heads(q_ref[pl.ds(q0, tq), :], first), k_ref[pl.ds(ks, kw), :])

    def consume(g, st_ref):
        probs = []
        for u in range(group):
            q0, ks = window(g, u)
            band = band_ref[(q0 - ks) // half]
            row = []
            for j in range(2):
                s = st_ref[u, j * tq:(j + 1) * tq, :] + band
                m = jnp.max(s, axis=-1, keepdims=True)
                row.append((jnp.exp2(s - m).astype(BF16), m))
            probs.append(row)
        for u, row in enumerate(probs):
            q0, ks = window(g, u)
            v_aug = jnp.concatenate([v_ref[pl.ds(ks, kw), :], ones], axis=1)
            o2 = _dot(jnp.concatenate([row[0][0], row[1][0]], axis=0), v_aug)
            dens = [o2[j * tq:(j + 1) * tq, LANES:] for j in range(2)]
            outs = [o2[j * tq:(j + 1) * tq, :LANES] / dens[j] for j in range(2)]
            lses = [row[j][1] + jnp.log2(dens[j]) for j in range(2)]
            o_ref[pl.ds(q0, tq), :] = jnp.where(first, outs[0], outs[1]).astype(o_ref.dtype)
            lse_ref[pl.ds(q0, tq), :] = jnp.where(first, lses[0], lses[1])

    def body(it, carry):
        issue(2 * it + 1, sb_ref)
        consume(2 * it, sa_ref)
        issue(2 * it + 2, sa_ref)
        consume(2 * it + 1, sb_ref)
        return carry

    issue(0, sa_ref)
    lax.fori_loop(0, n_groups // 2, body, 0)


def _attn_b_call(q, k, v, half, tq):
    b, dil, n, w = q.shape
    pairs = w // LANES
    tq = min(tq, n)
    kw = min(tq + 2 * half, n)
    group = min(BLOCKS_PER_STEP, dil * n // tq)
    assert n % tq == 0 and (kw - tq) % half == 0 and (dil * n) % (2 * group * tq) == 0
    flat = lambda a: a.reshape(b, dil * n, w)
    spec = pl.BlockSpec((None, dil * n, LANES), lambda bi, p: (bi, 0, p))
    o, lse = pl.pallas_call(
        functools.partial(_attn_b_kernel, tq=tq, half=half, n=n),
        grid=(b, pairs),
        in_specs=[spec, spec, spec],
        out_specs=[spec, spec],
        out_shape=[jax.ShapeDtypeStruct((b, dil * n, w), BF16), jax.ShapeDtypeStruct((b, dil * n, w), F32)],
        scratch_shapes=[pltpu.VMEM(((kw - tq) // half + 1, tq, kw), F32),
                        pltpu.VMEM((group, 2 * tq, kw), F32), pltpu.VMEM((group, 2 * tq, kw), F32)],
        compiler_params=pltpu.CompilerParams(dimension_semantics=("parallel", "parallel"),
                                             vmem_limit_bytes=VMEM_LIMIT),
    )(flat(q), flat(k), flat(v))
    return o.reshape(q.shape), lse.reshape(q.shape)


def _na_bias_kernel(rpb_ref, o_ref):
    base = pl.program_id(0) * RPB_PER_HEAD
    shape = (GRID_W, LANES)
    lane = lax.broadcasted_iota(jnp.int32, shape, 1)
    p = lax.broadcasted_iota(jnp.int32, shape, 0)
    c = lane % GRID_W
    upper = lane >= GRID_W
    c_start = jnp.clip(p - NA_COLS // 2, 0, GRID_W - NA_COLS)
    col_ok = (c >= c_start) & (c < c_start + NA_COLS)
    dc = c - p + (NA_COLS - 1)
    n_dc = 2 * NA_COLS - 1
    for v in range(NA_ROWS):
        for m in range(NA_ROWS * GRID_W // LANES):
            a_lo = 2 * m - v + (NA_ROWS - 1)
            acc = jnp.full(shape, NEG_INF, F32)
            for b in range(n_dc):
                val = jnp.where(upper, rpb_ref[base + (a_lo + 1) * n_dc + b], rpb_ref[base + a_lo * n_dc + b])
                acc = jnp.where(dc == b, val, acc)
            o_ref[0, v, :, m * LANES:(m + 1) * LANES] = jnp.where(col_ok, acc * LOG2_E, NEG_INF)


def _na_bias_call(rpb):
    nh = rpb.shape[0] * rpb.shape[1]
    return pl.pallas_call(
        _na_bias_kernel,
        grid=(nh,),
        in_specs=[pl.BlockSpec(memory_space=pltpu.SMEM)],
        out_specs=pl.BlockSpec((1, NA_ROWS, GRID_W, NA_ROWS * GRID_W), lambda g: (g, 0, 0, 0)),
        out_shape=jax.ShapeDtypeStruct((nh, NA_ROWS, GRID_W, NA_ROWS * GRID_W), F32),
    )(rpb.reshape(-1))


def _attn_c_kernel(q_ref, k_ref, v_ref, bias_ref, o_ref, sa_ref, sb_ref):
    rows = q_ref.shape[1] // GRID_W
    win = NA_ROWS * GRID_W
    first = _lane_is_first_head((GRID_W, LANES))
    group = sa_ref.shape[0]
    n_groups = rows // group
    ones = jnp.ones((win, LANES), BF16)

    def window(g, u):
        r = g * group + u
        r_start = jnp.clip(r - NA_ROWS // 2, 0, rows - NA_ROWS)
        return pl.multiple_of(r * GRID_W, GRID_W), pl.multiple_of(r_start * GRID_W, GRID_W), r - r_start

    def issue(g, st_ref):
        g = jnp.minimum(g, n_groups - 1)
        for u in range(group):
            q0, ks, _ = window(g, u)
            st_ref[u] = _dot_nt(_stack_heads(q_ref[0, pl.ds(q0, GRID_W), :], first), k_ref[0, pl.ds(ks, win), :])

    def consume(g, st_ref):
        probs = []
        for u in range(group):
            variant = window(g, u)[2]
            row = []
            for j in range(2):
                s = st_ref[u, j * GRID_W:(j + 1) * GRID_W, :] + bias_ref[j, variant]
                row.append(jnp.exp2(s - jnp.max(s, axis=-1, keepdims=True)).astype(BF16))
            probs.append(row)
        for u, row in enumerate(probs):
            q0, ks, _ = window(g, u)
            v_aug = jnp.concatenate([v_ref[0, pl.ds(ks, win), :], ones], axis=1)
            o2 = _dot(jnp.concatenate(row, axis=0), v_aug)
            outs = [o2[j * GRID_W:(j + 1) * GRID_W, :LANES] / o2[j * GRID_W:(j + 1) * GRID_W, LANES:]
                    for j in range(2)]
            o_ref[0, pl.ds(q0, GRID_W), :] = jnp.where(first, outs[0], outs[1]).astype(o_ref.dtype)

    def body(it, carry):
        issue(2 * it + 1, sb_ref)
        consume(2 * it, sa_ref)
        issue(2 * it + 2, sa_ref)
        consume(2 * it + 1, sb_ref)
        return carry

    issue(0, sa_ref)
    lax.fori_loop(0, n_groups // 2, body, 0)


def _attn_c_call(q, k, v, bias, layer):
    b, s, w = q.shape
    pairs = w // LANES
    spec = pl.BlockSpec((1, s, LANES), lambda bi, p: (bi, 0, p))
    return pl.pallas_call(
        _attn_c_kernel,
        grid=(b, pairs),
        in_specs=[spec, spec, spec,
                  pl.BlockSpec((2,) + bias.shape[1:], lambda bi, p: (layer * pairs + p, 0, 0, 0))],
        out_specs=spec,
        out_shape=jax.ShapeDtypeStruct((b, s, w), BF16),
        scratch_shapes=[pltpu.VMEM((BLOCKS_PER_STEP, 2 * GRID_W, NA_ROWS * GRID_W), F32) for _ in range(2)],
        compiler_params=pltpu.CompilerParams(dimension_semantics=("parallel", "parallel"),
                                             vmem_limit_bytes=VMEM_LIMIT),
    )(q, k, v, bias)


def _out_kernel(x_ref, oa_ref, ob1_ref, l1_ref, *rest):
    nd = len(_DILATIONS)
    dil_refs = rest[:2 * nd]
    oc_ref, ga_ref, gb_ref, gc_ref, w_ref, o_ref, stage_ref = rest[2 * nd:]
    tm = x_ref.shape[0]
    nblk = WIDTH_B // LANES
    oa = jnp.concatenate([oa_ref[0, blk * LANES:(blk + 1) * LANES, :].astype(F32).T
                          for blk in range(WIDTH_A // LANES)], axis=-1)
    na = _rms(oa, ga_ref[...]).astype(BF16)
    outs, lses = [ob1_ref[...].astype(F32)], [l1_ref[...]]
    for di, dil in enumerate(_DILATIONS):
        rows = tm // dil
        for which, acc in ((0, outs), (1, lses)):
            src = dil_refs[2 * di + which]
            slot = (2 * di + which) * nblk
            for r in range(dil):
                for blk in range(nblk):
                    stage_ref[slot + blk, pl.ds(r, rows, stride=dil), :] = (
                        src[0, r, :, blk * LANES:(blk + 1) * LANES].astype(F32))
            acc.append(jnp.concatenate([stage_ref[slot + blk] for blk in range(nblk)], axis=-1))
    lmax = functools.reduce(jnp.maximum, lses)
    es = [jnp.exp2(l - lmax) for l in lses]
    ob = sum(e * o for e, o in zip(es, outs)) / sum(es)
    nb = _rms(ob, gb_ref[...]).astype(BF16)
    nc = _rms(oc_ref[...].astype(F32), gc_ref[...]).astype(BF16)
    acc = _dot(na, w_ref[0:WIDTH_A, :])
    acc += _dot(nb, w_ref[WIDTH_A:WIDTH_A + WIDTH_B, :])
    acc += _dot(nc, w_ref[WIDTH_A + WIDTH_B:, :])
    o_ref[...] = x_ref[...] + acc


def _out_call(x2, oa, ob1, l1, dilated, oc, ga, gb, gc, w, seq, tm, layer):
    t, d = x2.shape
    nseq = seq // tm
    row = lambda i: (i, 0)
    const = lambda i: (0, 0)
    rs = lambda width: pl.BlockSpec((tm, width), row)
    cs = lambda width: pl.BlockSpec((1, width), const)
    dil_specs, dil_args = [], []
    for dil, pair in zip(_DILATIONS, dilated):
        for a in pair:
            dil_specs.append(pl.BlockSpec((1, dil, tm // dil, WIDTH_B), lambda i: (i // nseq, 0, i % nseq, 0)))
            dil_args.append(a)
    return pl.pallas_call(
        _out_kernel,
        grid=(t // tm,),
        in_specs=[rs(d), pl.BlockSpec((1, WIDTH_A, tm), lambda i: (i // nseq, 0, i % nseq)),
                  rs(WIDTH_B), rs(WIDTH_B), *dil_specs,
                  rs(WIDTH_C), cs(WIDTH_A), cs(WIDTH_B), cs(WIDTH_C), _layer_spec(w, layer)],
        out_specs=rs(d),
        out_shape=jax.ShapeDtypeStruct((t, d), F32),
        scratch_shapes=[pltpu.VMEM((2 * len(_DILATIONS) * WIDTH_B // LANES, tm, LANES), F32)],
        compiler_params=pltpu.CompilerParams(dimension_semantics=("parallel",),
                                             vmem_limit_bytes=VMEM_LIMIT),
    )(x2, oa, ob1, l1, *dil_args, oc, ga, gb, gc, w)


def _mlp_kernel(x_ref, g_ref, w1_ref, w2_ref, gf_ref, o_ref, *, final_norm, tf):
    tm = x_ref.shape[0]
    hm = tm // MLP_SUBTILES
    rows = [slice(sub * hm, (sub + 1) * hm) for sub in range(MLP_SUBTILES)]
    hs = [_rms(x_ref[rs, :], g_ref[...]).astype(BF16) for rs in rows]
    accs = [None] * MLP_SUBTILES
    for f in range(w1_ref.shape[1] // tf):
        cols = slice(f * tf, (f + 1) * tf)
        us = [jnp.maximum(_dot(h, w1_ref[:, cols]), 0.0) for h in hs]
        for sub, u in enumerate(us):
            y = _dot((u * u).astype(BF16), w2_ref[cols, :])
            accs[sub] = y if accs[sub] is None else accs[sub] + y
    for rs, acc in zip(rows, accs):
        y = x_ref[rs, :] + acc
        if final_norm:
            y = _rms(y, gf_ref[...])
        o_ref[rs, :] = y


def _mlp_call(x2, g, w1, w2, gf, final_norm, tm, tf, layer):
    t, d = x2.shape
    return pl.pallas_call(
        functools.partial(_mlp_kernel, final_norm=final_norm, tf=tf),
        grid=(t // tm,),
        in_specs=[
            pl.BlockSpec((tm, d), lambda i: (i, 0)),
            pl.BlockSpec((1, d), lambda i: (0, 0)),
            _layer_spec(w1, layer, pipeline_mode=pl.Buffered(1)),
            _layer_spec(w2, layer, pipeline_mode=pl.Buffered(1)),
            pl.BlockSpec((1, d), lambda i: (0, 0)),
        ],
        out_specs=pl.BlockSpec((tm, d), lambda i: (i, 0)),
        out_shape=jax.ShapeDtypeStruct((t, d), F32),
        compiler_params=pltpu.CompilerParams(dimension_semantics=("parallel",),
                                             vmem_limit_bytes=VMEM_LIMIT),
    )(x2, g, w1, w2, gf)


def _rotate_half_cols(w, half):
    return jnp.concatenate([-w[..., half:], w[..., :half]], axis=-1)


def _rope_tables(seq):
    pos = jnp.arange(seq, dtype=F32)

    def cos_sin(half):
        inv_freq = ROPE_THETA ** (-jnp.arange(half, dtype=F32) / half)
        ang = pos[:, None] * inv_freq[None, :]
        return jnp.cos(ang), jnp.sin(ang)

    ca, sa = cos_sin(QK_ROPE // 2)
    ca2, sa2 = jnp.concatenate([ca, ca], -1), jnp.concatenate([sa, sa], -1)
    ones = jnp.ones((seq, QK_NOPE), F32)
    zeros = jnp.zeros((seq, QK_NOPE), F32)
    tail = jnp.zeros((seq, LANES - QK_NOPE - QK_ROPE), F32)
    scale_a = (QK_NOPE + QK_ROPE) ** -0.5 * LOG2_E
    cosq = jnp.concatenate([ones, ca2, tail], -1) * scale_a
    sinq = jnp.concatenate([zeros, sa2, tail], -1) * scale_a
    cosk = jnp.concatenate([zeros, ca2, tail], -1)
    sink = jnp.concatenate([zeros, sa2, tail], -1)
    cb, sb = cos_sin(HEAD_DIM // 2)
    cosb = jnp.concatenate([cb, cb, cb, cb], -1)
    sinb = jnp.concatenate([-sb, sb, -sb, sb], -1)
    return jnp.concatenate([cosq, sinq, cosk, sink, cosb, sinb], -1)


def _layer_weights(w_in, w_uq, w_ukv):
    w_in, w_uq, w_ukv = w_in.astype(BF16), w_uq.astype(BF16), w_ukv.astype(BF16)
    d = w_in.shape[0]
    scale = HEAD_DIM ** -0.5
    c_b = Q_LORA + KV_LORA + QK_ROPE
    c_c = c_b + 3 * WIDTH_B
    w_kpe = w_in[:, Q_LORA + KV_LORA:c_b]

    def place(w):
        return jnp.concatenate([jnp.zeros((d, QK_NOPE), BF16), w, jnp.zeros((d, LANES - QK_NOPE - QK_ROPE), BF16)], -1)

    wbig = jnp.concatenate([
        w_in[:, :Q_LORA + KV_LORA],
        place(w_kpe), place(_rotate_half_cols(w_kpe, QK_ROPE // 2)),
        w_in[:, c_b:c_b + WIDTH_B] * scale, w_in[:, c_b + WIDTH_B:c_c],
        w_in[:, c_c:c_c + WIDTH_C] * scale, w_in[:, c_c + WIDTH_C:],
    ], -1)

    uq = w_uq.reshape(Q_LORA, HEADS_A, QK_NOPE + QK_ROPE)
    pad = jnp.zeros((Q_LORA, HEADS_A, LANES - QK_NOPE - QK_ROPE), BF16)
    uq_pad = jnp.concatenate([uq, pad], -1)
    uq_rot = jnp.concatenate([jnp.zeros((Q_LORA, HEADS_A, QK_NOPE), BF16),
                              _rotate_half_cols(uq[..., QK_NOPE:], QK_ROPE // 2), pad], -1)
    wuq = jnp.concatenate([uq_pad.reshape(Q_LORA, QK_A_PAD), uq_rot.reshape(Q_LORA, QK_A_PAD)], -1)

    ukv = w_ukv.reshape(KV_LORA, HEADS_A, QK_NOPE + HEAD_DIM)
    uk_pad = jnp.concatenate([ukv[..., :QK_NOPE], jnp.zeros((KV_LORA, HEADS_A, LANES - QK_NOPE), BF16)], -1)
    wukv = jnp.concatenate([uk_pad.reshape(KV_LORA, QK_A_PAD), ukv[..., QK_NOPE:].reshape(KV_LORA, WIDTH_A)],
                           -1)
    return wbig, wuq, wukv


def kernel(x, g_mix, w_in, q_norm, w_uq, kv_norm, w_ukv, rpb, out_norm_a, out_norm_b, out_norm_c, w_out, g_mlp,
           w_mlp_in, w_mlp_out, g_final):
    b, s, d = x.shape
    depth = w_in.shape[0]
    t = b * s
    assert s % (2 * BLOCKS_PER_STEP * GRID_W) == 0 and s // GRID_W >= NA_ROWS
    tm_proj = min(TILE_ROWS_PROJ, s)
    tm_out = min(TILE_ROWS_OUT, s)
    tm_mlp = min(TILE_ROWS_MLP, t)
    tq_a = min(TILE_Q_DENSE, s)
    tk_a = min(TILE_K_DENSE, s)

    tabs = _rope_tables(s)
    na_bias = _na_bias_call(rpb)
    row = lambda a: a.reshape(1, -1)

    wbig, wuq, wukv = jax.vmap(_layer_weights)(w_in, w_uq, w_ukv)
    w_out_b, w1_b, w2_b = w_out.astype(BF16), w_mlp_in.astype(BF16), w_mlp_out.astype(BF16)

    x2 = x.reshape(t, d)
    for l in range(depth):
        qa, ka, va, qb, kb, vb, qc, kc, vc, *dil_qkv = _proj_call(
            x2, row(g_mix[l]), wbig, row(q_norm[l]), wuq, row(kv_norm[l]), wukv, tabs, s, tm_proj, l)
        seq3 = lambda a: a.reshape(b, s, a.shape[-1])
        oa = _attn_a_call(qa, seq3(ka), va, tq_a, tk_a)
        ob1, l1, dilated = None, None, []
        for window, dil in DILATED_PAIRS:
            half = window // (2 * dil)
            if dil == 1:
                unit = lambda a: a.reshape(b, 1, s, WIDTH_B)
                o_i, lse_i = _attn_b_call(unit(qb), unit(kb), unit(vb), half, TILE_Q_BANDED)
                ob1, l1 = o_i.reshape(t, WIDTH_B), lse_i.reshape(t, WIDTH_B)
            else:
                di = 3 * _DILATIONS.index(dil)
                dilated.append(_attn_b_call(*dil_qkv[di:di + 3], half, TILE_Q_BANDED))
        oc = _attn_c_call(seq3(qc), seq3(kc), seq3(vc), na_bias, l).reshape(t, WIDTH_C)
        x2 = _out_call(x2, oa, ob1, l1, dilated, oc, row(out_norm_a[l]), row(out_norm_b[l]), row(out_norm_c[l]),
                       w_out_b, s, tm_out, l)
        x2 = _mlp_call(x2, row(g_mlp[l]), w1_b, w2_b, row(g_final), l == depth - 1, tm_mlp,
                       min(TILE_FF_MLP, w_mlp_in.shape[2]), l)
    return x2.reshape(b, s, d)
```
